```python
import math
import jax, jax.numpy as jnp
from jax import lax
import numpy as np

D_MODEL = 2048
BATCH = 16
SEQ = 2048
DEPTH = 2
DEC_BATCH = 32
DEC_SEQ = 64
PAST_LEN = 2048

CHUNK = 64
PLE_DIM = 256
D_FF = 5632
N_EVEN = (DEPTH + 1) // 2
N_ODD = DEPTH // 2
POOL_WINDOWS = (2, 4, 8, 16)
POOL_WIDTH = D_MODEL // 4
POOL_GROUP_DIM = POOL_WIDTH // len(POOL_WINDOWS)
POOL_HIST = max(POOL_WINDOWS) - 1
DA_VDIM = 128
DA_HALF = DA_VDIM // 2
DA_WIDTH = D_MODEL - POOL_WIDTH
DA_HEADS = DA_WIDTH // DA_VDIM
N_BUCKETS = 32
MAX_DISTANCE = 128
IN_EVEN = POOL_WIDTH + 3 * DA_WIDTH
HG_DK = 128
HG_DV = 128
HG_WIDTH = D_MODEL
HG_HEADS = HG_WIDTH // HG_DK
IN_ODD = 4 * HG_WIDTH
REC_BLOCK = 16
ALPHA = (2 * DEPTH) ** 0.25
BETA = (8 * DEPTH) ** -0.25
EPS = 1e-5
NEG = -1e30

kernel_name = 'hybrid_stream_pool_diffattn_hgrn2_step'


def layer_norm(x, g, b):
    xf = x.astype(jnp.float32)
    mu = jnp.mean(xf, axis=-1, keepdims=True)
    var = jnp.mean(jnp.square(xf - mu), axis=-1, keepdims=True)
    y = (xf - mu) * lax.rsqrt(var + EPS) * g.astype(jnp.float32) + b.astype(jnp.float32)
    return y.astype(x.dtype)


def rms_norm(x, g):
    xf = x.astype(jnp.float32)
    return xf * lax.rsqrt(jnp.mean(jnp.square(xf), axis=-1, keepdims=True) + EPS) * g.astype(jnp.float32)


def swiglu(x, wg, wu, wd):
    return (jax.nn.silu(x @ wg) * (x @ wu)) @ wd


def t5_bucket(rel):
    nb = N_BUCKETS // 2
    max_exact = nb // 2
    n = jnp.abs(rel)
    nf = jnp.maximum(n, 1).astype(jnp.float32)
    large = max_exact + (jnp.log(nf / max_exact) / math.log(MAX_DISTANCE / max_exact)
                         * (nb - max_exact)).astype(jnp.int32)
    large = jnp.minimum(large, nb - 1)
    return jnp.where(rel > 0, nb, 0) + jnp.where(n < max_exact, n, large)


def pool_mixer(u, hist, start_pos, w_grp, scale):
    B, L, _ = u.shape
    full = jnp.concatenate([hist, u], axis=1).astype(jnp.float32)
    cs = jnp.cumsum(full, axis=1)
    cs = jnp.concatenate([jnp.zeros_like(cs[:, :1]), cs], axis=1)
    pos = start_pos + jnp.arange(L)
    outs = []
    for g, w in enumerate(POOL_WINDOWS):
        sl = slice(g * POOL_GROUP_DIM, (g + 1) * POOL_GROUP_DIM)
        s = cs[:, POOL_HIST + 1:POOL_HIST + 1 + L, sl] - cs[:, POOL_HIST + 1 - w:POOL_HIST + 1 - w + L, sl]
        cnt = jnp.minimum(pos + 1, w).astype(jnp.float32)[None, :, None]
        outs.append(s / cnt - full[:, POOL_HIST:, sl])
    pooled = jnp.stack(outs, axis=2)
    y = jnp.einsum('blgc,gcd->blgd', pooled, w_grp.astype(jnp.float32)).reshape(B, L, POOL_WIDTH)
    y = y * scale.astype(jnp.float32)
    return y.astype(u.dtype), full[:, -POOL_HIST:].astype(u.dtype)


def diff_attention(q1, q2, k1, k2, v, q_pos, k_pos, rel_bias, lam):
    B, Lq, H, _ = q1.shape
    bq = 128 if Lq % 128 == 0 else Lq
    nb = Lq // bq
    scale = DA_HALF ** -0.5
    k_chunk = k_pos // CHUNK

    def blocks(t):
        return t.reshape(B, nb, bq, *t.shape[2:]).swapaxes(0, 1)

    def one(args):
        qb1, qb2, qp = args
        rel = k_pos[None, :] - qp[:, None]
        bias = rel_bias[t5_bucket(rel)].transpose(2, 0, 1).astype(jnp.float32)
        mask = k_chunk[None, :] <= (qp // CHUNK)[:, None]

        def probs(qb, k):
            s = jnp.einsum('bqhd,bkhd->bhqk', qb, k).astype(jnp.float32) * scale + bias
            return jax.nn.softmax(jnp.where(mask, s, NEG), axis=-1)

        a = probs(qb1, k1) - lam * probs(qb2, k2)
        return jnp.einsum('bhqk,bkhe->bqhe', a.astype(v.dtype), v)

    o = lax.map(one, (blocks(q1), blocks(q2), q_pos.reshape(nb, bq)))
    return o.swapaxes(0, 1).reshape(B, Lq, H, v.shape[-1])


def even_mixer(x, k_cache, v_cache, pool_hist, layer_idx, w_in, w_out, pool_w, pool_scale,
               lam_q1, lam_k1, lam_q2, lam_k2, norm_g, rel_bias):
    B, L, _ = x.shape
    h = x @ w_in
    u = h[..., :POOL_WIDTH]
    q = h[..., POOL_WIDTH:POOL_WIDTH + DA_WIDTH].reshape(B, L, DA_HEADS, 2 * DA_HALF)
    k = h[..., POOL_WIDTH + DA_WIDTH:POOL_WIDTH + 2 * DA_WIDTH].reshape(B, L, DA_HEADS, 2 * DA_HALF)
    v = h[..., POOL_WIDTH + 2 * DA_WIDTH:].reshape(B, L, DA_HEADS, DA_VDIM)
    past = k_cache.shape[1]
    pool_out, new_hist = pool_mixer(u, pool_hist, past, pool_w, pool_scale)
    k_all = jnp.concatenate([k_cache, k], axis=1)
    v_all = jnp.concatenate([v_cache, v], axis=1)
    q_pos = past + jnp.arange(L, dtype=jnp.int32)
    k_pos = jnp.arange(past + L, dtype=jnp.int32)
    lam_init = 0.8 - 0.6 * math.exp(-0.3 * layer_idx)
    f32 = jnp.float32
    lam = (jnp.exp(jnp.sum(lam_q1.astype(f32) * lam_k1.astype(f32)))
           - jnp.exp(jnp.sum(lam_q2.astype(f32) * lam_k2.astype(f32))) + lam_init)
    o = diff_attention(q[..., :DA_HALF], q[..., DA_HALF:], k_all[..., :DA_HALF], k_all[..., DA_HALF:],
                       v_all, q_pos, k_pos, rel_bias, lam)
    o = (rms_norm(o, norm_g) * (1.0 - lam_init)).reshape(B, L, DA_WIDTH).astype(x.dtype)
    mix = jnp.concatenate([pool_out, o], axis=-1) @ w_out
    return mix, k, v, new_hist


def hgrn_scan(q, k, v, log_f, s0):
    B, L, H, DK = q.shape
    DV = v.shape[-1]
    blk = max(d for d in range(1, REC_BLOCK + 1) if L % d == 0)
    n = L // blk
    causal = jnp.tril(jnp.ones((blk, blk), bool))[None, :, :, None, None]

    def blocks(t):
        return t.reshape(B, n, blk, H, t.shape[-1]).swapaxes(0, 1)

    def step(S, inp):
        qc, kc, vc, gc = inp
        b = jnp.cumsum(gc, axis=1)
        o_inter = jnp.einsum('bthd,bhde->bthe', qc * jnp.exp(b), S)
        dec = jnp.exp(jnp.where(causal, b[:, :, None] - b[:, None, :], -jnp.inf))
        A = jnp.einsum('btshd,bshd->bhts', qc[:, :, None] * dec, kc)
        o_intra = jnp.einsum('bhts,bshe->bthe', A, vc)
        b_last = b[:, -1]
        S = (jnp.exp(b_last)[..., None] * S
             + jnp.einsum('bshd,bshe->bhde', kc * jnp.exp(b_last[:, None] - b), vc))
        return S, o_inter + o_intra

    S, o = lax.scan(step, s0, (blocks(q), blocks(k), blocks(v), blocks(log_f)))
    return o.swapaxes(0, 1).reshape(B, L, H, DV), S


def odd_mixer(x, s0, w_in, w_out, norm_g, lb):
    B, L, _ = x.shape
    f32 = jnp.float32
    h = (x @ w_in).reshape(B, L, 4, HG_HEADS, HG_DK)
    q = jax.nn.silu(h[:, :, 0].astype(f32))
    z = h[:, :, 1].astype(f32)
    inp = h[:, :, 2].astype(f32)
    gate = jax.nn.silu(h[:, :, 3].astype(f32))
    lb = lb.reshape(HG_HEADS, HG_DK)
    log_1mlb = jnp.log1p(-lb)
    log_f = jnp.logaddexp(jnp.log(lb), log_1mlb + jax.nn.log_sigmoid(z))
    k = jnp.exp(log_1mlb + jax.nn.log_sigmoid(-z))
    o, S = hgrn_scan(q, k, inp, log_f, s0.astype(f32))
    o = (rms_norm(o, norm_g) * gate).reshape(B, L, HG_WIDTH).astype(x.dtype)
    return o @ w_out, S.astype(x.dtype)


def trunk(x, p, k_cache, v_cache, pool_hist, hg_state, ln_g, ln_b, w_ffn_gate, w_ffn_up, w_ffn_down,
          w_ple_gate, w_ple_up, w_in_even, w_out_even, pool_w, pool_scale, lam_q1, lam_k1, lam_q2, lam_k2,
          diff_norm_g, rel_bias, w_in_odd, w_out_odd, hgrn_norm_g, lb_all):
    new_k, new_v, new_pool, new_s = [], [], [], []
    for i in range(DEPTH):
        x = layer_norm(ALPHA * x + 0.5 * swiglu(x, w_ffn_gate[i, 0], w_ffn_up[i, 0], w_ffn_down[i, 0]),
                       ln_g[i, 0], ln_b[i, 0])
        if i % 2 == 0:
            e = i // 2
            mix, k, v, ph = even_mixer(x, k_cache[e], v_cache[e], pool_hist[e], i, w_in_even[e], w_out_even[e],
                                       pool_w[e], pool_scale[e], lam_q1[e], lam_k1[e], lam_q2[e], lam_k2[e],
                                       diff_norm_g[e], rel_bias)
            new_k.append(k)
            new_v.append(v)
            new_pool.append(ph)
        else:
            o = i // 2
            mix, s = odd_mixer(x, hg_state[o], w_in_odd[o], w_out_odd[o], hgrn_norm_g[o], lb_all[i])
            new_s.append(s)
        x = layer_norm(ALPHA * x + mix, ln_g[i, 1], ln_b[i, 1])
        x = layer_norm(ALPHA * x + 0.5 * swiglu(x, w_ffn_gate[i, 1], w_ffn_up[i, 1], w_ffn_down[i, 1]),
                       ln_g[i, 2], ln_b[i, 2])
        x = x + jax.nn.sigmoid(x @ w_ple_gate[i]) * (p[i] @ w_ple_up[i])
    return x, jnp.stack(new_k), jnp.stack(new_v), jnp.stack(new_pool), jnp.stack(new_s)


def setup_inputs(seed: int = 0) -> dict:
    key = jax.random.key(seed)
    ks = iter(jax.random.split(key, 40))

    def nrm(shape, scale):
        return jax.random.normal(next(ks), shape, jnp.float32) * scale

    D = D_MODEL
    return {
        'x_prompt': nrm((BATCH, SEQ, D), 1.0),
        'x_sample': nrm((DEC_BATCH, DEC_SEQ, D), 1.0),
        'cache_diff_k': nrm((N_EVEN, DEC_BATCH, PAST_LEN, DA_HEADS, 2 * DA_HALF), 1.0),
        'cache_diff_v': nrm((N_EVEN, DEC_BATCH, PAST_LEN, DA_HEADS, DA_VDIM), 1.0),
        'state_pool': nrm((N_EVEN, DEC_BATCH, POOL_HIST, POOL_WIDTH), 1.0),
        'state_hgrn': nrm((N_ODD, DEC_BATCH, HG_HEADS, HG_DK, HG_DV), 0.5),
        'p_prompt': nrm((DEPTH, BATCH, SEQ, PLE_DIM), 1.0),
        'p_sample': nrm((DEPTH, DEC_BATCH, DEC_SEQ, PLE_DIM), 1.0),
        'ln_g': 1.0 + nrm((DEPTH, 3, D), 0.05),
        'ln_b': nrm((DEPTH, 3, D), 0.02),
        'w_ffn_gate': nrm((DEPTH, 2, D, D_FF), D ** -0.5),
        'w_ffn_up': nrm((DEPTH, 2, D, D_FF), D ** -0.5),
        'w_ffn_down': nrm((DEPTH, 2, D_FF, D), D_FF ** -0.5 * BETA),
        'w_ple_gate': nrm((DEPTH, D, D), D ** -0.5),
        'w_ple_up': nrm((DEPTH, PLE_DIM, D), PLE_DIM ** -0.5),
        'w_in_even': nrm((N_EVEN, D, IN_EVEN), D ** -0.5),
        'w_out_even': nrm((N_EVEN, POOL_WIDTH + DA_WIDTH, D), (POOL_WIDTH + DA_WIDTH) ** -0.5 * BETA),
        'pool_w': nrm((N_EVEN, len(POOL_WINDOWS), POOL_GROUP_DIM, POOL_GROUP_DIM), POOL_GROUP_DIM ** -0.5),
        'pool_scale': 1.0 + nrm((N_EVEN, POOL_WIDTH), 0.1),
        'lam_q1': nrm((N_EVEN, DA_HALF), 0.1),
        'lam_k1': nrm((N_EVEN, DA_HALF), 0.1),
        'lam_q2': nrm((N_EVEN, DA_HALF), 0.1),
        'lam_k2': nrm((N_EVEN, DA_HALF), 0.1),
        'diff_norm_g': 1.0 + nrm((N_EVEN, DA_VDIM), 0.05),
        'rel_bias': nrm((N_BUCKETS, DA_HEADS), 0.5),
        'w_in_odd': nrm((N_ODD, D, IN_ODD), D ** -0.5),
        'w_out_odd': nrm((N_ODD, HG_WIDTH, D), HG_WIDTH ** -0.5 * BETA),
        'hgrn_norm_g': 1.0 + nrm((N_ODD, HG_DV), 0.05),
        'hgrn_lb_logits': nrm((DEPTH, HG_WIDTH), 0.5),
    }


def reference(x_prompt, x_sample, cache_diff_k, cache_diff_v, state_pool, state_hgrn, p_prompt, p_sample,
              ln_g, ln_b, w_ffn_gate, w_ffn_up, w_ffn_down, w_ple_gate, w_ple_up, w_in_even, w_out_even,
              pool_w, pool_scale, lam_q1, lam_k1, lam_q2, lam_k2, diff_norm_g, rel_bias, w_in_odd, w_out_odd,
              hgrn_norm_g, hgrn_lb_logits):
    lbp = jax.nn.softmax(hgrn_lb_logits.astype(jnp.float32), axis=0)
    lb_all = jnp.cumsum(lbp, axis=0) - lbp[0]
    weights = (ln_g, ln_b, w_ffn_gate, w_ffn_up, w_ffn_down, w_ple_gate, w_ple_up, w_in_even, w_out_even,
               pool_w, pool_scale, lam_q1, lam_k1, lam_q2, lam_k2, diff_norm_g, rel_bias, w_in_odd, w_out_odd,
               hgrn_norm_g, lb_all)
    B = x_prompt.shape[0]
    dt = x_prompt.dtype
    empty_k = jnp.zeros((N_EVEN, B, 0, DA_HEADS, 2 * DA_HALF), dt)
    empty_v = jnp.zeros((N_EVEN, B, 0, DA_HEADS, DA_VDIM), dt)
    zero_pool = jnp.zeros((N_EVEN, B, POOL_HIST, POOL_WIDTH), dt)
    zero_s = jnp.zeros((N_ODD, B, HG_HEADS, HG_DK, HG_DV), dt)
    y_prompt, k_p, v_p, pool_p, s_p = trunk(x_prompt, p_prompt, empty_k, empty_v, zero_pool, zero_s, *weights)
    y_sample, k_s, v_s, pool_s, s_s = trunk(x_sample, p_sample, cache_diff_k, cache_diff_v, state_pool,
                                            state_hgrn, *weights)
    return (y_prompt, y_sample, k_p, v_p, k_s, v_s, pool_p, pool_s, s_p, s_s)
```

```python
import functools
import math

import jax
import jax.numpy as jnp
from jax import lax
from jax.experimental import pallas as pl
from jax.experimental.pallas import tpu as pltpu

F32 = jnp.float32
BF16 = jnp.bfloat16

CHUNK = 64
POOL_WINDOWS = (2, 4, 8, 16)
POOL_HIST = max(POOL_WINDOWS) - 1
N_BUCKETS = 32
MAX_DISTANCE = 128
EPS = 1e-5
NEG = -1e30
LANES = 128

VMEM_LIMIT = 56 * 1024 * 1024


def _cparams(*sem):
    return pltpu.CompilerParams(dimension_semantics=sem, vmem_limit_bytes=VMEM_LIMIT)


def _pick(n, pref):
    if n <= pref:
        return n
    t = pref
    while n % t:
        t //= 2
    return t


def _layer_norm(y, g, b):
    mu = jnp.mean(y, axis=-1, keepdims=True)
    d = y - mu
    var = jnp.mean(d * d, axis=-1, keepdims=True)
    return d * lax.rsqrt(var + EPS) * g + b


def _dot(a, b):
    return jnp.dot(a, b, preferred_element_type=F32)


def _dot_nt(a, b):
    return lax.dot_general(a, b, (((1,), (1,)), ((), ())), preferred_element_type=F32)


def _ffn_ln_body(x_ref, wg_ref, wu_ref, wd_ref, g_ref, b_ref, o_ref, xb_ref, acc_ref, *, alpha):
    j = pl.program_id(1)

    @pl.when(j == 0)
    def _():
        xb_ref[...] = x_ref[...].astype(BF16)
        acc_ref[...] = jnp.zeros_like(acc_ref)

    xb = xb_ref[...]
    hg = _dot(xb, wg_ref[...])
    hu = _dot(xb, wu_ref[...])
    act = hg * jax.nn.sigmoid(hg) * hu
    acc_ref[...] += _dot(act.astype(BF16), wd_ref[...])

    @pl.when(j == pl.num_programs(1) - 1)
    def _():
        y = alpha * x_ref[...] + 0.5 * acc_ref[...]
        o_ref[...] = _layer_norm(y, g_ref[...], b_ref[...])


def _ffn_ln(x, wg, wu, wd, g, b, li, si, alpha, tm=512, tf=512):
    T, D = x.shape
    F = wg.shape[-1]
    tm = _pick(T, tm)
    tf = _pick(F, tf)
    return pl.pallas_call(
        functools.partial(_ffn_ln_body, alpha=alpha),
        grid=(T // tm, F // tf),
        in_specs=[
            pl.BlockSpec((tm, D), lambda t, j: (t, 0)),
            pl.BlockSpec((None, None, D, tf), lambda t, j: (li, si, 0, j)),
            pl.BlockSpec((None, None, D, tf), lambda t, j: (li, si, 0, j)),
            pl.BlockSpec((None, None, tf, D), lambda t, j: (li, si, j, 0)),
            pl.BlockSpec((1, D), lambda t, j: (0, 0)),
            pl.BlockSpec((1, D), lambda t, j: (0, 0)),
        ],
        out_specs=pl.BlockSpec((tm, D), lambda t, j: (t, 0)),
        out_shape=jax.ShapeDtypeStruct((T, D), F32),
        scratch_shapes=[pltpu.VMEM((tm, D), BF16), pltpu.VMEM((tm, D), F32)],
        compiler_params=_cparams("parallel", "arbitrary"),
        name="ffn_ln",
    )(x, wg, wu, wd, g, b)


def _matmul_body(x_ref, w_ref, o_ref, xb_ref):
    @pl.when(pl.program_id(1) == 0)
    def _():
        xb_ref[...] = x_ref[...].astype(BF16)

    o_ref[...] = _dot(xb_ref[...], w_ref[...]).astype(o_ref.dtype)


def _matmul(x, w, li, out_dtype=F32, tm=1024, tn=512):
    T, K = x.shape
    N = w.shape[-1]
    tm = _pick(T, tm)
    tn = _pick(N, tn)
    return pl.pallas_call(
        _matmul_body,
        grid=(T // tm, N // tn),
        in_specs=[
            pl.BlockSpec((tm, K), lambda t, j: (t, 0)),
            pl.BlockSpec((None, K, tn), lambda t, j: (li, 0, j)),
        ],
        out_specs=pl.BlockSpec((tm, tn), lambda t, j: (t, j)),
        out_shape=jax.ShapeDtypeStruct((T, N), out_dtype),
        scratch_shapes=[pltpu.VMEM((tm, K), BF16)],
        compiler_params=_cparams("parallel", "arbitrary"),
        name="in_proj",
    )(x, w)


def _in_even_body(x_ref, w_ref, u_ref, k_ref, v_ref, qkv_ref, xb_ref, *, nu, nh, qscale):
    j = pl.program_id(1)

    @pl.when(j == 0)
    def _():
        xb_ref[...] = x_ref[...].astype(BF16)

    r = _dot(xb_ref[...], w_ref[...])

    @pl.when(j < nu)
    def _():
        u_ref[...] = r

    @pl.when((j >= nu) & (j < nu + nh))
    def _():
        qkv_ref[...] = (r * qscale).astype(BF16)

    @pl.when((j >= nu + nh) & (j < nu + 2 * nh))
    def _():
        k_ref[...] = r
        qkv_ref[...] = r.astype(BF16)

    @pl.when(j >= nu + 2 * nh)
    def _():
        v_ref[...] = r
        qkv_ref[...] = r.astype(BF16)


def _in_even(x, w, li, pool_width, da_width, qscale, tm=1024, tn=512):
    T, K = x.shape
    tm = _pick(T, tm)
    assert pool_width % tn == 0 and da_width % tn == 0
    nu, nh = pool_width // tn, da_width // tn
    n_tiles = nu + 3 * nh

    def clamp(j, lo, n):
        return jnp.clip(j - lo, 0, n - 1)

    return pl.pallas_call(
        functools.partial(_in_even_body, nu=nu, nh=nh, qscale=qscale),
        grid=(T // tm, n_tiles),
        in_specs=[
            pl.BlockSpec((tm, K), lambda t, j: (t, 0)),
            pl.BlockSpec((None, K, tn), lambda t, j: (li, 0, j)),
        ],
        out_specs=[
            pl.BlockSpec((tm, tn), lambda t, j: (t, clamp(j, 0, nu))),
            pl.BlockSpec((tm, tn), lambda t, j: (t, clamp(j, nu + nh, nh))),
            pl.BlockSpec((tm, tn), lambda t, j: (t, clamp(j, nu + 2 * nh, nh))),
            pl.BlockSpec((tm, tn), lambda t, j: (t, clamp(j, nu, 3 * nh))),
        ],
        out_shape=[
            jax.ShapeDtypeStruct((T, pool_width), F32),
            jax.ShapeDtypeStruct((T, da_width), F32),
            jax.ShapeDtypeStruct((T, da_width), F32),
            jax.ShapeDtypeStruct((T, 3 * da_width), BF16),
        ],
        scratch_shapes=[pltpu.VMEM((tm, K), BF16)],
        compiler_params=_cparams("parallel", "arbitrary"),
        name="in_proj_even",
    )(x, w)


def _out_ln_body(*refs, alpha, widths):
    n = len(widths)
    parts = refs[:n]
    x_ref, w_ref, g_ref, b_ref, o_ref = refs[n:]
    acc = alpha * x_ref[...]
    off = 0
    for p_ref, wd in zip(parts, widths):
        acc = acc + _dot(p_ref[...], w_ref[off:off + wd, :])
        off += wd
    o_ref[...] = _layer_norm(acc, g_ref[...], b_ref[...])


def _out_ln(parts, x, w, li, g, b, alpha, tm=512):
    T, D = x.shape
    tm = _pick(T, tm)
    widths = tuple(p.shape[1] for p in parts)
    kin = sum(widths)
    return pl.pallas_call(
        functools.partial(_out_ln_body, alpha=alpha, widths=widths),
        grid=(T // tm,),
        in_specs=[pl.BlockSpec((tm, wd), lambda t: (t, 0)) for wd in widths] + [
            pl.BlockSpec((tm, D), lambda t: (t, 0)),
            pl.BlockSpec((None, kin, D), lambda t: (li, 0, 0)),
            pl.BlockSpec((1, D), lambda t: (0, 0)),
            pl.BlockSpec((1, D), lambda t: (0, 0)),
        ],
        out_specs=pl.BlockSpec((tm, D), lambda t: (t, 0)),
        out_shape=jax.ShapeDtypeStruct((T, D), F32),
        compiler_params=_cparams("parallel"),
        name="out_proj_ln",
    )(*parts, x, w, g, b)


def _ple_body(x_ref, p_ref, wg_ref, wu_ref, o_ref, *, tn):
    xb = x_ref[...].astype(BF16)
    pb = p_ref[...].astype(BF16)
    D = o_ref.shape[1]
    for c in range(D // tn):
        sl = slice(c * tn, (c + 1) * tn)
        gate = jax.nn.sigmoid(_dot(xb, wg_ref[:, sl]))
        up = _dot(pb, wu_ref[:, sl])
        o_ref[:, sl] = x_ref[:, sl] + gate * up


def _ple(x, p, wg, wu, li, tm=512, tn=512):
    T, D = x.shape
    P = p.shape[1]
    tm = _pick(T, tm)
    return pl.pallas_call(
        functools.partial(_ple_body, tn=_pick(D, tn)),
        grid=(T // tm,),
        in_specs=[
            pl.BlockSpec((tm, D), lambda t: (t, 0)),
            pl.BlockSpec((tm, P), lambda t: (t, 0)),
            pl.BlockSpec((None, D, D), lambda t: (li, 0, 0)),
            pl.BlockSpec((None, P, D), lambda t: (li, 0, 0)),
        ],
        out_specs=pl.BlockSpec((tm, D), lambda t: (t, 0)),
        out_shape=jax.ShapeDtypeStruct((T, D), F32),
        compiler_params=_cparams("parallel"),
        name="ple_gate",
    )(x, p, wg, wu)


def _pool_body(u_ref, hist_ref, w_ref, sc_ref, o_ref, nh_ref, ext_ref, *, tl, start_pos):
    l = pl.program_id(1)
    nl = pl.num_programs(1)
    H = POOL_HIST + 1

    @pl.when(l == 0)
    def _():
        ext_ref[0:1, :] = jnp.zeros((1, ext_ref.shape[1]), F32)
        ext_ref[1:H, :] = hist_ref[...]

    @pl.when(l > 0)
    def _():
        ext_ref[0:H, :] = ext_ref[tl:tl + H, :]

    ext_ref[H:H + tl, :] = u_ref[...]

    pos = start_pos + l * tl + lax.broadcasted_iota(jnp.int32, (tl, 1), 0)
    gd = LANES
    for g, wnd in enumerate(POOL_WINDOWS):
        cs = slice(g * gd, (g + 1) * gd)
        s = ext_ref[H:H + tl, cs]
        cur = s
        for d in range(1, wnd):
            s = s + ext_ref[H - d:H - d + tl, cs]
        cnt = jnp.minimum(pos + 1, wnd).astype(F32)
        pooled = s / cnt - cur
        y = _dot(pooled.astype(BF16), w_ref[g]) * sc_ref[:, cs]
        o_ref[:, cs] = y.astype(o_ref.dtype)

    @pl.when(l == nl - 1)
    def _():
        nh_ref[...] = ext_ref[tl + 1:tl + H, :]


def _pool(u, hist, w, scale, start_pos, tl=512):
    B, L, PW = u.shape
    tl = _pick(L, tl)
    assert tl >= POOL_HIST + 1
    return pl.pallas_call(
        functools.partial(_pool_body, tl=tl, start_pos=start_pos),
        grid=(B, L // tl),
        in_specs=[
            pl.BlockSpec((None, tl, PW), lambda b, l: (b, l, 0)),
            pl.BlockSpec((None, POOL_HIST, PW), lambda b, l: (b, 0, 0)),
            pl.BlockSpec(w.shape, lambda b, l: (0, 0, 0)),
            pl.BlockSpec((1, PW), lambda b, l: (0, 0)),
        ],
        out_specs=[
            pl.BlockSpec((None, tl, PW), lambda b, l: (b, l, 0)),
            pl.BlockSpec((None, POOL_HIST, PW), lambda b, l: (b, 0, 0)),
        ],
        out_shape=[
            jax.ShapeDtypeStruct((B, L, PW), BF16),
            jax.ShapeDtypeStruct((B, POOL_HIST, PW), F32),
        ],
        scratch_shapes=[pltpu.VMEM((tl + POOL_HIST + 1, PW), F32)],
        compiler_params=_cparams("parallel", "arbitrary"),
        name="pool_mixer",
    )(u, hist, w, scale)


def _t5_bucket(rel):
    nb = N_BUCKETS // 2
    max_exact = nb // 2
    n = jnp.abs(rel)
    nf = jnp.maximum(n, 1).astype(jnp.float32)
    large = max_exact + (jnp.log(nf / max_exact) / math.log(MAX_DISTANCE / max_exact)
                         * (nb - max_exact)).astype(jnp.int32)
    large = jnp.minimum(large, nb - 1)
    return jnp.where(rel > 0, nb, 0) + jnp.where(n < max_exact, n, large)


def _attn_prep_body(tbl_ref, bkt_ref, lq1_ref, lk1_ref, lq2_ref, lk2_ref, bias_ref, lam_ref, *, tq, lam_init,
                    far_bucket):
    h = pl.program_id(0)
    bkt = bkt_ref[...]
    far = tbl_ref[far_bucket, h]
    acc = jnp.zeros(bkt.shape, F32)
    for b in range(N_BUCKETS):
        acc = jnp.where(bkt == b, tbl_ref[b, h] - far, acc)
    r = lax.broadcasted_iota(jnp.int32, bkt.shape, 1)
    c = lax.broadcasted_iota(jnp.int32, bkt.shape, 2)
    t = lax.broadcasted_iota(jnp.int32, bkt.shape, 0)
    visible = (t == 0) | ((c // CHUNK) <= (r // CHUNK))
    bias_ref[...] = jnp.where(visible, acc, NEG)
    e1 = jnp.exp(jnp.sum(lq1_ref[...] * lk1_ref[...], axis=-1, keepdims=True))
    e2 = jnp.exp(jnp.sum(lq2_ref[...] * lk2_ref[...], axis=-1, keepdims=True))
    lam_ref[...] = jnp.broadcast_to(e1 - e2 + lam_init, lam_ref.shape)


def _attn_prep(rel_bias, lq1, lk1, lq2, lk2, tq, lam_init):
    nbk, H = rel_bias.shape
    r = jnp.arange(tq, dtype=jnp.int32)[:, None]
    c = jnp.arange(tq, dtype=jnp.int32)[None, :]
    bkt = jnp.stack([_t5_bucket(c - r - tq), _t5_bucket(c - r)])
    far_bucket = N_BUCKETS // 2 - 1
    assert tq >= MAX_DISTANCE
    row = lambda a: a.reshape(1, -1).astype(F32)
    return pl.pallas_call(
        functools.partial(_attn_prep_body, tq=tq, lam_init=lam_init, far_bucket=far_bucket),
        grid=(H,),
        in_specs=[
            pl.BlockSpec(memory_space=pltpu.SMEM),
            pl.BlockSpec((2, tq, tq), lambda h: (0, 0, 0)),
            pl.BlockSpec((1, lq1.shape[-1]), lambda h: (0, 0)),
            pl.BlockSpec((1, lq1.shape[-1]), lambda h: (0, 0)),
            pl.BlockSpec((1, lq1.shape[-1]), lambda h: (0, 0)),
            pl.BlockSpec((1, lq1.shape[-1]), lambda h: (0, 0)),
        ],
        out_specs=[
            pl.BlockSpec((None, 2, tq, tq), lambda h: (h, 0, 0, 0)),
            pl.BlockSpec((8, LANES), lambda h: (0, 0)),
        ],
        out_shape=[
            jax.ShapeDtypeStruct((H, 2, tq, tq), F32),
            jax.ShapeDtypeStruct((8, LANES), F32),
        ],
        compiler_params=_cparams("arbitrary"),
        name="attn_prep",
    )(rel_bias.astype(F32), bkt, row(lq1), row(lk1), row(lq2), row(lk2))


def _split_q(q):
    lane = lax.broadcasted_iota(jnp.int32, q.shape, 1)
    half = q.shape[1] // 2
    zero = jnp.zeros_like(q)
    return jnp.concatenate([jnp.where(lane < half, q, zero), jnp.where(lane >= half, q, zero)], axis=0)


def _softmax_step(qq, k, v, bias, m_ref, l_ref, acc_ref):
    s = _dot_nt(qq, k)
    if bias is not None:
        tq = bias.shape[0]
        s = (s.reshape(2, tq, s.shape[1]) + bias[None]).reshape(2 * tq, s.shape[1])
    m_prev = m_ref[...]
    m_new = jnp.maximum(m_prev, jnp.max(s, axis=1, keepdims=True))
    a = jnp.exp(m_prev - m_new)
    p = jnp.exp(s - m_new)
    l_ref[...] = a * l_ref[...] + jnp.sum(p, axis=1, keepdims=True)
    acc_ref[...] = a * acc_ref[...] + _dot(p.astype(BF16), v)
    m_ref[...] = m_new


def _attn_finish(o_ref, lam_ref, g_ref, l_ref, acc_ref, tq, out_scale):
    lam = lam_ref[0:1, 0:1]
    o = acc_ref[...] / l_ref[...]
    o = o[:tq] - lam * o[tq:]
    y = o * lax.rsqrt(jnp.mean(o * o, axis=-1, keepdims=True) + EPS) * g_ref[...] * out_scale
    o_ref[...] = y.astype(o_ref.dtype)


def _attn_init(m_ref, l_ref, acc_ref):
    m_ref[...] = jnp.full(m_ref.shape, NEG, F32)
    l_ref[...] = jnp.zeros_like(l_ref)
    acc_ref[...] = jnp.zeros_like(acc_ref)


def _attn_prompt_body(q_ref, k_ref, v_ref, bias_ref, lam_ref, g_ref, o_ref, m_ref, l_ref, acc_ref, *, tq,
                      out_scale):
    qi = pl.program_id(2)
    qq = _split_q(q_ref[...])
    _attn_init(m_ref, l_ref, acc_ref)

    def far(kt, c):
        ks = pl.ds(pl.multiple_of(kt * tq, tq), tq)
        _softmax_step(qq, k_ref[ks, :], v_ref[ks, :], None, m_ref, l_ref, acc_ref)
        return c

    lax.fori_loop(0, jnp.maximum(qi - 1, 0), far, 0)

    @pl.when(qi >= 1)
    def _():
        ks = pl.ds(pl.multiple_of((qi - 1) * tq, tq), tq)
        _softmax_step(qq, k_ref[ks, :], v_ref[ks, :], bias_ref[0], m_ref, l_ref, acc_ref)

    ks = pl.ds(pl.multiple_of(qi * tq, tq), tq)
    _softmax_step(qq, k_ref[ks, :], v_ref[ks, :], bias_ref[1], m_ref, l_ref, acc_ref)
    _attn_finish(o_ref, lam_ref, g_ref, l_ref, acc_ref, tq, out_scale)


def _attn_prompt(qkv, bias, lam, g, n_heads, out_scale):
    B, L, _ = qkv.shape
    H = n_heads
    tq = bias.shape[-1]
    assert L % tq == 0 and tq % CHUNK == 0
    return pl.pallas_call(
        functools.partial(_attn_prompt_body, tq=tq, out_scale=out_scale),
        grid=(B, H, L // tq),
        in_specs=[
            pl.BlockSpec((None, tq, LANES), lambda b, h, i: (b, i, h)),
            pl.BlockSpec((None, L, LANES), lambda b, h, i: (b, 0, H + h)),
            pl.BlockSpec((None, L, LANES), lambda b, h, i: (b, 0, 2 * H + h)),
            pl.BlockSpec((None, 2, tq, tq), lambda b, h, i: (h, 0, 0, 0)),
            pl.BlockSpec((8, LANES), lambda b, h, i: (0, 0)),
            pl.BlockSpec((1, LANES), lambda b, h, i: (0, 0)),
        ],
        out_specs=pl.BlockSpec((None, tq, LANES), lambda b, h, i: (b, i, h)),
        out_shape=jax.ShapeDtypeStruct((B, L, H * LANES), BF16),
        scratch_shapes=[
            pltpu.VMEM((2 * tq, 1), F32),
            pltpu.VMEM((2 * tq, 1), F32),
            pltpu.VMEM((2 * tq, LANES), F32),
        ],
        compiler_params=_cparams("parallel", "parallel", "arbitrary"),
        name="diff_attn_prompt",
    )(qkv, qkv, qkv, bias, lam, g)


def _attn_sample_body(q_ref, kn_ref, vn_ref, kc_ref, vc_ref, bprev_ref, bdiag_ref, lam_ref, g_ref, o_ref,
                      m_ref, l_ref, acc_ref, *, lq, tk, out_scale):
    qq = _split_q(q_ref[...])
    _attn_init(m_ref, l_ref, acc_ref)
    n_tiles = kc_ref.shape[0] // tk

    def far(kt, c):
        ks = pl.ds(pl.multiple_of(kt * tk, tk), tk)
        _softmax_step(qq, kc_ref[ks, :].astype(BF16), vc_ref[ks, :].astype(BF16), None, m_ref, l_ref, acc_ref)
        return c

    lax.fori_loop(0, n_tiles - 1, far, 0)
    ks = pl.ds((n_tiles - 1) * tk, tk)
    _softmax_step(qq, kc_ref[ks, :].astype(BF16), vc_ref[ks, :].astype(BF16), bprev_ref[...], m_ref, l_ref,
                  acc_ref)
    _softmax_step(qq, kn_ref[...], vn_ref[...], bdiag_ref[:, 0:lq], m_ref, l_ref, acc_ref)
    _attn_finish(o_ref, lam_ref, g_ref, l_ref, acc_ref, lq, out_scale)


def _attn_sample(qkv, k_cache, v_cache, bias, lam, g, n_heads, out_scale):
    B, lq, _ = qkv.shape
    H = n_heads
    P = k_cache.shape[1]
    tk = bias.shape[-1]
    assert lq == CHUNK and P % tk == 0 and P % CHUNK == 0 and lq <= tk
    return pl.pallas_call(
        functools.partial(_attn_sample_body, lq=lq, tk=tk, out_scale=out_scale),
        grid=(B, H),
        in_specs=[
            pl.BlockSpec((None, lq, LANES), lambda b, h: (b, 0, h)),
            pl.BlockSpec((None, lq, LANES), lambda b, h: (b, 0, H + h)),
            pl.BlockSpec((None, lq, LANES), lambda b, h: (b, 0, 2 * H + h)),
            pl.BlockSpec((None, P, LANES), lambda b, h: (b, 0, h)),
            pl.BlockSpec((None, P, LANES), lambda b, h: (b, 0, h)),
            pl.BlockSpec((None, None, lq, tk), lambda b, h: (h, 0, 0, 0)),
            pl.BlockSpec((None, None, lq, tk), lambda b, h: (h, 1, 0, 0)),
            pl.BlockSpec((8, LANES), lambda b, h: (0, 0)),
            pl.BlockSpec((1, LANES), lambda b, h: (0, 0)),
        ],
        out_specs=pl.BlockSpec((None, lq, LANES), lambda b, h: (b, 0, h)),
        out_shape=jax.ShapeDtypeStruct((B, lq, H * LANES), BF16),
        scratch_shapes=[
            pltpu.VMEM((2 * lq, 1), F32),
            pltpu.VMEM((2 * lq, 1), F32),
            pltpu.VMEM((2 * lq, LANES), F32),
        ],
        compiler_params=_cparams("parallel", "arbitrary"),
        name="diff_attn_sample",
    )(qkv, qkv, qkv, k_cache.reshape(B, P, H * LANES), v_cache.reshape(B, P, H * LANES), bias, bias, lam, g)


def _log_sigmoid(z):
    return jnp.minimum(z, 0.0) - jnp.log1p(jnp.exp(-jnp.abs(z)))


def _logaddexp(a, b):
    return jnp.maximum(a, b) + jnp.log1p(jnp.exp(-jnp.abs(a - b)))


def _hgrn_body(hq_ref, hz_ref, hi_ref, hg_ref, lb_ref, ng_ref, s0_ref, o_ref, s_ref, st_ref, lvl_ref, *, ck,
               layer):
    l = pl.program_id(2)
    tl = hq_ref.shape[0]
    n_lev = ck.bit_length() - 1

    @pl.when(l == 0)
    def _():
        st_ref[...] = s0_ref[...].T

    t_i = lax.broadcasted_iota(jnp.int32, (ck, ck), 0)
    s_i = lax.broadcasted_iota(jnp.int32, (ck, ck), 1)
    x = t_i ^ s_i
    hb = jnp.zeros((ck, ck), jnp.int32)
    for b in range(1, n_lev):
        hb = hb + (x >= (1 << b)).astype(jnp.int32)
    lvl_ref[...] = jnp.where(t_i > s_i, hb, -1)

    lg = lb_ref[...]
    e = jnp.exp(lg - jnp.max(lg, axis=0, keepdims=True))
    p = e / jnp.sum(e, axis=0, keepdims=True)
    cum = p[0:1]
    for d in range(1, layer + 1):
        cum = cum + p[d:d + 1]
    lb = cum - p[0:1]
    log_lb = jnp.log(lb)
    log_1mlb = jnp.log1p(-lb)
    row = lax.broadcasted_iota(jnp.int32, (ck, LANES), 0)

    def chunk(c, carry):
        rs = pl.ds(pl.multiple_of(c * ck, ck), ck)
        hq = hq_ref[rs, :]
        z = hz_ref[rs, :]
        v = hi_ref[rs, :]
        hg = hg_ref[rs, :]
        q = hq * jax.nn.sigmoid(hq)
        ls = _log_sigmoid(z)
        g = _logaddexp(log_lb, log_1mlb + ls)
        k = jnp.exp(log_1mlb + ls - z)
        vb = v.astype(BF16)

        pf = g
        tot = g
        lvl = lvl_ref[...]
        a = jnp.zeros((ck, ck), F32)
        for lev in range(n_lev):
            hsz = 1 << lev
            ql = (q * jnp.exp(pf)).astype(BF16)
            kl = (k * jnp.exp(tot - pf)).astype(BF16)
            a = jnp.where(lvl == lev, _dot_nt(ql, kl), a)
            second = (row & hsz) != 0
            up = pltpu.roll(tot, hsz, 0)
            dn = pltpu.roll(tot, ck - hsz, 0)
            pf = pf + jnp.where(second, up, 0.0)
            tot = tot + jnp.where(second, up, dn)
        st = st_ref[...]
        o = _dot(a.astype(BF16), vb) + jnp.sum(q * k, axis=1, keepdims=True) * v
        o = o + _dot_nt((q * jnp.exp(pf)).astype(BF16), st.astype(BF16))
        kst = (k * jnp.exp(tot - pf)).astype(BF16)
        st_ref[...] = st * jnp.exp(tot[0:1, :]) + _dot(v.T.astype(BF16), kst)
        y = o * lax.rsqrt(jnp.mean(o * o, axis=-1, keepdims=True) + EPS) * ng_ref[...]
        y = y * (hg * jax.nn.sigmoid(hg))
        o_ref[rs, :] = y.astype(o_ref.dtype)
        return carry

    lax.fori_loop(0, tl // ck, chunk, 0)

    @pl.when(l == pl.num_programs(2) - 1)
    def _():
        s_ref[...] = st_ref[...].T


def _hgrn(h, lb_logits, norm_g, s0, layer, n_heads, tl=2048, ck=128):
    B, L, _ = h.shape
    H = n_heads
    tl = _pick(L, tl)
    ck = _pick(tl, ck)
    assert ck & (ck - 1) == 0 and ck >= 8
    depth = lb_logits.shape[0]
    blk = lambda part: pl.BlockSpec((None, tl, LANES), lambda b, hd, l: (b, l, part * H + hd))
    return pl.pallas_call(
        functools.partial(_hgrn_body, ck=ck, layer=layer),
        grid=(B, H, L // tl),
        in_specs=[
            blk(0), blk(1), blk(2), blk(3),
            pl.BlockSpec((depth, LANES), lambda b, hd, l: (0, hd)),
            pl.BlockSpec((1, LANES), lambda b, hd, l: (0, 0)),
            pl.BlockSpec((None, None, LANES, LANES), lambda b, hd, l: (b, hd, 0, 0)),
        ],
        out_specs=[
            pl.BlockSpec((None, tl, LANES), lambda b, hd, l: (b, l, hd)),
            pl.BlockSpec((None, None, LANES, LANES), lambda b, hd, l: (b, hd, 0, 0)),
        ],
        out_shape=[
            jax.ShapeDtypeStruct((B, L, H * LANES), BF16),
            jax.ShapeDtypeStruct((B, H, LANES, LANES), F32),
        ],
        scratch_shapes=[pltpu.VMEM((LANES, LANES), F32), pltpu.VMEM((ck, ck), jnp.int32)],
        compiler_params=_cparams("parallel", "parallel", "arbitrary"),
        name="hgrn_scan",
    )(h, h, h, h, lb_logits, norm_g, s0)


def _trunk(x, p, k_cache, v_cache, pool_hist, hg_state, W, attn_prep):
    B, L, D = x.shape
    depth = W["ln_g"].shape[0]
    alpha = (2 * depth) ** 0.25
    T = B * L
    xt = x.reshape(T, D)
    new_k, new_v, new_pool, new_s = [], [], [], []
    pool_width = W["pool_scale"].shape[-1]
    da_width = D - pool_width
    n_da_heads = da_width // LANES
    n_hg_heads = D // LANES
    lnrow = lambda a, i, s: a[i, s].reshape(1, D)
    for i in range(depth):
        xt = _ffn_ln(xt, W["wg"], W["wu"], W["wd"], lnrow(W["ln_g"], i, 0), lnrow(W["ln_b"], i, 0), i, 0, alpha)
        if i % 2 == 0:
            e = i // 2
            u, k, v, qkv = _in_even(xt, W["w_in_even"], e, pool_width, da_width, (LANES // 2) ** -0.5)
            past = 0 if k_cache is None else k_cache.shape[2]
            pool_out, nh = _pool(u.reshape(B, L, pool_width), pool_hist[e], W["pool_w"][e],
                                 W["pool_scale"][e].reshape(1, pool_width), past)
            bias, lam = attn_prep[e]
            lam_init = 0.8 - 0.6 * math.exp(-0.3 * i)
            g = W["diff_norm_g"][e].reshape(1, LANES)
            qkv3 = qkv.reshape(B, L, 3 * da_width)
            if k_cache is None:
                o = _attn_prompt(qkv3, bias, lam, g, n_da_heads, 1.0 - lam_init)
            else:
                o = _attn_sample(qkv3, k_cache[e], v_cache[e], bias, lam, g, n_da_heads, 1.0 - lam_init)
            parts = [pool_out.reshape(T, pool_width), o.reshape(T, da_width)]
            w_out = W["w_out_even"]
            new_k.append(k.reshape(B, L, n_da_heads, LANES))
            new_v.append(v.reshape(B, L, n_da_heads, LANES))
            new_pool.append(nh)
            li = e
        else:
            od = i // 2
            h = _matmul(xt, W["w_in_odd"], od)
            o, s = _hgrn(h.reshape(B, L, 4 * D), W["lb_logits"], W["hgrn_norm_g"][od].reshape(1, LANES),
                         hg_state[od], i, n_hg_heads)
            parts = [o.reshape(T, D)]
            w_out = W["w_out_odd"]
            new_s.append(s)
            li = od
        xt = _out_ln(parts, xt, w_out, li, lnrow(W["ln_g"], i, 1), lnrow(W["ln_b"], i, 1), alpha)
        xt = _ffn_ln(xt, W["wg"], W["wu"], W["wd"], lnrow(W["ln_g"], i, 2), lnrow(W["ln_b"], i, 2), i, 1, alpha)
        xt = _ple(xt, p[i].reshape(T, -1), W["w_ple_gate"], W["w_ple_up"], i)
    return xt.reshape(B, L, D), jnp.stack(new_k), jnp.stack(new_v), jnp.stack(new_pool), jnp.stack(new_s)


ATTN_TILE = 256


def kernel(x_prompt, x_sample, cache_diff_k, cache_diff_v, state_pool, state_hgrn, p_prompt, p_sample, ln_g, ln_b, w_ffn_gate, w_ffn_up, w_ffn_down, w_ple_gate, w_ple_up, w_in_even, w_out_even, pool_w, pool_scale, lam_q1, lam_k1, lam_q2, lam_k2, diff_norm_g, rel_bias, w_in_odd, w_out_odd, hgrn_norm_g, hgrn_lb_logits):
    bf = lambda a: a.astype(BF16)
    W = dict(
        ln_g=ln_g.astype(F32), ln_b=ln_b.astype(F32),
        wg=bf(w_ffn_gate), wu=bf(w_ffn_up), wd=bf(w_ffn_down),
        w_ple_gate=bf(w_ple_gate), w_ple_up=bf(w_ple_up),
        w_in_even=bf(w_in_even), w_out_even=bf(w_out_even),
        pool_w=bf(pool_w), pool_scale=pool_scale.astype(F32),
        diff_norm_g=diff_norm_g.astype(F32),
        w_in_odd=bf(w_in_odd), w_out_odd=bf(w_out_odd),
        hgrn_norm_g=hgrn_norm_g.astype(F32), lb_logits=hgrn_lb_logits.astype(F32),
    )
    n_even = w_in_even.shape[0]
    n_odd = w_in_odd.shape[0]
    attn_prep = []
    for e in range(n_even):
        lam_init = 0.8 - 0.6 * math.exp(-0.3 * (2 * e))
        attn_prep.append(_attn_prep(rel_bias, lam_q1[e], lam_k1[e], lam_q2[e], lam_k2[e], ATTN_TILE, lam_init))

    B = x_prompt.shape[0]
    dt = x_prompt.dtype
    zero_pool = jnp.zeros((n_even, B) + state_pool.shape[2:], dt)
    zero_s = jnp.zeros((n_odd, B) + state_hgrn.shape[2:], dt)
    y_p, k_p, v_p, pool_p, s_p = _trunk(x_prompt, p_prompt, None, None, zero_pool, zero_s, W, attn_prep)
    y_s, k_s, v_s, pool_s, s_s = _trunk(x_sample, p_sample, cache_diff_k, cache_diff_v, state_pool, state_hgrn,
                                        W, attn_prep)
    return (y_p, y_s, k_p, v_p, k_s, v_s, pool_p, pool_s, s_p, s_s)
```

```python
import functools
import math

import jax
import jax.numpy as jnp
from jax import lax
from jax.experimental import pallas as pl
from jax.experimental.pallas import tpu as pltpu

F32 = jnp.float32
BF16 = jnp.bfloat16

CHUNK = 64
POOL_WINDOWS = (2, 4, 8, 16)
POOL_HIST = max(POOL_WINDOWS) - 1
N_BUCKETS = 32
MAX_DISTANCE = 128
EPS = 1e-5
NEG = -1e30
LOG2E = math.log2(math.e)
LANES = 128

VMEM_LIMIT = 56 * 1024 * 1024


def _cparams(*sem):
    return pltpu.CompilerParams(dimension_semantics=sem, vmem_limit_bytes=VMEM_LIMIT)


def _pick(n, pref):
    if n <= pref:
        return n
    t = pref
    while n % t:
        t //= 2
    return t


def _layer_norm(y, g, b):
    mu = jnp.mean(y, axis=-1, keepdims=True)
    d = y - mu
    var = jnp.mean(d * d, axis=-1, keepdims=True)
    return d * lax.rsqrt(var + EPS) * g + b


def _dot(a, b):
    return jnp.dot(a, b, preferred_element_type=F32)


def _dot_nt(a, b):
    return lax.dot_general(a, b, (((1,), (1,)), ((), ())), preferred_element_type=F32)


def _ffn_ln_body(x_ref, wg_ref, wu_ref, wd_ref, g_ref, b_ref, o_ref, xb_ref, acc_ref, *, alpha):
    j = pl.program_id(1)

    @pl.when(j == 0)
    def _():
        xb_ref[...] = x_ref[...].astype(BF16)
        acc_ref[...] = jnp.zeros_like(acc_ref)

    xb = xb_ref[...]
    hg = _dot(xb, wg_ref[...])
    hu = _dot(xb, wu_ref[...])
    act = hg * jax.nn.sigmoid(hg) * hu
    acc_ref[...] += _dot(act.astype(BF16), wd_ref[...])

    @pl.when(j == pl.num_programs(1) - 1)
    def _():
        y = alpha * x_ref[...] + 0.5 * acc_ref[...]
        o_ref[...] = _layer_norm(y, g_ref[...], b_ref[...])


def _ffn_ln(x, wg, wu, wd, g, b, li, si, alpha, tm=512, tf=512):
    T, D = x.shape
    F = wg.shape[-1]
    tm = _pick(T, tm)
    tf = _pick(F, tf)
    return pl.pallas_call(
        functools.partial(_ffn_ln_body, alpha=alpha),
        grid=(T // tm, F // tf),
        in_specs=[
            pl.BlockSpec((tm, D), lambda t, j: (t, 0)),
            pl.BlockSpec((None, None, D, tf), lambda t, j: (li, si, 0, j)),
            pl.BlockSpec((None, None, D, tf), lambda t, j: (li, si, 0, j)),
            pl.BlockSpec((None, None, tf, D), lambda t, j: (li, si, j, 0)),
            pl.BlockSpec((1, D), lambda t, j: (0, 0)),
            pl.BlockSpec((1, D), lambda t, j: (0, 0)),
        ],
        out_specs=pl.BlockSpec((tm, D), lambda t, j: (t, 0)),
        out_shape=jax.ShapeDtypeStruct((T, D), F32),
        scratch_shapes=[pltpu.VMEM((tm, D), BF16), pltpu.VMEM((tm, D), F32)],
        compiler_params=_cparams("parallel", "arbitrary"),
        name="ffn_ln",
    )(x, wg, wu, wd, g, b)


def _matmul_body(x_ref, w_ref, o_ref, xb_ref):
    @pl.when(pl.program_id(1) == 0)
    def _():
        xb_ref[...] = x_ref[...].astype(BF16)

    o_ref[...] = _dot(xb_ref[...], w_ref[...]).astype(o_ref.dtype)


def _matmul(x, w, li, out_dtype=F32, tm=1024, tn=512):
    T, K = x.shape
    N = w.shape[-1]
    tm = _pick(T, tm)
    tn = _pick(N, tn)
    return pl.pallas_call(
        _matmul_body,
        grid=(T // tm, N // tn),
        in_specs=[
            pl.BlockSpec((tm, K), lambda t, j: (t, 0)),
            pl.BlockSpec((None, K, tn), lambda t, j: (li, 0, j)),
        ],
        out_specs=pl.BlockSpec((tm, tn), lambda t, j: (t, j)),
        out_shape=jax.ShapeDtypeStruct((T, N), out_dtype),
        scratch_shapes=[pltpu.VMEM((tm, K), BF16)],
        compiler_params=_cparams("parallel", "arbitrary"),
        name="in_proj",
    )(x, w)


def _in_even_body(x_ref, w_ref, u_ref, k_ref, v_ref, qkv_ref, xb_ref, *, nu, nh, qscale):
    j = pl.program_id(1)

    @pl.when(j == 0)
    def _():
        xb_ref[...] = x_ref[...].astype(BF16)

    r = _dot(xb_ref[...], w_ref[...])

    def put_heads(ref):
        bb, hpt, tl, _ = ref.shape
        for hh in range(hpt):
            ref[:, hh] = r[:, hh * LANES:(hh + 1) * LANES].reshape(bb, tl, LANES)

    @pl.when(j < nu)
    def _():
        u_ref[...] = r

    @pl.when((j >= nu) & (j < nu + nh))
    def _():
        qkv_ref[...] = (r * qscale).astype(BF16)

    @pl.when((j >= nu + nh) & (j < nu + 2 * nh))
    def _():
        put_heads(k_ref)
        qkv_ref[...] = r.astype(BF16)

    @pl.when(j >= nu + 2 * nh)
    def _():
        put_heads(v_ref)
        qkv_ref[...] = r.astype(BF16)


def _in_even(x, w, li, B, L, pool_width, da_width, qscale, tm=1024, tn=512):
    T, K = x.shape
    tm = _pick(T, tm)
    assert pool_width % tn == 0 and da_width % tn == 0 and tn % LANES == 0
    nu, nh = pool_width // tn, da_width // tn
    n_tiles = nu + 3 * nh
    hpt = tn // LANES
    tl = min(L, tm)
    assert tm % tl == 0 and L % tl == 0
    bb, nl = tm // tl, L // tl

    def clamp(j, lo, n):
        return jnp.clip(j - lo, 0, n - 1)

    def head_spec(lo):
        return pl.BlockSpec((bb, hpt, tl, LANES), lambda t, j: (t // nl, clamp(j, lo, nh), t % nl, 0))

    return pl.pallas_call(
        functools.partial(_in_even_body, nu=nu, nh=nh, qscale=qscale),
        grid=(T // tm, n_tiles),
        in_specs=[
            pl.BlockSpec((tm, K), lambda t, j: (t, 0)),
            pl.BlockSpec((None, K, tn), lambda t, j: (li, 0, j)),
        ],
        out_specs=[
            pl.BlockSpec((tm, tn), lambda t, j: (t, clamp(j, 0, nu))),
            head_spec(nu + nh),
            head_spec(nu + 2 * nh),
            pl.BlockSpec((tm, tn), lambda t, j: (t, clamp(j, nu, 3 * nh))),
        ],
        out_shape=[
            jax.ShapeDtypeStruct((T, pool_width), F32),
            jax.ShapeDtypeStruct((B, da_width // LANES, L, LANES), F32),
            jax.ShapeDtypeStruct((B, da_width // LANES, L, LANES), F32),
            jax.ShapeDtypeStruct((T, 3 * da_width), BF16),
        ],
        scratch_shapes=[pltpu.VMEM((tm, K), BF16)],
        compiler_params=_cparams("parallel", "arbitrary"),
        name="in_proj_even",
    )(x, w)


def _out_ln_body(*refs, alpha, widths):
    n = len(widths)
    parts = refs[:n]
    x_ref, w_ref, g_ref, b_ref, o_ref = refs[n:]
    acc = alpha * x_ref[...]
    off = 0
    for p_ref, wd in zip(parts, widths):
        acc = acc + _dot(p_ref[...], w_ref[off:off + wd, :])
        off += wd
    o_ref[...] = _layer_norm(acc, g_ref[...], b_ref[...])


def _out_ln(parts, x, w, li, g, b, alpha, tm=512):
    T, D = x.shape
    tm = _pick(T, tm)
    widths = tuple(p.shape[1] for p in parts)
    kin = sum(widths)
    return pl.pallas_call(
        functools.partial(_out_ln_body, alpha=alpha, widths=widths),
        grid=(T // tm,),
        in_specs=[pl.BlockSpec((tm, wd), lambda t: (t, 0)) for wd in widths] + [
            pl.BlockSpec((tm, D), lambda t: (t, 0)),
            pl.BlockSpec((None, kin, D), lambda t: (li, 0, 0)),
            pl.BlockSpec((1, D), lambda t: (0, 0)),
            pl.BlockSpec((1, D), lambda t: (0, 0)),
        ],
        out_specs=pl.BlockSpec((tm, D), lambda t: (t, 0)),
        out_shape=jax.ShapeDtypeStruct((T, D), F32),
        compiler_params=_cparams("parallel"),
        name="out_proj_ln",
    )(*parts, x, w, g, b)


def _ple_body(x_ref, p_ref, wg_ref, wu_ref, o_ref, *, tn):
    xb = x_ref[...].astype(BF16)
    pb = p_ref[...].astype(BF16)
    D = o_ref.shape[1]
    for c in range(D // tn):
        sl = slice(c * tn, (c + 1) * tn)
        gate = jax.nn.sigmoid(_dot(xb, wg_ref[:, sl]))
        up = _dot(pb, wu_ref[:, sl])
        o_ref[:, sl] = x_ref[:, sl] + gate * up


def _ple(x, p, wg, wu, li, tm=512, tn=512):
    T, D = x.shape
    P = p.shape[2]
    tm = _pick(T, tm)
    return pl.pallas_call(
        functools.partial(_ple_body, tn=_pick(D, tn)),
        grid=(T // tm,),
        in_specs=[
            pl.BlockSpec((tm, D), lambda t: (t, 0)),
            pl.BlockSpec((None, tm, P), lambda t: (li, t, 0)),
            pl.BlockSpec((None, D, D), lambda t: (li, 0, 0)),
            pl.BlockSpec((None, P, D), lambda t: (li, 0, 0)),
        ],
        out_specs=pl.BlockSpec((tm, D), lambda t: (t, 0)),
        out_shape=jax.ShapeDtypeStruct((T, D), F32),
        compiler_params=_cparams("parallel"),
        name="ple_gate",
    )(x, p, wg, wu)


def _pool_body(u_ref, hist_ref, w_ref, sc_ref, o_ref, nh_ref, ext_ref, *, tl, start_pos):
    l = pl.program_id(1)
    nl = pl.num_programs(1)
    H = POOL_HIST + 1

    @pl.when(l == 0)
    def _():
        ext_ref[0:1, :] = jnp.zeros((1, ext_ref.shape[1]), F32)
        ext_ref[1:H, :] = hist_ref[...]

    @pl.when(l > 0)
    def _():
        ext_ref[0:H, :] = ext_ref[tl:tl + H, :]

    ext_ref[H:H + tl, :] = u_ref[...]

    pos = start_pos + l * tl + lax.broadcasted_iota(jnp.int32, (tl, 1), 0)
    gd = LANES
    for g, wnd in enumerate(POOL_WINDOWS):
        cs = slice(g * gd, (g + 1) * gd)
        s = ext_ref[H:H + tl, cs]
        cur = s
        for d in range(1, wnd):
            s = s + ext_ref[H - d:H - d + tl, cs]
        cnt = jnp.minimum(pos + 1, wnd).astype(F32)
        pooled = s / cnt - cur
        y = _dot(pooled.astype(BF16), w_ref[g]) * sc_ref[:, cs]
        o_ref[:, cs] = y.astype(o_ref.dtype)

    @pl.when(l == nl - 1)
    def _():
        nh_ref[...] = ext_ref[tl + 1:tl + H, :]


def _pool(u, hist, w, scale, start_pos, tl=512):
    B, L, PW = u.shape
    tl = _pick(L, tl)
    assert tl >= POOL_HIST + 1
    return pl.pallas_call(
        functools.partial(_pool_body, tl=tl, start_pos=start_pos),
        grid=(B, L // tl),
        in_specs=[
            pl.BlockSpec((None, tl, PW), lambda b, l: (b, l, 0)),
            pl.BlockSpec((None, POOL_HIST, PW), lambda b, l: (b, 0, 0)),
            pl.BlockSpec(w.shape, lambda b, l: (0, 0, 0)),
            pl.BlockSpec((1, PW), lambda b, l: (0, 0)),
        ],
        out_specs=[
            pl.BlockSpec((None, tl, PW), lambda b, l: (b, l, 0)),
            pl.BlockSpec((None, POOL_HIST, PW), lambda b, l: (b, 0, 0)),
        ],
        out_shape=[
            jax.ShapeDtypeStruct((B, L, PW), BF16),
            jax.ShapeDtypeStruct((B, POOL_HIST, PW), F32),
        ],
        scratch_shapes=[pltpu.VMEM((tl + POOL_HIST + 1, PW), F32)],
        compiler_params=_cparams("parallel", "arbitrary"),
        name="pool_mixer",
    )(u, hist, w, scale)


def _t5_bucket(rel):
    nb = N_BUCKETS // 2
    max_exact = nb // 2
    n = jnp.abs(rel)
    nf = jnp.maximum(n, 1).astype(jnp.float32)
    large = max_exact + (jnp.log(nf / max_exact) / math.log(MAX_DISTANCE / max_exact)
                         * (nb - max_exact)).astype(jnp.int32)
    large = jnp.minimum(large, nb - 1)
    return jnp.where(rel > 0, nb, 0) + jnp.where(n < max_exact, n, large)


def _attn_prep_body(tbl_ref, bkt_ref, lq1_ref, lk1_ref, lq2_ref, lk2_ref, bias_ref, lam_ref, *, tq, lam_init,
                    far_bucket):
    h = pl.program_id(0)
    bkt = bkt_ref[...]
    far = tbl_ref[far_bucket, h]
    acc = jnp.zeros(bkt.shape, F32)
    for b in range(N_BUCKETS):
        acc = jnp.where(bkt == b, tbl_ref[b, h] - far, acc)
    r = lax.broadcasted_iota(jnp.int32, bkt.shape, 1)
    c = lax.broadcasted_iota(jnp.int32, bkt.shape, 2)
    t = lax.broadcasted_iota(jnp.int32, bkt.shape, 0)
    visible = (t == 0) | ((c // CHUNK) <= (r // CHUNK))
    bias_ref[...] = jnp.where(visible, acc, NEG)
    e1 = jnp.exp(jnp.sum(lq1_ref[...] * lk1_ref[...], axis=-1, keepdims=True))
    e2 = jnp.exp(jnp.sum(lq2_ref[...] * lk2_ref[...], axis=-1, keepdims=True))
    lam_ref[...] = jnp.broadcast_to(e1 - e2 + lam_init, lam_ref.shape)


def _attn_prep(rel_bias, lq1, lk1, lq2, lk2, tq, lam_init):
    nbk, H = rel_bias.shape
    r = jnp.arange(tq, dtype=jnp.int32)[:, None]
    c = jnp.arange(tq, dtype=jnp.int32)[None, :]
    bkt = jnp.stack([_t5_bucket(c - r - tq), _t5_bucket(c - r)])
    far_bucket = N_BUCKETS // 2 - 1
    assert tq >= MAX_DISTANCE
    row = lambda a: a.reshape(1, -1).astype(F32)
    return pl.pallas_call(
        functools.partial(_attn_prep_body, tq=tq, lam_init=lam_init, far_bucket=far_bucket),
        grid=(H,),
        in_specs=[
            pl.BlockSpec(memory_space=pltpu.SMEM),
            pl.BlockSpec((2, tq, tq), lambda h: (0, 0, 0)),
            pl.BlockSpec((1, lq1.shape[-1]), lambda h: (0, 0)),
            pl.BlockSpec((1, lq1.shape[-1]), lambda h: (0, 0)),
            pl.BlockSpec((1, lq1.shape[-1]), lambda h: (0, 0)),
            pl.BlockSpec((1, lq1.shape[-1]), lambda h: (0, 0)),
        ],
        out_specs=[
            pl.BlockSpec((None, 2, tq, tq), lambda h: (h, 0, 0, 0)),
            pl.BlockSpec((8, LANES), lambda h: (0, 0)),
        ],
        out_shape=[
            jax.ShapeDtypeStruct((H, 2, tq, tq), F32),
            jax.ShapeDtypeStruct((8, LANES), F32),
        ],
        compiler_params=_cparams("arbitrary"),
        name="attn_prep",
    )(rel_bias.astype(F32), bkt, row(lq1), row(lk1), row(lq2), row(lk2))


def _split_q(q):
    lane = lax.broadcasted_iota(jnp.int32, q.shape, 1)
    half = q.shape[1] // 2
    zero = jnp.zeros_like(q)
    return jnp.concatenate([jnp.where(lane < half, q, zero), jnp.where(lane >= half, q, zero)], axis=0)


def _attend(qq, spans, s_ref, mx_ref, l_ref, acc_ref):
    rows = qq.shape[0]
    mx_ref[...] = jnp.full(mx_ref.shape, NEG, F32)
    for col, get_k, _, bias in spans:
        s = _dot_nt(qq, get_k())
        w = s.shape[1]
        if bias is not None:
            s = (s.reshape(2, rows // 2, w) + bias[None]).reshape(rows, w)
        s_ref[:, col:col + w] = s
        if w % LANES == 0:
            m = functools.reduce(jnp.maximum, [s[:, c:c + LANES] for c in range(0, w, LANES)])
            mx_ref[...] = jnp.maximum(mx_ref[...], m)
        else:
            mx_ref[:, 0:w] = jnp.maximum(mx_ref[:, 0:w], s)
    m_b = jnp.broadcast_to(jnp.max(mx_ref[...], axis=1, keepdims=True), mx_ref.shape)
    mx_ref[...] = m_b
    l_ref[...] = jnp.zeros_like(l_ref)
    acc_ref[...] = jnp.zeros_like(acc_ref)
    for col, get_k, get_v, _ in spans:
        w = get_v().shape[0]
        s = s_ref[:, col:col + w]
        m_b = mx_ref[...]
        if w % LANES == 0:
            ps = [jnp.exp(s[:, c:c + LANES] - m_b) for c in range(0, w, LANES)]
            l_ref[...] += functools.reduce(jnp.add, ps)
            p = ps[0] if len(ps) == 1 else jnp.concatenate(ps, axis=1)
        else:
            p = jnp.exp(s - m_b[:, 0:w])
            l_ref[:, 0:w] += p
        acc_ref[...] += _dot(p.astype(BF16), get_v())


def _attn_finish(lam_ref, g_ref, l_ref, acc_ref, tq, out_scale):
    lam = lam_ref[0:1, 0:1]
    o = acc_ref[...] / jnp.sum(l_ref[...], axis=1, keepdims=True)
    o = o[:tq] - lam * o[tq:]
    return o * lax.rsqrt(jnp.mean(o * o, axis=-1, keepdims=True) + EPS) * g_ref[...] * out_scale


ATTN_KEY_SPAN = 512


def _attn_prompt_body(q_ref, k_ref, v_ref, bias_ref, lam_ref, g_ref, o_ref, s_ref, mx_ref, l_ref, acc_ref, *,
                      tq, out_scale):
    L = q_ref.shape[0]

    def span(st, w, bias):
        return (st, lambda: k_ref[st:st + w, :], lambda: v_ref[st:st + w, :], bias)

    for qi in range(L // tq):
        par = qi % 2
        far_end = max(qi - 1, 0) * tq
        spans = [span(st, min(ATTN_KEY_SPAN, far_end - st), None) for st in range(0, far_end, ATTN_KEY_SPAN)]
        if qi >= 1:
            spans.append(span((qi - 1) * tq, tq, bias_ref[0]))
        spans.append(span(qi * tq, tq, bias_ref[1]))
        qq = _split_q(q_ref[qi * tq:(qi + 1) * tq, :])
        _attend(qq, spans, s_ref.at[par], mx_ref.at[par], l_ref.at[par], acc_ref.at[par])
        y = _attn_finish(lam_ref, g_ref, l_ref.at[par], acc_ref.at[par], tq, out_scale)
        o_ref[qi * tq:(qi + 1) * tq, :] = y.astype(o_ref.dtype)


def _attn_prompt(qkv, bias, lam, g, n_heads, out_scale):
    B, L, _ = qkv.shape
    H = n_heads
    tq = bias.shape[-1]
    assert L % tq == 0 and tq % CHUNK == 0
    return pl.pallas_call(
        functools.partial(_attn_prompt_body, tq=tq, out_scale=out_scale),
        grid=(B, H),
        in_specs=[
            pl.BlockSpec((None, L, LANES), lambda b, h: (b, 0, h)),
            pl.BlockSpec((None, L, LANES), lambda b, h: (b, 0, H + h)),
            pl.BlockSpec((None, L, LANES), lambda b, h: (b, 0, 2 * H + h)),
            pl.BlockSpec((None, 2, tq, tq), lambda b, h: (h, 0, 0, 0)),
            pl.BlockSpec((8, LANES), lambda b, h: (0, 0)),
            pl.BlockSpec((1, LANES), lambda b, h: (0, 0)),
        ],
        out_specs=pl.BlockSpec((None, L, LANES), lambda b, h: (b, 0, h)),
        out_shape=jax.ShapeDtypeStruct((B, L, H * LANES), BF16),
        scratch_shapes=[
            pltpu.VMEM((2, 2 * tq, L), F32),
            pltpu.VMEM((2, 2 * tq, LANES), F32),
            pltpu.VMEM((2, 2 * tq, LANES), F32),
            pltpu.VMEM((2, 2 * tq, LANES), F32),
        ],
        compiler_params=_cparams("parallel", "parallel"),
        name="diff_attn_prompt",
    )(qkv, qkv, qkv, bias, lam, g)


def _attn_sample_body(q_ref, kn_ref, vn_ref, kc_ref, vc_ref, bprev_ref, bdiag_ref, lam_ref, g_ref, o_ref,
                      s_ref, mx_ref, l_ref, acc_ref, *, lq, tk, out_scale):
    P = kc_ref.shape[0]
    near = P - tk

    def span(st, w, bias):
        return (st, lambda: kc_ref[st:st + w, :].astype(BF16), lambda: vc_ref[st:st + w, :].astype(BF16), bias)

    spans = [span(st, min(ATTN_KEY_SPAN, near - st), None) for st in range(0, near, ATTN_KEY_SPAN)]
    spans.append(span(near, tk, bprev_ref[...]))
    spans.append((P, lambda: kn_ref[...], lambda: vn_ref[...], bdiag_ref[:, 0:lq]))
    _attend(_split_q(q_ref[...]), spans, s_ref, mx_ref, l_ref, acc_ref)
    o_ref[...] = _attn_finish(lam_ref, g_ref, l_ref, acc_ref, lq, out_scale).astype(o_ref.dtype)


def _attn_sample(qkv, k_cache, v_cache, bias, lam, g, n_heads, out_scale):
    B, lq, _ = qkv.shape
    H = n_heads
    P = k_cache.shape[2]
    tk = bias.shape[-1]
    assert lq == CHUNK and P % tk == 0 and P % CHUNK == 0 and lq <= tk
    return pl.pallas_call(
        functools.partial(_attn_sample_body, lq=lq, tk=tk, out_scale=out_scale),
        grid=(B, H),
        in_specs=[
            pl.BlockSpec((None, lq, LANES), lambda b, h: (b, 0, h)),
            pl.BlockSpec((None, lq, LANES), lambda b, h: (b, 0, H + h)),
            pl.BlockSpec((None, lq, LANES), lambda b, h: (b, 0, 2 * H + h)),
            pl.BlockSpec((None, None, P, LANES), lambda b, h: (b, h, 0, 0)),
            pl.BlockSpec((None, None, P, LANES), lambda b, h: (b, h, 0, 0)),
            pl.BlockSpec((None, None, lq, tk), lambda b, h: (h, 0, 0, 0)),
            pl.BlockSpec((None, None, lq, tk), lambda b, h: (h, 1, 0, 0)),
            pl.BlockSpec((8, LANES), lambda b, h: (0, 0)),
            pl.BlockSpec((1, LANES), lambda b, h: (0, 0)),
        ],
        out_specs=pl.BlockSpec((None, lq, LANES), lambda b, h: (b, 0, h)),
        out_shape=jax.ShapeDtypeStruct((B, lq, H * LANES), BF16),
        scratch_shapes=[
            pltpu.VMEM((2 * lq, P + LANES), F32),
            pltpu.VMEM((2 * lq, LANES), F32),
            pltpu.VMEM((2 * lq, LANES), F32),
            pltpu.VMEM((2 * lq, LANES), F32),
        ],
        compiler_params=_cparams("parallel", "arbitrary"),
        name="diff_attn_sample",
    )(qkv, qkv, qkv, k_cache, v_cache, bias, bias, lam, g)


def _log_sigmoid(z):
    return jnp.minimum(z, 0.0) - jnp.log1p(jnp.exp(-jnp.abs(z)))


def _logaddexp(a, b):
    return jnp.maximum(a, b) + jnp.log1p(jnp.exp(-jnp.abs(a - b)))


def _hgrn_body(hq_ref, hz_ref, hi_ref, hg_ref, lb_ref, ng_ref, s0_ref, o_ref, s_ref, st_ref, lvl_ref, *, ck,
               layer):
    l = pl.program_id(2)
    tl = hq_ref.shape[0]
    n_lev = ck.bit_length() - 1

    @pl.when(l == 0)
    def _():
        st_ref[...] = s0_ref[...].T

    t_i = lax.broadcasted_iota(jnp.int32, (ck, ck), 0)
    s_i = lax.broadcasted_iota(jnp.int32, (ck, ck), 1)
    x = t_i ^ s_i
    hb = jnp.zeros((ck, ck), jnp.int32)
    for b in range(1, n_lev):
        hb = hb + (x >= (1 << b)).astype(jnp.int32)
    lvl_ref[...] = jnp.where(t_i > s_i, hb, -1)

    lg = lb_ref[...]
    e = jnp.exp(lg - jnp.max(lg, axis=0, keepdims=True))
    p = e / jnp.sum(e, axis=0, keepdims=True)
    cum = p[0:1]
    for d in range(1, layer + 1):
        cum = cum + p[d:d + 1]
    lb = cum - p[0:1]
    log_lb = jnp.log(lb)
    log_1mlb = jnp.log1p(-lb)
    row = lax.broadcasted_iota(jnp.int32, (ck, LANES), 0)

    def chunk(c, carry):
        rs = pl.ds(pl.multiple_of(c * ck, ck), ck)
        hq = hq_ref[rs, :]
        z = hz_ref[rs, :]
        v = hi_ref[rs, :]
        hg = hg_ref[rs, :]
        q = hq * jax.nn.sigmoid(hq)
        ls = _log_sigmoid(z)
        g = _logaddexp(log_lb, log_1mlb + ls) * LOG2E
        k = jnp.exp(log_1mlb + ls - z)
        vb = v.astype(BF16)

        pf = g
        tot = g
        lvl = lvl_ref[...]
        a = jnp.zeros((ck, ck), F32)
        for lev in range(n_lev):
            hsz = 1 << lev
            ql = (q * jnp.exp2(pf)).astype(BF16)
            kl = (k if lev == 0 else k * jnp.exp2(tot - pf)).astype(BF16)
            a = jnp.where(lvl == lev, _dot_nt(ql, kl), a)
            second = (row & hsz) != 0
            up = pltpu.roll(tot, hsz, 0)
            dn = pltpu.roll(tot, ck - hsz, 0)
            pf = pf + jnp.where(second, up, 0.0)
            tot = tot + jnp.where(second, up, dn)
        st = st_ref[...]
        o = _dot(a.astype(BF16), vb) + jnp.sum(q * k, axis=1, keepdims=True) * v
        o = o + _dot_nt((q * jnp.exp2(pf)).astype(BF16), st.astype(BF16))
        kst = (k * jnp.exp2(tot - pf)).astype(BF16)
        st_ref[...] = st * jnp.exp2(tot[0:1, :]) + _dot(v.T.astype(BF16), kst)
        y = o * lax.rsqrt(jnp.mean(o * o, axis=-1, keepdims=True) + EPS) * ng_ref[...]
        y = y * (hg * jax.nn.sigmoid(hg))
        o_ref[rs, :] = y.astype(o_ref.dtype)
        return carry

    n_chunks = tl // ck
    lax.fori_loop(0, n_chunks, chunk, 0, unroll=2 if n_chunks % 2 == 0 else 1)

    @pl.when(l == pl.num_programs(2) - 1)
    def _():
        s_ref[...] = st_ref[...].T


def _hgrn(h, lb_logits, norm_g, s0, layer, n_heads, tl=2048, ck=128):
    B, L, _ = h.shape
    H = n_heads
    tl = _pick(L, tl)
    ck = _pick(tl, ck)
    assert ck & (ck - 1) == 0 and ck >= 8
    depth = lb_logits.shape[0]
    blk = lambda part: pl.BlockSpec((None, tl, LANES), lambda b, hd, l: (b, l, part * H + hd))
    return pl.pallas_call(
        functools.partial(_hgrn_body, ck=ck, layer=layer),
        grid=(B, H, L // tl),
        in_specs=[
            blk(0), blk(1), blk(2), blk(3),
            pl.BlockSpec((depth, LANES), lambda b, hd, l: (0, hd)),
            pl.BlockSpec((1, LANES), lambda b, hd, l: (0, 0)),
            pl.BlockSpec((None, None, LANES, LANES), lambda b, hd, l: (b, hd, 0, 0)),
        ],
        out_specs=[
            pl.BlockSpec((None, tl, LANES), lambda b, hd, l: (b, l, hd)),
            pl.BlockSpec((None, None, LANES, LANES), lambda b, hd, l: (b, hd, 0, 0)),
        ],
        out_shape=[
            jax.ShapeDtypeStruct((B, L, H * LANES), BF16),
            jax.ShapeDtypeStruct((B, H, LANES, LANES), F32),
        ],
        scratch_shapes=[pltpu.VMEM((LANES, LANES), F32), pltpu.VMEM((ck, ck), jnp.int32)],
        compiler_params=_cparams("parallel", "parallel", "arbitrary"),
        name="hgrn_scan",
    )(h, h, h, h, lb_logits, norm_g, s0)


def _trunk(x, p, k_cache, v_cache, pool_hist, hg_state, W, attn_prep):
    B, L, D = x.shape
    depth = W["ln_g"].shape[0]
    alpha = (2 * depth) ** 0.25
    T = B * L
    xt = x.reshape(T, D)
    new_k, new_v, new_pool, new_s = [], [], [], []
    pool_width = W["pool_scale"].shape[-1]
    da_width = D - pool_width
    n_da_heads = da_width // LANES
    n_hg_heads = D // LANES
    lnrow = lambda a, i, s: a[i, s].reshape(1, D)
    for i in range(depth):
        xt = _ffn_ln(xt, W["wg"], W["wu"], W["wd"], lnrow(W["ln_g"], i, 0), lnrow(W["ln_b"], i, 0), i, 0, alpha)
        if i % 2 == 0:
            e = i // 2
            u, k, v, qkv = _in_even(xt, W["w_in_even"], e, B, L, pool_width, da_width, (LANES // 2) ** -0.5)
            past = 0 if k_cache is None else k_cache.shape[2]
            pool_out, nh = _pool(u.reshape(B, L, pool_width), pool_hist[e], W["pool_w"][e],
                                 W["pool_scale"][e].reshape(1, pool_width), past)
            bias, lam = attn_prep[e]
            lam_init = 0.8 - 0.6 * math.exp(-0.3 * i)
            g = W["diff_norm_g"][e].reshape(1, LANES)
            qkv3 = qkv.reshape(B, L, 3 * da_width)
            if k_cache is None:
                o = _attn_prompt(qkv3, bias, lam, g, n_da_heads, 1.0 - lam_init)
            else:
                kc = jnp.transpose(k_cache[e], (0, 2, 1, 3))
                vc = jnp.transpose(v_cache[e], (0, 2, 1, 3))
                o = _attn_sample(qkv3, kc, vc, bias, lam, g, n_da_heads, 1.0 - lam_init)
            parts = [pool_out.reshape(T, pool_width), o.reshape(T, da_width)]
            w_out = W["w_out_even"]
            new_k.append(jnp.transpose(k, (0, 2, 1, 3)))
            new_v.append(jnp.transpose(v, (0, 2, 1, 3)))
            new_pool.append(nh)
            li = e
        else:
            od = i // 2
            h = _matmul(xt, W["w_in_odd"], od)
            o, s = _hgrn(h.reshape(B, L, 4 * D), W["lb_logits"], W["hgrn_norm_g"][od].reshape(1, LANES),
                         hg_state[od], i, n_hg_heads)
            parts = [o.reshape(T, D)]
            w_out = W["w_out_odd"]
            new_s.append(s)
            li = od
        xt = _out_ln(parts, xt, w_out, li, lnrow(W["ln_g"], i, 1), lnrow(W["ln_b"], i, 1), alpha)
        xt = _ffn_ln(xt, W["wg"], W["wu"], W["wd"], lnrow(W["ln_g"], i, 2), lnrow(W["ln_b"], i, 2), i, 1, alpha)
        xt = _ple(xt, p.reshape(depth, T, -1), W["w_ple_gate"], W["w_ple_up"], i)
    return xt.reshape(B, L, D), jnp.stack(new_k), jnp.stack(new_v), jnp.stack(new_pool), jnp.stack(new_s)


ATTN_TILE = 256


def kernel(x_prompt, x_sample, cache_diff_k, cache_diff_v, state_pool, state_hgrn, p_prompt, p_sample, ln_g, ln_b, w_ffn_gate, w_ffn_up, w_ffn_down, w_ple_gate, w_ple_up, w_in_even, w_out_even, pool_w, pool_scale, lam_q1, lam_k1, lam_q2, lam_k2, diff_norm_g, rel_bias, w_in_odd, w_out_odd, hgrn_norm_g, hgrn_lb_logits):
    bf = lambda a: a.astype(BF16)
    W = dict(
        ln_g=ln_g.astype(F32), ln_b=ln_b.astype(F32),
        wg=bf(w_ffn_gate), wu=bf(w_ffn_up), wd=bf(w_ffn_down),
        w_ple_gate=bf(w_ple_gate), w_ple_up=bf(w_ple_up),
        w_in_even=bf(w_in_even), w_out_even=bf(w_out_even),
        pool_w=bf(pool_w), pool_scale=pool_scale.astype(F32),
        diff_norm_g=diff_norm_g.astype(F32),
        w_in_odd=bf(w_in_odd), w_out_odd=bf(w_out_odd),
        hgrn_norm_g=hgrn_norm_g.astype(F32), lb_logits=hgrn_lb_logits.astype(F32),
    )
    n_even = w_in_even.shape[0]
    n_odd = w_in_odd.shape[0]
    attn_prep = []
    for e in range(n_even):
        lam_init = 0.8 - 0.6 * math.exp(-0.3 * (2 * e))
        attn_prep.append(_attn_prep(rel_bias, lam_q1[e], lam_k1[e], lam_q2[e], lam_k2[e], ATTN_TILE, lam_init))

    B = x_prompt.shape[0]
    dt = x_prompt.dtype
    zero_pool = jnp.zeros((n_even, B) + state_pool.shape[2:], dt)
    zero_s = jnp.zeros((n_odd, B) + state_hgrn.shape[2:], dt)
    y_p, k_p, v_p, pool_p, s_p = _trunk(x_prompt, p_prompt, None, None, zero_pool, zero_s, W, attn_prep)
    y_s, k_s, v_s, pool_s, s_s = _trunk(x_sample, p_sample, cache_diff_k, cache_diff_v, state_pool, state_hgrn,
                                        W, attn_prep)
    return (y_p, y_s, k_p, v_p, k_s, v_s, pool_p, pool_s, s_p, s_s)
```

```python
import functools
import math

import jax
import jax.numpy as jnp
from jax import lax
from jax.experimental import pallas as pl
from jax.experimental.pallas import tpu as pltpu

F32 = jnp.float32
BF16 = jnp.bfloat16

CHUNK = 64
POOL_WINDOWS = (2, 4, 8, 16)
POOL_HIST = max(POOL_WINDOWS) - 1
N_BUCKETS = 32
MAX_DISTANCE = 128
EPS = 1e-5
NEG = -1e30
LANES = 128

VMEM_LIMIT = 56 * 1024 * 1024


def _cparams(*sem):
    return pltpu.CompilerParams(dimension_semantics=sem, vmem_limit_bytes=VMEM_LIMIT)


def _pick(n, pref):
    if n <= pref:
        return n
    t = pref
    while n % t:
        t //= 2
    return t


def _layer_norm(y, g, b):
    mu = jnp.mean(y, axis=-1, keepdims=True)
    d = y - mu
    var = jnp.mean(d * d, axis=-1, keepdims=True)
    return d * lax.rsqrt(var + EPS) * g + b


def _dot(a, b):
    return jnp.dot(a, b, preferred_element_type=F32)


def _dot_nt(a, b):
    return lax.dot_general(a, b, (((1,), (1,)), ((), ())), preferred_element_type=F32)


def _ffn_ln_body(x_ref, wg_ref, wu_ref, wd_ref, g_ref, b_ref, *refs, alpha):
    *o_refs, xb_ref, acc_ref = refs
    j = pl.program_id(1)

    @pl.when(j == 0)
    def _():
        xb_ref[...] = x_ref[...].astype(BF16)
        acc_ref[...] = jnp.zeros_like(acc_ref)

    xb = xb_ref[...]
    hg = _dot(xb, wg_ref[...])
    hu = _dot(xb, wu_ref[...])
    act = hg * jax.nn.sigmoid(hg) * hu
    acc_ref[...] += _dot(act.astype(BF16), wd_ref[...])

    @pl.when(j == pl.num_programs(1) - 1)
    def _():
        y = _layer_norm(alpha * x_ref[...] + 0.5 * acc_ref[...], g_ref[...], b_ref[...])
        for o_ref in o_refs:
            o_ref[...] = y.astype(o_ref.dtype)


def _ffn_ln(x, wg, wu, wd, g, b, li, si, alpha, also_bf16=False, tm=512, tf=512):
    T, D = x.shape
    F = wg.shape[-1]
    tm = _pick(T, tm)
    tf = _pick(F, tf)
    out_dtypes = (F32, BF16) if also_bf16 else (F32,)
    outs = pl.pallas_call(
        functools.partial(_ffn_ln_body, alpha=alpha),
        grid=(T // tm, F // tf),
        in_specs=[
            pl.BlockSpec((tm, D), lambda t, j: (t, 0)),
            pl.BlockSpec((None, None, D, tf), lambda t, j: (li, si, 0, j)),
            pl.BlockSpec((None, None, D, tf), lambda t, j: (li, si, 0, j)),
            pl.BlockSpec((None, None, tf, D), lambda t, j: (li, si, j, 0)),
            pl.BlockSpec((1, D), lambda t, j: (0, 0)),
            pl.BlockSpec((1, D), lambda t, j: (0, 0)),
        ],
        out_specs=[pl.BlockSpec((tm, D), lambda t, j: (t, 0)) for _ in out_dtypes],
        out_shape=[jax.ShapeDtypeStruct((T, D), dt) for dt in out_dtypes],
        scratch_shapes=[pltpu.VMEM((tm, D), BF16), pltpu.VMEM((tm, D), F32)],
        compiler_params=_cparams("parallel", "arbitrary"),
        name="ffn_ln",
    )(x, wg, wu, wd, g, b)
    return tuple(outs) if also_bf16 else outs[0]


def _matmul_body(x_ref, w_ref, o_ref, xb_ref):
    @pl.when(pl.program_id(1) == 0)
    def _():
        xb_ref[...] = x_ref[...].astype(BF16)

    o_ref[...] = _dot(xb_ref[...], w_ref[...]).astype(o_ref.dtype)


def _matmul(x, w, li, out_dtype=F32, tm=1024, tn=1024):
    T, K = x.shape
    N = w.shape[-1]
    tm = _pick(T, tm)
    tn = _pick(N, tn)
    return pl.pallas_call(
        _matmul_body,
        grid=(T // tm, N // tn),
        in_specs=[
            pl.BlockSpec((tm, K), lambda t, j: (t, 0)),
            pl.BlockSpec((None, K, tn), lambda t, j: (li, 0, j)),
        ],
        out_specs=pl.BlockSpec((tm, tn), lambda t, j: (t, j)),
        out_shape=jax.ShapeDtypeStruct((T, N), out_dtype),
        scratch_shapes=[pltpu.VMEM((tm, K), BF16)],
        compiler_params=_cparams("parallel", "arbitrary"),
        name="in_proj",
    )(x, w)


def _in_even_body(x_ref, w_ref, u_ref, k_ref, v_ref, qkv_ref, xb_ref, *, nu, nh, qscale):
    j = pl.program_id(1)

    @pl.when(j == 0)
    def _():
        xb_ref[...] = x_ref[...].astype(BF16)

    r = _dot(xb_ref[...], w_ref[...])

    def put_heads(ref):
        bb, hpt, tl, _ = ref.shape
        for hh in range(hpt):
            ref[:, hh] = r[:, hh * LANES:(hh + 1) * LANES].reshape(bb, tl, LANES)

    @pl.when(j < nu)
    def _():
        u_ref[...] = r

    @pl.when((j >= nu) & (j < nu + nh))
    def _():
        qkv_ref[...] = (r * qscale).astype(BF16)

    @pl.when((j >= nu + nh) & (j < nu + 2 * nh))
    def _():
        put_heads(k_ref)
        qkv_ref[...] = r.astype(BF16)

    @pl.when(j >= nu + 2 * nh)
    def _():
        put_heads(v_ref)
        qkv_ref[...] = r.astype(BF16)


def _in_even(x, w, li, B, L, pool_width, da_width, qscale, tm=1024, tn=512):
    T, K = x.shape
    tm = _pick(T, tm)
    assert pool_width % tn == 0 and da_width % tn == 0 and tn % LANES == 0
    nu, nh = pool_width // tn, da_width // tn
    n_tiles = nu + 3 * nh
    hpt = tn // LANES
    tl = min(L, tm)
    assert tm % tl == 0 and L % tl == 0
    bb, nl = tm // tl, L // tl

    def clamp(j, lo, n):
        return jnp.clip(j - lo, 0, n - 1)

    def head_spec(lo):
        return pl.BlockSpec((bb, hpt, tl, LANES), lambda t, j: (t // nl, clamp(j, lo, nh), t % nl, 0))

    return pl.pallas_call(
        functools.partial(_in_even_body, nu=nu, nh=nh, qscale=qscale),
        grid=(T // tm, n_tiles),
        in_specs=[
            pl.BlockSpec((tm, K), lambda t, j: (t, 0)),
            pl.BlockSpec((None, K, tn), lambda t, j: (li, 0, j)),
        ],
        out_specs=[
            pl.BlockSpec((tm, tn), lambda t, j: (t, clamp(j, 0, nu))),
            head_spec(nu + nh),
            head_spec(nu + 2 * nh),
            pl.BlockSpec((tm, tn), lambda t, j: (t, clamp(j, nu, 3 * nh))),
        ],
        out_shape=[
            jax.ShapeDtypeStruct((T, pool_width), F32),
            jax.ShapeDtypeStruct((B, da_width // LANES, L, LANES), F32),
            jax.ShapeDtypeStruct((B, da_width // LANES, L, LANES), F32),
            jax.ShapeDtypeStruct((T, 3 * da_width), BF16),
        ],
        scratch_shapes=[pltpu.VMEM((tm, K), BF16)],
        compiler_params=_cparams("parallel", "arbitrary"),
        name="in_proj_even",
    )(x, w)


def _out_ln_body(*refs, alpha, widths):
    n = len(widths)
    parts = refs[:n]
    x_ref, w_ref, g_ref, b_ref, o_ref = refs[n:]
    acc = alpha * x_ref[...]
    off = 0
    for p_ref, wd in zip(parts, widths):
        acc = acc + _dot(p_ref[...], w_ref[off:off + wd, :])
        off += wd
    o_ref[...] = _layer_norm(acc, g_ref[...], b_ref[...])


def _out_ln(parts, x, w, li, g, b, alpha, tm=512):
    T, D = x.shape
    tm = _pick(T, tm)
    widths = tuple(p.shape[1] for p in parts)
    kin = sum(widths)
    return pl.pallas_call(
        functools.partial(_out_ln_body, alpha=alpha, widths=widths),
        grid=(T // tm,),
        in_specs=[pl.BlockSpec((tm, wd), lambda t: (t, 0)) for wd in widths] + [
            pl.BlockSpec((tm, D), lambda t: (t, 0)),
            pl.BlockSpec((None, kin, D), lambda t: (li, 0, 0)),
            pl.BlockSpec((1, D), lambda t: (0, 0)),
            pl.BlockSpec((1, D), lambda t: (0, 0)),
        ],
        out_specs=pl.BlockSpec((tm, D), lambda t: (t, 0)),
        out_shape=jax.ShapeDtypeStruct((T, D), F32),
        compiler_params=_cparams("parallel"),
        name="out_proj_ln",
    )(*parts, x, w, g, b)


def _ple_body(x_ref, p_ref, wg_ref, wu_ref, o_ref, *, tn):
    xb = x_ref[...].astype(BF16)
    pb = p_ref[...].astype(BF16)
    D = o_ref.shape[1]
    for c in range(D // tn):
        sl = slice(c * tn, (c + 1) * tn)
        gate = jax.nn.sigmoid(_dot(xb, wg_ref[:, sl]))
        up = _dot(pb, wu_ref[:, sl])
        o_ref[:, sl] = x_ref[:, sl] + gate * up


def _ple(x, p, wg, wu, li, tm=512, tn=512):
    T, D = x.shape
    P = p.shape[2]
    tm = _pick(T, tm)
    return pl.pallas_call(
        functools.partial(_ple_body, tn=_pick(D, tn)),
        grid=(T // tm,),
        in_specs=[
            pl.BlockSpec((tm, D), lambda t: (t, 0)),
            pl.BlockSpec((None, tm, P), lambda t: (li, t, 0)),
            pl.BlockSpec((None, D, D), lambda t: (li, 0, 0)),
            pl.BlockSpec((None, P, D), lambda t: (li, 0, 0)),
        ],
        out_specs=pl.BlockSpec((tm, D), lambda t: (t, 0)),
        out_shape=jax.ShapeDtypeStruct((T, D), F32),
        compiler_params=_cparams("parallel"),
        name="ple_gate",
    )(x, p, wg, wu)


def _pool_body(u_ref, hist_ref, w_ref, sc_ref, o_ref, nh_ref, ext_ref, *, tl, start_pos):
    l = pl.program_id(1)
    nl = pl.num_programs(1)
    H = POOL_HIST + 1

    @pl.when(l == 0)
    def _():
        ext_ref[0:1, :] = jnp.zeros((1, ext_ref.shape[1]), F32)
        ext_ref[1:H, :] = hist_ref[...]

    @pl.when(l > 0)
    def _():
        ext_ref[0:H, :] = ext_ref[tl:tl + H, :]

    ext_ref[H:H + tl, :] = u_ref[...]

    pos = start_pos + l * tl + lax.broadcasted_iota(jnp.int32, (tl, 1), 0)
    gd = LANES
    for g, wnd in enumerate(POOL_WINDOWS):
        cs = slice(g * gd, (g + 1) * gd)
        s = ext_ref[H:H + tl, cs]
        cur = s
        for d in range(1, wnd):
            s = s + ext_ref[H - d:H - d + tl, cs]
        cnt = jnp.minimum(pos + 1, wnd).astype(F32)
        pooled = s / cnt - cur
        y = _dot(pooled.astype(BF16), w_ref[g]) * sc_ref[:, cs]
        o_ref[:, cs] = y.astype(o_ref.dtype)

    @pl.when(l == nl - 1)
    def _():
        nh_ref[...] = ext_ref[tl + 1:tl + H, :]


def _pool(u, hist, w, scale, start_pos, tl=512):
    B, L, PW = u.shape
    tl = _pick(L, tl)
    assert tl >= POOL_HIST + 1
    return pl.pallas_call(
        functools.partial(_pool_body, tl=tl, start_pos=start_pos),
        grid=(B, L // tl),
        in_specs=[
            pl.BlockSpec((None, tl, PW), lambda b, l: (b, l, 0)),
            pl.BlockSpec((None, POOL_HIST, PW), lambda b, l: (b, 0, 0)),
            pl.BlockSpec(w.shape, lambda b, l: (0, 0, 0)),
            pl.BlockSpec((1, PW), lambda b, l: (0, 0)),
        ],
        out_specs=[
            pl.BlockSpec((None, tl, PW), lambda b, l: (b, l, 0)),
            pl.BlockSpec((None, POOL_HIST, PW), lambda b, l: (b, 0, 0)),
        ],
        out_shape=[
            jax.ShapeDtypeStruct((B, L, PW), BF16),
            jax.ShapeDtypeStruct((B, POOL_HIST, PW), F32),
        ],
        scratch_shapes=[pltpu.VMEM((tl + POOL_HIST + 1, PW), F32)],
        compiler_params=_cparams("parallel", "arbitrary"),
        name="pool_mixer",
    )(u, hist, w, scale)


def _t5_bucket(rel):
    nb = N_BUCKETS // 2
    max_exact = nb // 2
    n = jnp.abs(rel)
    nf = jnp.maximum(n, 1).astype(jnp.float32)
    large = max_exact + (jnp.log(nf / max_exact) / math.log(MAX_DISTANCE / max_exact)
                         * (nb - max_exact)).astype(jnp.int32)
    large = jnp.minimum(large, nb - 1)
    return jnp.where(rel > 0, nb, 0) + jnp.where(n < max_exact, n, large)


def _attn_prep_body(tbl_ref, bkt_ref, lq1_ref, lk1_ref, lq2_ref, lk2_ref, bias_ref, lam_ref, *, tq, lam_init,
                    far_bucket):
    h = pl.program_id(0)
    bkt = bkt_ref[...]
    far = tbl_ref[far_bucket, h]
    acc = jnp.zeros(bkt.shape, F32)
    for b in range(N_BUCKETS):
        acc = jnp.where(bkt == b, tbl_ref[b, h] - far, acc)
    r = lax.broadcasted_iota(jnp.int32, bkt.shape, 1)
    c = lax.broadcasted_iota(jnp.int32, bkt.shape, 2)
    t = lax.broadcasted_iota(jnp.int32, bkt.shape, 0)
    visible = (t == 0) | ((c // CHUNK) <= (r // CHUNK))
    bias_ref[...] = jnp.where(visible, acc, NEG)
    e1 = jnp.exp(jnp.sum(lq1_ref[...] * lk1_ref[...], axis=-1, keepdims=True))
    e2 = jnp.exp(jnp.sum(lq2_ref[...] * lk2_ref[...], axis=-1, keepdims=True))
    lam_ref[...] = jnp.broadcast_to(e1 - e2 + lam_init, lam_ref.shape)


def _attn_prep(rel_bias, lq1, lk1, lq2, lk2, tq, lam_init):
    nbk, H = rel_bias.shape
    r = jnp.arange(tq, dtype=jnp.int32)[:, None]
    c = jnp.arange(tq, dtype=jnp.int32)[None, :]
    bkt = jnp.stack([_t5_bucket(c - r - tq), _t5_bucket(c - r)])
    far_bucket = N_BUCKETS // 2 - 1
    assert tq >= MAX_DISTANCE
    row = lambda a: a.reshape(1, -1).astype(F32)
    return pl.pallas_call(
        functools.partial(_attn_prep_body, tq=tq, lam_init=lam_init, far_bucket=far_bucket),
        grid=(H,),
        in_specs=[
            pl.BlockSpec(memory_space=pltpu.SMEM),
            pl.BlockSpec((2, tq, tq), lambda h: (0, 0, 0)),
            pl.BlockSpec((1, lq1.shape[-1]), lambda h: (0, 0)),
            pl.BlockSpec((1, lq1.shape[-1]), lambda h: (0, 0)),
            pl.BlockSpec((1, lq1.shape[-1]), lambda h: (0, 0)),
            pl.BlockSpec((1, lq1.shape[-1]), lambda h: (0, 0)),
        ],
        out_specs=[
            pl.BlockSpec((None, 2, tq, tq), lambda h: (h, 0, 0, 0)),
            pl.BlockSpec((8, LANES), lambda h: (0, 0)),
        ],
        out_shape=[
            jax.ShapeDtypeStruct((H, 2, tq, tq), F32),
            jax.ShapeDtypeStruct((8, LANES), F32),
        ],
        compiler_params=_cparams("arbitrary"),
        name="attn_prep",
    )(rel_bias.astype(F32), bkt, row(lq1), row(lk1), row(lq2), row(lk2))


def _split_q(q):
    lane = lax.broadcasted_iota(jnp.int32, q.shape, 1)
    half = q.shape[1] // 2
    zero = jnp.zeros_like(q)
    return jnp.concatenate([jnp.where(lane < half, q, zero), jnp.where(lane >= half, q, zero)], axis=0)


def _attend(qq, spans, s_ref, mx_ref, l_ref, acc_ref):
    rows = qq.shape[0]
    mx_ref[...] = jnp.full(mx_ref.shape, NEG, F32)
    for col, get_k, _, bias in spans:
        s = _dot_nt(qq, get_k())
        w = s.shape[1]
        if bias is not None:
            s = (s.reshape(2, rows // 2, w) + bias[None]).reshape(rows, w)
        s_ref[:, col:col + w] = s
        if w % LANES == 0:
            m = functools.reduce(jnp.maximum, [s[:, c:c + LANES] for c in range(0, w, LANES)])
            mx_ref[...] = jnp.maximum(mx_ref[...], m)
        else:
            mx_ref[:, 0:w] = jnp.maximum(mx_ref[:, 0:w], s)
    m_b = jnp.broadcast_to(jnp.max(mx_ref[...], axis=1, keepdims=True), mx_ref.shape)
    mx_ref[...] = m_b
    l_ref[...] = jnp.zeros_like(l_ref)
    acc_ref[...] = jnp.zeros_like(acc_ref)
    for col, get_k, get_v, _ in spans:
        w = get_v().shape[0]
        s = s_ref[:, col:col + w]
        m_b = mx_ref[...]
        if w % LANES == 0:
            ps = [jnp.exp(s[:, c:c + LANES] - m_b) for c in range(0, w, LANES)]
            l_ref[...] += functools.reduce(jnp.add, ps)
            p = ps[0] if len(ps) == 1 else jnp.concatenate(ps, axis=1)
        else:
            p = jnp.exp(s - m_b[:, 0:w])
            l_ref[:, 0:w] += p
        acc_ref[...] += _dot(p.astype(BF16), get_v())


def _attn_finish(lam_ref, g_ref, l_ref, acc_ref, tq, out_scale):
    lam = lam_ref[0:1, 0:1]
    o = acc_ref[...] / jnp.sum(l_ref[...], axis=1, keepdims=True)
    o = o[:tq] - lam * o[tq:]
    return o * lax.rsqrt(jnp.mean(o * o, axis=-1, keepdims=True) + EPS) * g_ref[...] * out_scale


ATTN_KEY_SPAN = 512


def _attn_prompt_body(q_ref, k_ref, v_ref, bias_ref, lam_ref, g_ref, o_ref, s_ref, mx_ref, l_ref, acc_ref, *,
                      tq, out_scale):
    L = q_ref.shape[0]

    def span(st, w, bias):
        return (st, lambda: k_ref[st:st + w, :], lambda: v_ref[st:st + w, :], bias)

    for qi in range(L // tq):
        par = qi % 2
        far_end = max(qi - 1, 0) * tq
        spans = [span(st, min(ATTN_KEY_SPAN, far_end - st), None) for st in range(0, far_end, ATTN_KEY_SPAN)]
        if qi >= 1:
            spans.append(span((qi - 1) * tq, tq, bias_ref[0]))
        spans.append(span(qi * tq, tq, bias_ref[1]))
        qq = _split_q(q_ref[qi * tq:(qi + 1) * tq, :])
        _attend(qq, spans, s_ref.at[par], mx_ref.at[par], l_ref.at[par], acc_ref.at[par])
        y = _attn_finish(lam_ref, g_ref, l_ref.at[par], acc_ref.at[par], tq, out_scale)
        o_ref[qi * tq:(qi + 1) * tq, :] = y.astype(o_ref.dtype)


def _attn_prompt(qkv, bias, lam, g, n_heads, out_scale):
    B, L, _ = qkv.shape
    H = n_heads
    tq = bias.shape[-1]
    assert L % tq == 0 and tq % CHUNK == 0
    return pl.pallas_call(
        functools.partial(_attn_prompt_body, tq=tq, out_scale=out_scale),
        grid=(B, H),
        in_specs=[
            pl.BlockSpec((None, L, LANES), lambda b, h: (b, 0, h)),
            pl.BlockSpec((None, L, LANES), lambda b, h: (b, 0, H + h)),
            pl.BlockSpec((None, L, LANES), lambda b, h: (b, 0, 2 * H + h)),
            pl.BlockSpec((None, 2, tq, tq), lambda b, h: (h, 0, 0, 0)),
            pl.BlockSpec((8, LANES), lambda b, h: (0, 0)),
            pl.BlockSpec((1, LANES), lambda b, h: (0, 0)),
        ],
        out_specs=pl.BlockSpec((None, L, LANES), lambda b, h: (b, 0, h)),
        out_shape=jax.ShapeDtypeStruct((B, L, H * LANES), BF16),
        scratch_shapes=[
            pltpu.VMEM((2, 2 * tq, L), F32),
            pltpu.VMEM((2, 2 * tq, LANES), F32),
            pltpu.VMEM((2, 2 * tq, LANES), F32),
            pltpu.VMEM((2, 2 * tq, LANES), F32),
        ],
        compiler_params=_cparams("parallel", "parallel"),
        name="diff_attn_prompt",
    )(qkv, qkv, qkv, bias, lam, g)


def _attn_sample_body(q_ref, kn_ref, vn_ref, kc_ref, vc_ref, bprev_ref, bdiag_ref, lam_ref, g_ref, o_ref,
                      s_ref, mx_ref, l_ref, acc_ref, *, lq, tk, out_scale):
    P = kc_ref.shape[0]
    near = P - tk

    def span(st, w, bias):
        return (st, lambda: kc_ref[st:st + w, :].astype(BF16), lambda: vc_ref[st:st + w, :].astype(BF16), bias)

    spans = [span(st, min(ATTN_KEY_SPAN, near - st), None) for st in range(0, near, ATTN_KEY_SPAN)]
    spans.append(span(near, tk, bprev_ref[...]))
    spans.append((P, lambda: kn_ref[...], lambda: vn_ref[...], bdiag_ref[:, 0:lq]))
    _attend(_split_q(q_ref[...]), spans, s_ref, mx_ref, l_ref, acc_ref)
    o_ref[...] = _attn_finish(lam_ref, g_ref, l_ref, acc_ref, lq, out_scale).astype(o_ref.dtype)


def _attn_sample(qkv, k_cache, v_cache, bias, lam, g, n_heads, out_scale):
    B, lq, _ = qkv.shape
    H = n_heads
    P = k_cache.shape[2]
    tk = bias.shape[-1]
    assert lq == CHUNK and P % tk == 0 and P % CHUNK == 0 and lq <= tk
    return pl.pallas_call(
        functools.partial(_attn_sample_body, lq=lq, tk=tk, out_scale=out_scale),
        grid=(B, H),
        in_specs=[
            pl.BlockSpec((None, lq, LANES), lambda b, h: (b, 0, h)),
            pl.BlockSpec((None, lq, LANES), lambda b, h: (b, 0, H + h)),
            pl.BlockSpec((None, lq, LANES), lambda b, h: (b, 0, 2 * H + h)),
            pl.BlockSpec((None, None, P, LANES), lambda b, h: (b, h, 0, 0)),
            pl.BlockSpec((None, None, P, LANES), lambda b, h: (b, h, 0, 0)),
            pl.BlockSpec((None, None, lq, tk), lambda b, h: (h, 0, 0, 0)),
            pl.BlockSpec((None, None, lq, tk), lambda b, h: (h, 1, 0, 0)),
            pl.BlockSpec((8, LANES), lambda b, h: (0, 0)),
            pl.BlockSpec((1, LANES), lambda b, h: (0, 0)),
        ],
        out_specs=pl.BlockSpec((None, lq, LANES), lambda b, h: (b, 0, h)),
        out_shape=jax.ShapeDtypeStruct((B, lq, H * LANES), BF16),
        scratch_shapes=[
            pltpu.VMEM((2 * lq, P + LANES), F32),
            pltpu.VMEM((2 * lq, LANES), F32),
            pltpu.VMEM((2 * lq, LANES), F32),
            pltpu.VMEM((2 * lq, LANES), F32),
        ],
        compiler_params=_cparams("parallel", "arbitrary"),
        name="diff_attn_sample",
    )(qkv, qkv, qkv, k_cache, v_cache, bias, bias, lam, g)


MIN_LOG2 = -150.0


def _hgrn_setup(lb_ref, lvl_ref, ck, layer):
    n_lev = ck.bit_length() - 1
    t_i = lax.broadcasted_iota(jnp.int32, (ck, ck), 0)
    s_i = lax.broadcasted_iota(jnp.int32, (ck, ck), 1)
    x = t_i ^ s_i
    hb = jnp.zeros((ck, ck), jnp.int32)
    for b in range(1, n_lev):
        hb = hb + (x >= (1 << b)).astype(jnp.int32)
    lvl_ref[...] = jnp.where(t_i > s_i, hb, -1)

    lg = lb_ref[...]
    e = jnp.exp(lg - jnp.max(lg, axis=0, keepdims=True))
    p = e / jnp.sum(e, axis=0, keepdims=True)
    cum = p[0:1]
    for d in range(1, layer + 1):
        cum = cum + p[d:d + 1]
    lb = cum - p[0:1]
    return lb, 1.0 - lb, lax.broadcasted_iota(jnp.int32, (ck, LANES), 0)


def _hgrn_chunk(hq, z, v, hg, consts, lvl_ref, ng_ref, st_ref):
    lb, one_mlb, row = consts
    ck = hq.shape[0]
    n_lev = ck.bit_length() - 1
    q = hq * jax.nn.sigmoid(hq)
    r = 1.0 / (1.0 + jnp.exp(z))
    k = one_mlb * r
    f = lb + one_mlb * (1.0 - r)
    g = jnp.maximum(jnp.log2(f), MIN_LOG2)
    vb = v.astype(BF16)

    pf = g
    tot = g
    lvl = lvl_ref[...]
    a = jnp.zeros((ck, ck), F32)
    for lev in range(n_lev):
        hsz = 1 << lev
        ql = (q * jnp.exp2(pf)).astype(BF16)
        kl = (k if lev == 0 else k * jnp.exp2(tot - pf)).astype(BF16)
        a = jnp.where(lvl == lev, _dot_nt(ql, kl), a)
        if hsz % 8 == 0:
            nb = ck // (2 * hsz)
            t4 = tot.reshape(nb, 2, hsz, LANES)
            p4 = pf.reshape(nb, 2, hsz, LANES)
            both = t4[:, 0:1] + t4[:, 1:2]
            tot = jnp.concatenate([both, both], axis=1).reshape(ck, LANES)
            pf = jnp.concatenate([p4[:, 0:1], p4[:, 1:2] + t4[:, 0:1]], axis=1).reshape(ck, LANES)
        else:
            second = (row & hsz) != 0
            up = pltpu.roll(tot, hsz, 0)
            dn = pltpu.roll(tot, ck - hsz, 0)
            pf = pf + jnp.where(second, up, 0.0)
            tot = tot + jnp.where(second, up, dn)
    st = st_ref[...]
    o = _dot(a.astype(BF16), vb) + jnp.sum(q * k, axis=1, keepdims=True) * v
    o = o + _dot_nt((q * jnp.exp2(pf)).astype(BF16), st.astype(BF16))
    kst = (k * jnp.exp2(tot - pf)).astype(BF16)
    st_ref[...] = st * jnp.exp2(tot[0:1, :]) + _dot(v.T.astype(BF16), kst)
    y = o * lax.rsqrt(jnp.mean(o * o, axis=-1, keepdims=True) + EPS) * ng_ref[...]
    return y * (hg * jax.nn.sigmoid(hg))


def _hgrn_body(hq_ref, hz_ref, hi_ref, hg_ref, lb_ref, ng_ref, s0_ref, o_ref, s_ref, st_ref, lvl_ref, *, ck,
               layer):
    l = pl.program_id(2)
    tl = hq_ref.shape[0]

    @pl.when(l == 0)
    def _():
        st_ref[...] = s0_ref[...].T

    consts = _hgrn_setup(lb_ref, lvl_ref, ck, layer)

    def chunk(c, carry):
        rs = pl.ds(pl.multiple_of(c * ck, ck), ck)
        y = _hgrn_chunk(hq_ref[rs, :], hz_ref[rs, :], hi_ref[rs, :], hg_ref[rs, :], consts, lvl_ref, ng_ref,
                        st_ref)
        o_ref[rs, :] = y.astype(o_ref.dtype)
        return carry

    n_chunks = tl // ck
    lax.fori_loop(0, n_chunks, chunk, 0, unroll=4 if n_chunks % 4 == 0 else 1)

    @pl.when(l == pl.num_programs(2) - 1)
    def _():
        s_ref[...] = st_ref[...].T


def _hgrn(h, lb_logits, norm_g, s0, layer, n_heads, tl=2048, ck=128):
    B, L, _ = h.shape
    H = n_heads
    tl = _pick(L, tl)
    ck = _pick(tl, ck)
    assert ck & (ck - 1) == 0 and ck >= 8
    depth = lb_logits.shape[0]
    blk = lambda part: pl.BlockSpec((None, tl, LANES), lambda b, hd, l: (b, l, part * H + hd))
    return pl.pallas_call(
        functools.partial(_hgrn_body, ck=ck, layer=layer),
        grid=(B, H, L // tl),
        in_specs=[
            blk(0), blk(1), blk(2), blk(3),
            pl.BlockSpec((depth, LANES), lambda b, hd, l: (0, hd)),
            pl.BlockSpec((1, LANES), lambda b, hd, l: (0, 0)),
            pl.BlockSpec((None, None, LANES, LANES), lambda b, hd, l: (b, hd, 0, 0)),
        ],
        out_specs=[
            pl.BlockSpec((None, tl, LANES), lambda b, hd, l: (b, l, hd)),
            pl.BlockSpec((None, None, LANES, LANES), lambda b, hd, l: (b, hd, 0, 0)),
        ],
        out_shape=[
            jax.ShapeDtypeStruct((B, L, H * LANES), BF16),
            jax.ShapeDtypeStruct((B, H, LANES, LANES), F32),
        ],
        scratch_shapes=[pltpu.VMEM((LANES, LANES), F32), pltpu.VMEM((ck, ck), jnp.int32)],
        compiler_params=_cparams("parallel", "parallel", "arbitrary"),
        name="hgrn_scan",
    )(h, h, h, h, lb_logits, norm_g, s0)


def _hgrn_proj_body(x_ref, w_ref, lb_ref, ng_ref, s0_ref, o_ref, s_ref, st_ref, lvl_ref, h_ref, *, ck, layer,
                    rows):
    L = x_ref.shape[0]
    st_ref[...] = s0_ref[...].T
    consts = _hgrn_setup(lb_ref, lvl_ref, ck, layer)

    def project(s):
        h_ref[s % 2] = _dot(x_ref[s * rows:(s + 1) * rows, :], w_ref[...])

    project(0)
    for s in range(L // rows):
        if (s + 1) * rows < L:
            project(s + 1)
        h = h_ref.at[s % 2]
        for c in range(rows // ck):
            rs = slice(c * ck, (c + 1) * ck)
            y = _hgrn_chunk(h[rs, 0:LANES], h[rs, LANES:2 * LANES], h[rs, 2 * LANES:3 * LANES],
                            h[rs, 3 * LANES:4 * LANES], consts, lvl_ref, ng_ref, st_ref)
            o_ref[s * rows + c * ck:s * rows + (c + 1) * ck, :] = y.astype(o_ref.dtype)
    s_ref[...] = st_ref[...].T


def _hgrn_proj(x, w_heads, lb_logits, norm_g, s0, layer, ck=128, rows=512):
    B, L, D = x.shape
    H = w_heads.shape[0]
    assert L % rows == 0 and rows % ck == 0 and ck & (ck - 1) == 0
    depth = lb_logits.shape[0]
    return pl.pallas_call(
        functools.partial(_hgrn_proj_body, ck=ck, layer=layer, rows=rows),
        grid=(B, H),
        in_specs=[
            pl.BlockSpec((None, L, D), lambda b, hd: (b, 0, 0)),
            pl.BlockSpec((None, D, 4 * LANES), lambda b, hd: (hd, 0, 0)),
            pl.BlockSpec((depth, LANES), lambda b, hd: (0, hd)),
            pl.BlockSpec((1, LANES), lambda b, hd: (0, 0)),
            pl.BlockSpec((None, None, LANES, LANES), lambda b, hd: (b, hd, 0, 0)),
        ],
        out_specs=[
            pl.BlockSpec((None, L, LANES), lambda b, hd: (b, 0, hd)),
            pl.BlockSpec((None, None, LANES, LANES), lambda b, hd: (b, hd, 0, 0)),
        ],
        out_shape=[
            jax.ShapeDtypeStruct((B, L, H * LANES), BF16),
            jax.ShapeDtypeStruct((B, H, LANES, LANES), F32),
        ],
        scratch_shapes=[
            pltpu.VMEM((LANES, LANES), F32),
            pltpu.VMEM((ck, ck), jnp.int32),
            pltpu.VMEM((2, rows, 4 * LANES), F32),
        ],
        compiler_params=_cparams("parallel", "arbitrary"),
        name="hgrn_proj_scan",
    )(x, w_heads, lb_logits, norm_g, s0)


def _trunk(x, p, k_cache, v_cache, pool_hist, hg_state, W, attn_prep):
    B, L, D = x.shape
    depth = W["ln_g"].shape[0]
    alpha = (2 * depth) ** 0.25
    T = B * L
    xt = x.reshape(T, D)
    new_k, new_v, new_pool, new_s = [], [], [], []
    pool_width = W["pool_scale"].shape[-1]
    da_width = D - pool_width
    n_da_heads = da_width // LANES
    n_hg_heads = D // LANES
    lnrow = lambda a, i, s: a[i, s].reshape(1, D)
    for i in range(depth):
        fuse_proj = i % 2 == 1 and L % HGRN_PROJ_ROWS == 0
        xt = _ffn_ln(xt, W["wg"], W["wu"], W["wd"], lnrow(W["ln_g"], i, 0), lnrow(W["ln_b"], i, 0), i, 0, alpha,
                     also_bf16=fuse_proj)
        if fuse_proj:
            xt, xt16 = xt
        if i % 2 == 0:
            e = i // 2
            u, k, v, qkv = _in_even(xt, W["w_in_even"], e, B, L, pool_width, da_width, (LANES // 2) ** -0.5)
            past = 0 if k_cache is None else k_cache.shape[2]
            pool_out, nh = _pool(u.reshape(B, L, pool_width), pool_hist[e], W["pool_w"][e],
                                 W["pool_scale"][e].reshape(1, pool_width), past)
            bias, lam = attn_prep[e]
            lam_init = 0.8 - 0.6 * math.exp(-0.3 * i)
            g = W["diff_norm_g"][e].reshape(1, LANES)
            qkv3 = qkv.reshape(B, L, 3 * da_width)
            if k_cache is None:
                o = _attn_prompt(qkv3, bias, lam, g, n_da_heads, 1.0 - lam_init)
            else:
                kc = jnp.transpose(k_cache[e], (0, 2, 1, 3))
                vc = jnp.transpose(v_cache[e], (0, 2, 1, 3))
                o = _attn_sample(qkv3, kc, vc, bias, lam, g, n_da_heads, 1.0 - lam_init)
            parts = [pool_out.reshape(T, pool_width), o.reshape(T, da_width)]
            w_out = W["w_out_even"]
            new_k.append(jnp.transpose(k, (0, 2, 1, 3)))
            new_v.append(jnp.transpose(v, (0, 2, 1, 3)))
            new_pool.append(nh)
            li = e
        else:
            od = i // 2
            ng = W["hgrn_norm_g"][od].reshape(1, LANES)
            if fuse_proj:
                o, s = _hgrn_proj(xt16.reshape(B, L, D), W["w_in_odd_heads"][od], W["lb_logits"], ng,
                                  hg_state[od], i, rows=HGRN_PROJ_ROWS)
            else:
                h = _matmul(xt, W["w_in_odd"], od)
                o, s = _hgrn(h.reshape(B, L, 4 * D), W["lb_logits"], ng, hg_state[od], i, n_hg_heads)
            parts = [o.reshape(T, D)]
            w_out = W["w_out_odd"]
            new_s.append(s)
            li = od
        xt = _out_ln(parts, xt, w_out, li, lnrow(W["ln_g"], i, 1), lnrow(W["ln_b"], i, 1), alpha)
        xt = _ffn_ln(xt, W["wg"], W["wu"], W["wd"], lnrow(W["ln_g"], i, 2), lnrow(W["ln_b"], i, 2), i, 1, alpha)
        xt = _ple(xt, p.reshape(depth, T, -1), W["w_ple_gate"], W["w_ple_up"], i)
    return xt.reshape(B, L, D), jnp.stack(new_k), jnp.stack(new_v), jnp.stack(new_pool), jnp.stack(new_s)


ATTN_TILE = 256
HGRN_PROJ_ROWS = 512


def kernel(x_prompt, x_sample, cache_diff_k, cache_diff_v, state_pool, state_hgrn, p_prompt, p_sample, ln_g, ln_b, w_ffn_gate, w_ffn_up, w_ffn_down, w_ple_gate, w_ple_up, w_in_even, w_out_even, pool_w, pool_scale, lam_q1, lam_k1, lam_q2, lam_k2, diff_norm_g, rel_bias, w_in_odd, w_out_odd, hgrn_norm_g, hgrn_lb_logits):
    bf = lambda a: a.astype(BF16)
    W = dict(
        ln_g=ln_g.astype(F32), ln_b=ln_b.astype(F32),
        wg=bf(w_ffn_gate), wu=bf(w_ffn_up), wd=bf(w_ffn_down),
        w_ple_gate=bf(w_ple_gate), w_ple_up=bf(w_ple_up),
        w_in_even=bf(w_in_even), w_out_even=bf(w_out_even),
        pool_w=bf(pool_w), pool_scale=pool_scale.astype(F32),
        diff_norm_g=diff_norm_g.astype(F32),
        w_in_odd=bf(w_in_odd), w_out_odd=bf(w_out_odd),
        hgrn_norm_g=hgrn_norm_g.astype(F32), lb_logits=hgrn_lb_logits.astype(F32),
    )
    n_even = w_in_even.shape[0]
    n_odd = w_in_odd.shape[0]
    d_model = w_in_odd.shape[1]
    n_hg = d_model // LANES
    W["w_in_odd_heads"] = jnp.transpose(W["w_in_odd"].reshape(n_odd, d_model, 4, n_hg, LANES),
                                        (0, 3, 1, 2, 4)).reshape(n_odd, n_hg, d_model, 4 * LANES)
    attn_prep = []
    for e in range(n_even):
        lam_init = 0.8 - 0.6 * math.exp(-0.3 * (2 * e))
        attn_prep.append(_attn_prep(rel_bias, lam_q1[e], lam_k1[e], lam_q2[e], lam_k2[e], ATTN_TILE, lam_init))

    B = x_prompt.shape[0]
    dt = x_prompt.dtype
    zero_pool = jnp.zeros((n_even, B) + state_pool.shape[2:], dt)
    zero_s = jnp.zeros((n_odd, B) + state_hgrn.shape[2:], dt)
    y_p, k_p, v_p, pool_p, s_p = _trunk(x_prompt, p_prompt, None, None, zero_pool, zero_s, W, attn_prep)
    y_s, k_s, v_s, pool_s, s_s = _trunk(x_sample, p_sample, cache_diff_k, cache_diff_v, state_pool, state_hgrn,
                                        W, attn_prep)
    return (y_p, y_s, k_p, v_p, k_s, v_s, pool_p, pool_s, s_p, s_s)
```

```python
import functools
import math

import jax
import jax.numpy as jnp
from jax import lax
from jax.experimental import pallas as pl
from jax.experimental.pallas import tpu as pltpu

F32 = jnp.float32
BF16 = jnp.bfloat16

CHUNK = 64
POOL_WINDOWS = (2, 4, 8, 16)
POOL_HIST = max(POOL_WINDOWS) - 1
N_BUCKETS = 32
MAX_DISTANCE = 128
EPS = 1e-5
NEG = -1e30
LANES = 128

VMEM_LIMIT = 56 * 1024 * 1024


def _cparams(*sem):
    return pltpu.CompilerParams(dimension_semantics=sem, vmem_limit_bytes=VMEM_LIMIT)


def _pick(n, pref):
    if n <= pref:
        return n
    t = pref
    while n % t:
        t //= 2
    return t


def _layer_norm(y, g, b):
    mu = jnp.mean(y, axis=-1, keepdims=True)
    d = y - mu
    var = jnp.mean(d * d, axis=-1, keepdims=True)
    return d * lax.rsqrt(var + EPS) * g + b


def _dot(a, b):
    return jnp.dot(a, b, preferred_element_type=F32)


def _dot_nt(a, b):
    return lax.dot_general(a, b, (((1,), (1,)), ((), ())), preferred_element_type=F32)


def _ffn_ln_body(x_ref, wg_ref, wu_ref, wd_ref, g_ref, b_ref, *refs, alpha):
    *o_refs, xb_ref, acc_ref = refs
    j = pl.program_id(1)

    @pl.when(j == 0)
    def _():
        xb_ref[...] = x_ref[...].astype(BF16)
        acc_ref[...] = jnp.zeros_like(acc_ref)

    xb = xb_ref[...]
    hg = _dot(xb, wg_ref[...])
    hu = _dot(xb, wu_ref[...])
    act = hg * jax.nn.sigmoid(hg) * hu
    acc_ref[...] += _dot(act.astype(BF16), wd_ref[...])

    @pl.when(j == pl.num_programs(1) - 1)
    def _():
        y = _layer_norm(alpha * x_ref[...] + 0.5 * acc_ref[...], g_ref[...], b_ref[...])
        for o_ref in o_refs:
            o_ref[...] = y.astype(o_ref.dtype)


def _ffn_ln(x, wg, wu, wd, g, b, li, si, alpha, also_bf16=False, tm=512, tf=512):
    T, D = x.shape
    F = wg.shape[-1]
    tm = _pick(T, tm)
    tf = _pick(F, tf)
    out_dtypes = (F32, BF16) if also_bf16 else (F32,)
    outs = pl.pallas_call(
        functools.partial(_ffn_ln_body, alpha=alpha),
        grid=(T // tm, F // tf),
        in_specs=[
            pl.BlockSpec((tm, D), lambda t, j: (t, 0)),
            pl.BlockSpec((None, None, D, tf), lambda t, j: (li, si, 0, j)),
            pl.BlockSpec((None, None, D, tf), lambda t, j: (li, si, 0, j)),
            pl.BlockSpec((None, None, tf, D), lambda t, j: (li, si, j, 0)),
            pl.BlockSpec((1, D), lambda t, j: (0, 0)),
            pl.BlockSpec((1, D), lambda t, j: (0, 0)),
        ],
        out_specs=[pl.BlockSpec((tm, D), lambda t, j: (t, 0)) for _ in out_dtypes],
        out_shape=[jax.ShapeDtypeStruct((T, D), dt) for dt in out_dtypes],
        scratch_shapes=[pltpu.VMEM((tm, D), BF16), pltpu.VMEM((tm, D), F32)],
        compiler_params=_cparams("parallel", "arbitrary"),
        name="ffn_ln",
    )(x, wg, wu, wd, g, b)
    return tuple(outs) if also_bf16 else outs[0]


def _matmul_body(x_ref, w_ref, o_ref, xb_ref):
    @pl.when(pl.program_id(1) == 0)
    def _():
        xb_ref[...] = x_ref[...].astype(BF16)

    o_ref[...] = _dot(xb_ref[...], w_ref[...]).astype(o_ref.dtype)


def _matmul(x, w, li, out_dtype=F32, tm=1024, tn=1024):
    T, K = x.shape
    N = w.shape[-1]
    tm = _pick(T, tm)
    tn = _pick(N, tn)
    return pl.pallas_call(
        _matmul_body,
        grid=(T // tm, N // tn),
        in_specs=[
            pl.BlockSpec((tm, K), lambda t, j: (t, 0)),
            pl.BlockSpec((None, K, tn), lambda t, j: (li, 0, j)),
        ],
        out_specs=pl.BlockSpec((tm, tn), lambda t, j: (t, j)),
        out_shape=jax.ShapeDtypeStruct((T, N), out_dtype),
        scratch_shapes=[pltpu.VMEM((tm, K), BF16)],
        compiler_params=_cparams("parallel", "arbitrary"),
        name="in_proj",
    )(x, w)


def _proj_body(x_ref, w_ref, *o_refs, scale):
    r = _dot(x_ref[...], w_ref[...])
    for o_ref in o_refs:
        if len(o_ref.shape) == 4:
            bb, hpt, tl, _ = o_ref.shape
            for hh in range(hpt):
                o_ref[:, hh] = r[:, hh * LANES:(hh + 1) * LANES].reshape(bb, tl, LANES)
        elif o_ref.dtype == BF16:
            o_ref[...] = (r * scale).astype(BF16)
        else:
            o_ref[...] = r


def _proj(x16, w, li, col0, width, B, L, kinds, scale=1.0, tm=1024, tn=512):
    T, K = x16.shape
    tm = _pick(T, tm)
    assert width % tn == 0 and col0 % tn == 0 and tn % LANES == 0
    hpt = tn // LANES
    tl = min(L, tm)
    assert tm % tl == 0 and L % tl == 0
    bb, nl = tm // tl, L // tl
    specs, shapes = [], []
    for kind in kinds:
        if kind == "heads":
            specs.append(pl.BlockSpec((bb, hpt, tl, LANES), lambda t, j: (t // nl, j, t % nl, 0)))
            shapes.append(jax.ShapeDtypeStruct((B, width // LANES, L, LANES), F32))
        else:
            specs.append(pl.BlockSpec((tm, tn), lambda t, j: (t, j)))
            shapes.append(jax.ShapeDtypeStruct((T, width), BF16 if kind == "bf16" else F32))
    return pl.pallas_call(
        functools.partial(_proj_body, scale=scale),
        grid=(T // tm, width // tn),
        in_specs=[
            pl.BlockSpec((tm, K), lambda t, j: (t, 0)),
            pl.BlockSpec((None, K, tn), lambda t, j: (li, 0, col0 // tn + j)),
        ],
        out_specs=specs,
        out_shape=shapes,
        compiler_params=_cparams("parallel", "arbitrary"),
        name="in_proj_cols",
    )(x16, w)


def _out_ln_body(*refs, alpha, widths):
    n = len(widths)
    parts = refs[:n]
    x_ref, w_ref, g_ref, b_ref, o_ref = refs[n:]
    tm = x_ref.shape[0]
    n_split = 2 if tm % 16 == 0 else 1
    hm = tm // n_split
    for h in range(n_split):
        rows = slice(h * hm, (h + 1) * hm)
        acc = alpha * x_ref[rows, :]
        off = 0
        for p_ref, wd in zip(parts, widths):
            acc = acc + _dot(p_ref[rows, :], w_ref[off:off + wd, :])
            off += wd
        o_ref[rows, :] = _layer_norm(acc, g_ref[...], b_ref[...])


def _out_ln(parts, x, w, li, g, b, alpha, tm=512):
    T, D = x.shape
    tm = _pick(T, tm)
    widths = tuple(p.shape[1] for p in parts)
    kin = sum(widths)
    return pl.pallas_call(
        functools.partial(_out_ln_body, alpha=alpha, widths=widths),
        grid=(T // tm,),
        in_specs=[pl.BlockSpec((tm, wd), lambda t: (t, 0)) for wd in widths] + [
            pl.BlockSpec((tm, D), lambda t: (t, 0)),
            pl.BlockSpec((None, kin, D), lambda t: (li, 0, 0)),
            pl.BlockSpec((1, D), lambda t: (0, 0)),
            pl.BlockSpec((1, D), lambda t: (0, 0)),
        ],
        out_specs=pl.BlockSpec((tm, D), lambda t: (t, 0)),
        out_shape=jax.ShapeDtypeStruct((T, D), F32),
        compiler_params=_cparams("parallel"),
        name="out_proj_ln",
    )(*parts, x, w, g, b)


def _ple_body(x_ref, p_ref, wg_ref, wu_ref, o_ref, *, tn):
    xb = x_ref[...].astype(BF16)
    pb = p_ref[...].astype(BF16)
    D = o_ref.shape[1]
    for c in range(D // tn):
        sl = slice(c * tn, (c + 1) * tn)
        gate = jax.nn.sigmoid(_dot(xb, wg_ref[:, sl]))
        up = _dot(pb, wu_ref[:, sl])
        o_ref[:, sl] = x_ref[:, sl] + gate * up


def _ple(x, p, wg, wu, li, tm=512, tn=512):
    T, D = x.shape
    P = p.shape[2]
    tm = _pick(T, tm)
    return pl.pallas_call(
        functools.partial(_ple_body, tn=_pick(D, tn)),
        grid=(T // tm,),
        in_specs=[
            pl.BlockSpec((tm, D), lambda t: (t, 0)),
            pl.BlockSpec((None, tm, P), lambda t: (li, t, 0)),
            pl.BlockSpec((None, D, D), lambda t: (li, 0, 0)),
            pl.BlockSpec((None, P, D), lambda t: (li, 0, 0)),
        ],
        out_specs=pl.BlockSpec((tm, D), lambda t: (t, 0)),
        out_shape=jax.ShapeDtypeStruct((T, D), F32),
        compiler_params=_cparams("parallel"),
        name="ple_gate",
    )(x, p, wg, wu)


def _pool_body(u_ref, hist_ref, w_ref, sc_ref, o_ref, nh_ref, ext_ref, *, tl, start_pos):
    l = pl.program_id(1)
    nl = pl.num_programs(1)
    H = POOL_HIST + 1

    @pl.when(l == 0)
    def _():
        ext_ref[0:1, :] = jnp.zeros((1, ext_ref.shape[1]), F32)
        ext_ref[1:H, :] = hist_ref[...]

    @pl.when(l > 0)
    def _():
        ext_ref[0:H, :] = ext_ref[tl:tl + H, :]

    ext_ref[H:H + tl, :] = u_ref[...]

    pos = start_pos + l * tl + lax.broadcasted_iota(jnp.int32, (tl, 1), 0)
    gd = LANES
    for g, wnd in enumerate(POOL_WINDOWS):
        cs = slice(g * gd, (g + 1) * gd)
        s = ext_ref[H:H + tl, cs]
        cur = s
        for d in range(1, wnd):
            s = s + ext_ref[H - d:H - d + tl, cs]
        cnt = jnp.minimum(pos + 1, wnd).astype(F32)
        pooled = s / cnt - cur
        y = _dot(pooled.astype(BF16), w_ref[g]) * sc_ref[:, cs]
        o_ref[:, cs] = y.astype(o_ref.dtype)

    @pl.when(l == nl - 1)
    def _():
        nh_ref[...] = ext_ref[tl + 1:tl + H, :]


def _pool(u, hist, w, scale, start_pos, tl=512):
    B, L, PW = u.shape
    tl = _pick(L, tl)
    assert tl >= POOL_HIST + 1
    return pl.pallas_call(
        functools.partial(_pool_body, tl=tl, start_pos=start_pos),
        grid=(B, L // tl),
        in_specs=[
            pl.BlockSpec((None, tl, PW), lambda b, l: (b, l, 0)),
            pl.BlockSpec((None, POOL_HIST, PW), lambda b, l: (b, 0, 0)),
            pl.BlockSpec(w.shape, lambda b, l: (0, 0, 0)),
            pl.BlockSpec((1, PW), lambda b, l: (0, 0)),
        ],
        out_specs=[
            pl.BlockSpec((None, tl, PW), lambda b, l: (b, l, 0)),
            pl.BlockSpec((None, POOL_HIST, PW), lambda b, l: (b, 0, 0)),
        ],
        out_shape=[
            jax.ShapeDtypeStruct((B, L, PW), BF16),
            jax.ShapeDtypeStruct((B, POOL_HIST, PW), F32),
        ],
        scratch_shapes=[pltpu.VMEM((tl + POOL_HIST + 1, PW), F32)],
        compiler_params=_cparams("parallel", "arbitrary"),
        name="pool_mixer",
    )(u, hist, w, scale)


def _t5_bucket(rel):
    nb = N_BUCKETS // 2
    max_exact = nb // 2
    n = jnp.abs(rel)
    nf = jnp.maximum(n, 1).astype(jnp.float32)
    large = max_exact + (jnp.log(nf / max_exact) / math.log(MAX_DISTANCE / max_exact)
                         * (nb - max_exact)).astype(jnp.int32)
    large = jnp.minimum(large, nb - 1)
    return jnp.where(rel > 0, nb, 0) + jnp.where(n < max_exact, n, large)


def _attn_prep_body(tbl_ref, bkt_ref, lq1_ref, lk1_ref, lq2_ref, lk2_ref, bias_ref, lam_ref, *, tq, lam_init,
                    far_bucket):
    h = pl.program_id(0)
    bkt = bkt_ref[...]
    far = tbl_ref[far_bucket, h]
    acc = jnp.zeros(bkt.shape, F32)
    for b in range(N_BUCKETS):
        acc = jnp.where(bkt == b, tbl_ref[b, h] - far, acc)
    r = lax.broadcasted_iota(jnp.int32, bkt.shape, 1)
    c = lax.broadcasted_iota(jnp.int32, bkt.shape, 2)
    t = lax.broadcasted_iota(jnp.int32, bkt.shape, 0)
    visible = (t == 0) | ((c // CHUNK) <= (r // CHUNK))
    bias_ref[...] = jnp.where(visible, acc, NEG)
    e1 = jnp.exp(jnp.sum(lq1_ref[...] * lk1_ref[...], axis=-1, keepdims=True))
    e2 = jnp.exp(jnp.sum(lq2_ref[...] * lk2_ref[...], axis=-1, keepdims=True))
    lam_ref[...] = jnp.broadcast_to(e1 - e2 + lam_init, lam_ref.shape)


def _attn_prep(rel_bias, lq1, lk1, lq2, lk2, tq, lam_init):
    nbk, H = rel_bias.shape
    r = jnp.arange(tq, dtype=jnp.int32)[:, None]
    c = jnp.arange(tq, dtype=jnp.int32)[None, :]
    bkt = jnp.stack([_t5_bucket(c - r - tq), _t5_bucket(c - r)])
    far_bucket = N_BUCKETS // 2 - 1
    assert tq >= MAX_DISTANCE
    row = lambda a: a.reshape(1, -1).astype(F32)
    return pl.pallas_call(
        functools.partial(_attn_prep_body, tq=tq, lam_init=lam_init, far_bucket=far_bucket),
        grid=(H,),
        in_specs=[
            pl.BlockSpec(memory_space=pltpu.SMEM),
            pl.BlockSpec((2, tq, tq), lambda h: (0, 0, 0)),
            pl.BlockSpec((1, lq1.shape[-1]), lambda h: (0, 0)),
            pl.BlockSpec((1, lq1.shape[-1]), lambda h: (0, 0)),
            pl.BlockSpec((1, lq1.shape[-1]), lambda h: (0, 0)),
            pl.BlockSpec((1, lq1.shape[-1]), lambda h: (0, 0)),
        ],
        out_specs=[
            pl.BlockSpec((None, 2, tq, tq), lambda h: (h, 0, 0, 0)),
            pl.BlockSpec((8, LANES), lambda h: (0, 0)),
        ],
        out_shape=[
            jax.ShapeDtypeStruct((H, 2, tq, tq), F32),
            jax.ShapeDtypeStruct((8, LANES), F32),
        ],
        compiler_params=_cparams("arbitrary"),
        name="attn_prep",
    )(rel_bias.astype(F32), bkt, row(lq1), row(lk1), row(lq2), row(lk2))


def _split_q(q):
    lane = lax.broadcasted_iota(jnp.int32, q.shape, 1)
    half = q.shape[1] // 2
    zero = jnp.zeros_like(q)
    return jnp.concatenate([jnp.where(lane < half, q, zero), jnp.where(lane >= half, q, zero)], axis=0)


def _attend(qq, spans, s_ref, mx_ref, l_ref, acc_ref):
    rows = qq.shape[0]
    mx_ref[...] = jnp.full(mx_ref.shape, NEG, F32)
    for col, get_k, _, bias in spans:
        s = _dot_nt(qq, get_k())
        w = s.shape[1]
        if bias is not None:
            s = (s.reshape(2, rows // 2, w) + bias[None]).reshape(rows, w)
        s_ref[:, col:col + w] = s
        if w % LANES == 0:
            m = functools.reduce(jnp.maximum, [s[:, c:c + LANES] for c in range(0, w, LANES)])
            mx_ref[...] = jnp.maximum(mx_ref[...], m)
        else:
            mx_ref[:, 0:w] = jnp.maximum(mx_ref[:, 0:w], s)
    m_b = jnp.broadcast_to(jnp.max(mx_ref[...], axis=1, keepdims=True), mx_ref.shape)
    mx_ref[...] = m_b
    l_ref[...] = jnp.zeros_like(l_ref)
    acc_ref[...] = jnp.zeros_like(acc_ref)
    for col, get_k, get_v, _ in spans:
        w = get_v().shape[0]
        s = s_ref[:, col:col + w]
        m_b = mx_ref[...]
        if w % LANES == 0:
            ps = [jnp.exp(s[:, c:c + LANES] - m_b) for c in range(0, w, LANES)]
            l_ref[...] += functools.reduce(jnp.add, ps)
            p = ps[0] if len(ps) == 1 else jnp.concatenate(ps, axis=1)
        else:
            p = jnp.exp(s - m_b[:, 0:w])
            l_ref[:, 0:w] += p
        acc_ref[...] += _dot(p.astype(BF16), get_v())


def _attn_finish(lam_ref, g_ref, l_ref, acc_ref, tq, out_scale):
    lam = lam_ref[0:1, 0:1]
    o = acc_ref[...] / jnp.sum(l_ref[...], axis=1, keepdims=True)
    o = o[:tq] - lam * o[tq:]
    return o * lax.rsqrt(jnp.mean(o * o, axis=-1, keepdims=True) + EPS) * g_ref[...] * out_scale


ATTN_KEY_SPAN = 512
ATTN_SAMPLE_HEADS = 2


def _attn_prompt_body(q_ref, k_ref, v_ref, bias_ref, lam_ref, g_ref, o_ref, s_ref, mx_ref, l_ref, acc_ref, *,
                      tq, out_scale):
    L = q_ref.shape[0]

    def span(st, w, bias):
        return (st, lambda: k_ref[st:st + w, :], lambda: v_ref[st:st + w, :], bias)

    for qi in reversed(range(L // tq)):
        par = qi % 2
        far_end = max(qi - 1, 0) * tq
        spans = [span(st, min(ATTN_KEY_SPAN, far_end - st), None) for st in range(0, far_end, ATTN_KEY_SPAN)]
        if qi >= 1:
            spans.append(span((qi - 1) * tq, tq, bias_ref[0]))
        spans.append(span(qi * tq, tq, bias_ref[1]))
        qq = _split_q(q_ref[qi * tq:(qi + 1) * tq, :])
        _attend(qq, spans, s_ref.at[par], mx_ref.at[par], l_ref.at[par], acc_ref.at[par])
        y = _attn_finish(lam_ref, g_ref, l_ref.at[par], acc_ref.at[par], tq, out_scale)
        o_ref[qi * tq:(qi + 1) * tq, :] = y.astype(o_ref.dtype)


def _attn_prompt(q, k, v, bias, lam, g, out_scale):
    B, L, _ = q.shape
    H = q.shape[2] // LANES
    tq = bias.shape[-1]
    assert L % tq == 0 and tq % CHUNK == 0
    return pl.pallas_call(
        functools.partial(_attn_prompt_body, tq=tq, out_scale=out_scale),
        grid=(B, H),
        in_specs=[
            pl.BlockSpec((None, L, LANES), lambda b, h: (b, 0, h)),
            pl.BlockSpec((None, L, LANES), lambda b, h: (b, 0, h)),
            pl.BlockSpec((None, L, LANES), lambda b, h: (b, 0, h)),
            pl.BlockSpec((None, 2, tq, tq), lambda b, h: (h, 0, 0, 0)),
            pl.BlockSpec((8, LANES), lambda b, h: (0, 0)),
            pl.BlockSpec((1, LANES), lambda b, h: (0, 0)),
        ],
        out_specs=pl.BlockSpec((None, L, LANES), lambda b, h: (b, 0, h)),
        out_shape=jax.ShapeDtypeStruct((B, L, H * LANES), BF16),
        scratch_shapes=[
            pltpu.VMEM((2, 2 * tq, L), F32),
            pltpu.VMEM((2, 2 * tq, LANES), F32),
            pltpu.VMEM((2, 2 * tq, LANES), F32),
            pltpu.VMEM((2, 2 * tq, LANES), F32),
        ],
        compiler_params=_cparams("parallel", "parallel"),
        name="diff_attn_prompt",
    )(q, k, v, bias, lam, g)


def _attn_sample_body(q_ref, kn_ref, vn_ref, kc_ref, vc_ref, bprev_ref, bdiag_ref, lam_ref, g_ref, o_ref,
                      s_ref, mx_ref, l_ref, acc_ref, *, lq, tk, out_scale):
    hp, P = kc_ref.shape[0], kc_ref.shape[1]
    near = P - tk
    for i in range(hp):
        cs = slice(i * LANES, (i + 1) * LANES)

        def span(st, w, bias, i=i):
            return (st, lambda: kc_ref[i, st:st + w, :].astype(BF16),
                    lambda: vc_ref[i, st:st + w, :].astype(BF16), bias)

        spans = [span(st, min(ATTN_KEY_SPAN, near - st), None) for st in range(0, near, ATTN_KEY_SPAN)]
        spans.append(span(near, tk, bprev_ref[i]))
        spans.append((P, lambda cs=cs: kn_ref[:, cs], lambda cs=cs: vn_ref[:, cs], bdiag_ref[i, :, 0:lq]))
        _attend(_split_q(q_ref[:, cs]), spans, s_ref.at[i], mx_ref.at[i], l_ref.at[i], acc_ref.at[i])
        y = _attn_finish(lam_ref, g_ref, l_ref.at[i], acc_ref.at[i], lq, out_scale)
        o_ref[:, cs] = y.astype(o_ref.dtype)


def _attn_sample(q, k, v, k_cache, v_cache, bias, lam, g, out_scale):
    B, lq, _ = q.shape
    H = q.shape[2] // LANES
    P = k_cache.shape[2]
    tk = bias.shape[-1]
    assert lq == CHUNK and P % tk == 0 and P % CHUNK == 0 and lq <= tk
    hp = ATTN_SAMPLE_HEADS if H % ATTN_SAMPLE_HEADS == 0 else 1
    ng = H // hp
    return pl.pallas_call(
        functools.partial(_attn_sample_body, lq=lq, tk=tk, out_scale=out_scale),
        grid=(B, ng),
        in_specs=[
            pl.BlockSpec((None, lq, hp * LANES), lambda b, g: (b, 0, g)),
            pl.BlockSpec((None, lq, hp * LANES), lambda b, g: (b, 0, g)),
            pl.BlockSpec((None, lq, hp * LANES), lambda b, g: (b, 0, g)),
            pl.BlockSpec((None, hp, P, LANES), lambda b, g: (b, g, 0, 0)),
            pl.BlockSpec((None, hp, P, LANES), lambda b, g: (b, g, 0, 0)),
            pl.BlockSpec((hp, None, lq, tk), lambda b, g: (g, 0, 0, 0)),
            pl.BlockSpec((hp, None, lq, tk), lambda b, g: (g, 1, 0, 0)),
            pl.BlockSpec((8, LANES), lambda b, g: (0, 0)),
            pl.BlockSpec((1, LANES), lambda b, g: (0, 0)),
        ],
        out_specs=pl.BlockSpec((None, lq, hp * LANES), lambda b, g: (b, 0, g)),
        out_shape=jax.ShapeDtypeStruct((B, lq, H * LANES), BF16),
        scratch_shapes=[
            pltpu.VMEM((hp, 2 * lq, P + LANES), F32),
            pltpu.VMEM((hp, 2 * lq, LANES), F32),
            pltpu.VMEM((hp, 2 * lq, LANES), F32),
            pltpu.VMEM((hp, 2 * lq, LANES), F32),
        ],
        compiler_params=_cparams("parallel", "arbitrary"),
        name="diff_attn_sample",
    )(q, k, v, k_cache, v_cache, bias, bias, lam, g)


HGRN_HEADS_PER_STEP = 4
MIN_LOG2 = -150.0


def _hgrn_setup(lb_ref, lvl_ref, ck, layer):
    n_lev = ck.bit_length() - 1
    t_i = lax.broadcasted_iota(jnp.int32, (ck, ck), 0)
    s_i = lax.broadcasted_iota(jnp.int32, (ck, ck), 1)
    x = t_i ^ s_i
    hb = jnp.zeros((ck, ck), jnp.int32)
    for b in range(1, n_lev):
        hb = hb + (x >= (1 << b)).astype(jnp.int32)
    lvl_ref[...] = jnp.where(t_i > s_i, hb, -1)

    lg = lb_ref[...]
    e = jnp.exp(lg - jnp.max(lg, axis=0, keepdims=True))
    p = e / jnp.sum(e, axis=0, keepdims=True)
    cum = p[0:1]
    for d in range(1, layer + 1):
        cum = cum + p[d:d + 1]
    lb = cum - p[0:1]
    return lb, 1.0 - lb, lax.broadcasted_iota(jnp.int32, (ck, LANES), 0)


def _hgrn_chunk(hq, z, v, hg, consts, lvl_ref, ng_ref, st_ref):
    lb, one_mlb, row = consts
    ck = hq.shape[0]
    n_lev = ck.bit_length() - 1
    q = hq * jax.nn.sigmoid(hq)
    r = 1.0 / (1.0 + jnp.exp(z))
    k = one_mlb * r
    f = lb + one_mlb * (1.0 - r)
    g = jnp.maximum(jnp.log2(f), MIN_LOG2)
    vb = v.astype(BF16)

    pf = g
    tot = g
    lvl = lvl_ref[...]
    a = jnp.zeros((ck, ck), F32)
    for lev in range(n_lev):
        hsz = 1 << lev
        ql = (q * jnp.exp2(pf)).astype(BF16)
        kl = (k if lev == 0 else k * jnp.exp2(tot - pf)).astype(BF16)
        a = jnp.where(lvl == lev, _dot_nt(ql, kl), a)
        if hsz % 8 == 0:
            nb = ck // (2 * hsz)
            t4 = tot.reshape(nb, 2, hsz, LANES)
            p4 = pf.reshape(nb, 2, hsz, LANES)
            both = t4[:, 0:1] + t4[:, 1:2]
            tot = jnp.concatenate([both, both], axis=1).reshape(ck, LANES)
            pf = jnp.concatenate([p4[:, 0:1], p4[:, 1:2] + t4[:, 0:1]], axis=1).reshape(ck, LANES)
        else:
            second = (row & hsz) != 0
            up = pltpu.roll(tot, hsz, 0)
            dn = pltpu.roll(tot, ck - hsz, 0)
            pf = pf + jnp.where(second, up, 0.0)
            tot = tot + jnp.where(second, up, dn)
    st = st_ref[...]
    o = _dot(a.astype(BF16), vb) + jnp.sum(q * k, axis=1, keepdims=True) * v
    o = o + _dot_nt((q * jnp.exp2(pf)).astype(BF16), st.astype(BF16))
    kst = (k * jnp.exp2(tot - pf)).astype(BF16)
    st_ref[...] = st * jnp.exp2(tot[0:1, :]) + _dot(v.T.astype(BF16), kst)
    y = o * lax.rsqrt(jnp.mean(o * o, axis=-1, keepdims=True) + EPS) * ng_ref[...]
    return y * (hg * jax.nn.sigmoid(hg))


def _hgrn_body(hq_ref, hz_ref, hi_ref, hg_ref, lb_ref, ng_ref, s0_ref, o_ref, s_ref, st_ref, lvl_ref, *, ck,
               layer):
    l = pl.program_id(2)
    tl = hq_ref.shape[0]
    hp = st_ref.shape[0]

    @pl.when(l == 0)
    def _():
        for i in range(hp):
            st_ref[i] = s0_ref[i].T

    lb, one_mlb, row = _hgrn_setup(lb_ref, lvl_ref, ck, layer)

    def chunk(c, carry):
        rs = slice(c * ck, (c + 1) * ck) if isinstance(c, int) else pl.ds(pl.multiple_of(c * ck, ck), ck)
        for i in range(hp):
            cs = slice(i * LANES, (i + 1) * LANES)
            y = _hgrn_chunk(hq_ref[rs, cs], hz_ref[rs, cs], hi_ref[rs, cs], hg_ref[rs, cs],
                            (lb[:, cs], one_mlb[:, cs], row), lvl_ref, ng_ref, st_ref.at[i])
            o_ref[rs, cs] = y.astype(o_ref.dtype)
        return carry

    n_chunks = tl // ck
    if n_chunks == 1:
        chunk(0, 0)
    else:
        lax.fori_loop(0, n_chunks, chunk, 0, unroll=4 if n_chunks % 4 == 0 else 1)

    @pl.when(l == pl.num_programs(2) - 1)
    def _():
        for i in range(hp):
            s_ref[i] = st_ref[i].T


def _hgrn(h, lb_logits, norm_g, s0, layer, n_heads, tl=2048, ck=128):
    B, L, _ = h.shape
    H = n_heads
    tl = _pick(L, tl)
    ck = _pick(tl, ck)
    assert ck & (ck - 1) == 0 and ck >= 8
    depth = lb_logits.shape[0]
    hp = HGRN_HEADS_PER_STEP if (L <= LANES and H % HGRN_HEADS_PER_STEP == 0) else 1
    ng = H // hp
    blk = lambda part: pl.BlockSpec((None, tl, hp * LANES), lambda b, g, l: (b, l, part * ng + g))
    return pl.pallas_call(
        functools.partial(_hgrn_body, ck=ck, layer=layer),
        grid=(B, ng, L // tl),
        in_specs=[
            blk(0), blk(1), blk(2), blk(3),
            pl.BlockSpec((depth, hp * LANES), lambda b, g, l: (0, g)),
            pl.BlockSpec((1, LANES), lambda b, g, l: (0, 0)),
            pl.BlockSpec((None, hp, LANES, LANES), lambda b, g, l: (b, g, 0, 0)),
        ],
        out_specs=[
            pl.BlockSpec((None, tl, hp * LANES), lambda b, g, l: (b, l, g)),
            pl.BlockSpec((None, hp, LANES, LANES), lambda b, g, l: (b, g, 0, 0)),
        ],
        out_shape=[
            jax.ShapeDtypeStruct((B, L, H * LANES), BF16),
            jax.ShapeDtypeStruct((B, H, LANES, LANES), F32),
        ],
        scratch_shapes=[pltpu.VMEM((hp, LANES, LANES), F32), pltpu.VMEM((ck, ck), jnp.int32)],
        compiler_params=_cparams("parallel", "parallel", "arbitrary"),
        name="hgrn_scan",
    )(h, h, h, h, lb_logits, norm_g, s0)


def _hgrn_proj_body(x_ref, w_ref, lb_ref, ng_ref, s0_ref, o_ref, s_ref, st_ref, lvl_ref, h_ref, *, ck, layer,
                    rows):
    L = x_ref.shape[0]
    st_ref[...] = s0_ref[...].T
    consts = _hgrn_setup(lb_ref, lvl_ref, ck, layer)

    def project(s):
        h_ref[s % 2] = _dot(x_ref[s * rows:(s + 1) * rows, :], w_ref[...])

    project(0)
    for s in range(L // rows):
        if (s + 1) * rows < L:
            project(s + 1)
        h = h_ref.at[s % 2]
        for c in range(rows // ck):
            rs = slice(c * ck, (c + 1) * ck)
            y = _hgrn_chunk(h[rs, 0:LANES], h[rs, LANES:2 * LANES], h[rs, 2 * LANES:3 * LANES],
                            h[rs, 3 * LANES:4 * LANES], consts, lvl_ref, ng_ref, st_ref)
            o_ref[s * rows + c * ck:s * rows + (c + 1) * ck, :] = y.astype(o_ref.dtype)
    s_ref[...] = st_ref[...].T


def _hgrn_proj(x, w_heads, lb_logits, norm_g, s0, layer, ck=128, rows=512):
    B, L, D = x.shape
    H = w_heads.shape[0]
    assert L % rows == 0 and rows % ck == 0 and ck & (ck - 1) == 0
    depth = lb_logits.shape[0]
    return pl.pallas_call(
        functools.partial(_hgrn_proj_body, ck=ck, layer=layer, rows=rows),
        grid=(B, H),
        in_specs=[
            pl.BlockSpec((None, L, D), lambda b, hd: (b, 0, 0)),
            pl.BlockSpec((None, D, 4 * LANES), lambda b, hd: (hd, 0, 0)),
            pl.BlockSpec((depth, LANES), lambda b, hd: (0, hd)),
            pl.BlockSpec((1, LANES), lambda b, hd: (0, 0)),
            pl.BlockSpec((None, None, LANES, LANES), lambda b, hd: (b, hd, 0, 0)),
        ],
        out_specs=[
            pl.BlockSpec((None, L, LANES), lambda b, hd: (b, 0, hd)),
            pl.BlockSpec((None, None, LANES, LANES), lambda b, hd: (b, hd, 0, 0)),
        ],
        out_shape=[
            jax.ShapeDtypeStruct((B, L, H * LANES), BF16),
            jax.ShapeDtypeStruct((B, H, LANES, LANES), F32),
        ],
        scratch_shapes=[
            pltpu.VMEM((LANES, LANES), F32),
            pltpu.VMEM((ck, ck), jnp.int32),
            pltpu.VMEM((2, rows, 4 * LANES), F32),
        ],
        compiler_params=_cparams("parallel", "arbitrary"),
        name="hgrn_proj_scan",
    )(x, w_heads, lb_logits, norm_g, s0)


def _trunk(x, p, k_cache, v_cache, pool_hist, hg_state, W, attn_prep):
    B, L, D = x.shape
    depth = W["ln_g"].shape[0]
    alpha = (2 * depth) ** 0.25
    T = B * L
    xt = x.reshape(T, D)
    new_k, new_v, new_pool, new_s = [], [], [], []
    pool_width = W["pool_scale"].shape[-1]
    da_width = D - pool_width
    n_da_heads = da_width // LANES
    n_hg_heads = D // LANES
    lnrow = lambda a, i, s: a[i, s].reshape(1, D)
    for i in range(depth):
        fuse_proj = i % 2 == 1 and L % HGRN_PROJ_ROWS == 0
        want16 = fuse_proj or i % 2 == 0
        xt = _ffn_ln(xt, W["wg"], W["wu"], W["wd"], lnrow(W["ln_g"], i, 0), lnrow(W["ln_b"], i, 0), i, 0, alpha,
                     also_bf16=want16)
        if want16:
            xt, xt16 = xt
        if i % 2 == 0:
            e = i // 2
            w_in = W["w_in_even"]
            (u,) = _proj(xt16, w_in, e, 0, pool_width, B, L, ("f32",))
            (q16,) = _proj(xt16, w_in, e, pool_width, da_width, B, L, ("bf16",), scale=(LANES // 2) ** -0.5)
            k, k16 = _proj(xt16, w_in, e, pool_width + da_width, da_width, B, L, ("heads", "bf16"))
            v, v16 = _proj(xt16, w_in, e, pool_width + 2 * da_width, da_width, B, L, ("heads", "bf16"))
            past = 0 if k_cache is None else k_cache.shape[2]
            pool_out, nh = _pool(u.reshape(B, L, pool_width), pool_hist[e], W["pool_w"][e],
                                 W["pool_scale"][e].reshape(1, pool_width), past)
            bias, lam = attn_prep[e]
            lam_init = 0.8 - 0.6 * math.exp(-0.3 * i)
            g = W["diff_norm_g"][e].reshape(1, LANES)
            q3, k3, v3 = (a.reshape(B, L, da_width) for a in (q16, k16, v16))
            if k_cache is None:
                o = _attn_prompt(q3, k3, v3, bias, lam, g, 1.0 - lam_init)
            else:
                kc = jnp.transpose(k_cache[e], (0, 2, 1, 3))
                vc = jnp.transpose(v_cache[e], (0, 2, 1, 3))
                o = _attn_sample(q3, k3, v3, kc, vc, bias, lam, g, 1.0 - lam_init)
            parts = [pool_out.reshape(T, pool_width), o.reshape(T, da_width)]
            w_out = W["w_out_even"]
            new_k.append(jnp.transpose(k, (0, 2, 1, 3)))
            new_v.append(jnp.transpose(v, (0, 2, 1, 3)))
            new_pool.append(nh)
            li = e
        else:
            od = i // 2
            ng = W["hgrn_norm_g"][od].reshape(1, LANES)
            if fuse_proj:
                o, s = _hgrn_proj(xt16.reshape(B, L, D), W["w_in_odd_heads"][od], W["lb_logits"], ng,
                                  hg_state[od], i, rows=HGRN_PROJ_ROWS)
            else:
                h = _matmul(xt, W["w_in_odd"], od)
                o, s = _hgrn(h.reshape(B, L, 4 * D), W["lb_logits"], ng, hg_state[od], i, n_hg_heads)
            parts = [o.reshape(T, D)]
            w_out = W["w_out_odd"]
            new_s.append(s)
            li = od
        xt = _out_ln(parts, xt, w_out, li, lnrow(W["ln_g"], i, 1), lnrow(W["ln_b"], i, 1), alpha)
        xt = _ffn_ln(xt, W["wg"], W["wu"], W["wd"], lnrow(W["ln_g"], i, 2), lnrow(W["ln_b"], i, 2), i, 1, alpha)
        xt = _ple(xt, p.reshape(depth, T, -1), W["w_ple_gate"], W["w_ple_up"], i)
    return xt.reshape(B, L, D), jnp.stack(new_k), jnp.stack(new_v), jnp.stack(new_pool), jnp.stack(new_s)


ATTN_TILE = 256
HGRN_PROJ_ROWS = 512


def kernel(x_prompt, x_sample, cache_diff_k, cache_diff_v, state_pool, state_hgrn, p_prompt, p_sample, ln_g, ln_b, w_ffn_gate, w_ffn_up, w_ffn_down, w_ple_gate, w_ple_up, w_in_even, w_out_even, pool_w, pool_scale, lam_q1, lam_k1, lam_q2, lam_k2, diff_norm_g, rel_bias, w_in_odd, w_out_odd, hgrn_norm_g, hgrn_lb_logits):
    bf = lambda a: a.astype(BF16)
    W = dict(
        ln_g=ln_g.astype(F32), ln_b=ln_b.astype(F32),
        wg=bf(w_ffn_gate), wu=bf(w_ffn_up), wd=bf(w_ffn_down),
        w_ple_gate=bf(w_ple_gate), w_ple_up=bf(w_ple_up),
        w_in_even=bf(w_in_even), w_out_even=bf(w_out_even),
        pool_w=bf(pool_w), pool_scale=pool_scale.astype(F32),
        diff_norm_g=diff_norm_g.astype(F32),
        w_in_odd=bf(w_in_odd), w_out_odd=bf(w_out_odd),
        hgrn_norm_g=hgrn_norm_g.astype(F32), lb_logits=hgrn_lb_logits.astype(F32),
    )
    n_even = w_in_even.shape[0]
    n_odd = w_in_odd.shape[0]
    d_model = w_in_odd.shape[1]
    n_hg = d_model // LANES
    W["w_in_odd_heads"] = jnp.transpose(W["w_in_odd"].reshape(n_odd, d_model, 4, n_hg, LANES),
                                        (0, 3, 1, 2, 4)).reshape(n_odd, n_hg, d_model, 4 * LANES)
    attn_prep = []
    for e in range(n_even):
        lam_init = 0.8 - 0.6 * math.exp(-0.3 * (2 * e))
        attn_prep.append(_attn_prep(rel_bias, lam_q1[e], lam_k1[e], lam_q2[e], lam_k2[e], ATTN_TILE, lam_init))

    B = x_prompt.shape[0]
    dt = x_prompt.dtype
    zero_pool = jnp.zeros((n_even, B) + state_pool.shape[2:], dt)
    zero_s = jnp.zeros((n_odd, B) + state_hgrn.shape[2:], dt)
    y_p, k_p, v_p, pool_p, s_p = _trunk(x_prompt, p_prompt, None, None, zero_pool, zero_s, W, attn_prep)
    y_s, k_s, v_s, pool_s, s_s = _trunk(x_sample, p_sample, cache_diff_k, cache_diff_v, state_pool, state_hgrn,
                                        W, attn_prep)
    return (y_p, y_s, k_p, v_p, k_s, v_s, pool_p, pool_s, s_p, s_s)
```

```python
import functools
import math

import jax
import jax.numpy as jnp
from jax import lax
from jax.experimental import pallas as pl
from jax.experimental.pallas import tpu as pltpu

F32 = jnp.float32
BF16 = jnp.bfloat16

CHUNK = 64
POOL_WINDOWS = (2, 4, 8, 16)
POOL_HIST = max(POOL_WINDOWS) - 1
N_BUCKETS = 32
MAX_DISTANCE = 128
EPS = 1e-5
NEG = -1e30
LANES = 128

VMEM_LIMIT = 56 * 1024 * 1024


def _cparams(*sem):
    return pltpu.CompilerParams(dimension_semantics=sem, vmem_limit_bytes=VMEM_LIMIT)


def _pick(n, pref):
    if n <= pref:
        return n
    t = pref
    while n % t:
        t //= 2
    return t


def _layer_norm(y, g, b):
    mu = jnp.mean(y, axis=-1, keepdims=True)
    d = y - mu
    var = jnp.mean(d * d, axis=-1, keepdims=True)
    return d * lax.rsqrt(var + EPS) * g + b


def _dot(a, b):
    return jnp.dot(a, b, preferred_element_type=F32)


def _dot_nt(a, b):
    return lax.dot_general(a, b, (((1,), (1,)), ((), ())), preferred_element_type=F32)


FFN_PAIR_CHUNK = 4


def _ffn_act_body(x_ref, wgu_ref, a_ref, xb_ref):
    @pl.when(pl.program_id(1) == 0)
    def _():
        xb_ref[...] = x_ref[...].astype(BF16)

    xb = xb_ref[...]
    n_pairs = a_ref.shape[1] // LANES
    for p0 in range(0, n_pairs, FFN_PAIR_CHUNK):
        p1 = min(p0 + FFN_PAIR_CHUNK, n_pairs)
        h = _dot(xb, wgu_ref[:, 2 * p0 * LANES:2 * p1 * LANES])
        for p in range(p1 - p0):
            hg = h[:, 2 * p * LANES:(2 * p + 1) * LANES]
            hu = h[:, (2 * p + 1) * LANES:(2 * p + 2) * LANES]
            a_ref[:, (p0 + p) * LANES:(p0 + p + 1) * LANES] = (hg * jax.nn.sigmoid(hg) * hu).astype(a_ref.dtype)


def _ffn_down_ln_body(a_ref, x_ref, wd_ref, g_ref, b_ref, *o_refs, alpha):
    tm = x_ref.shape[0]
    hm = tm // 2
    for rows in (slice(0, hm), slice(hm, tm)):
        y = alpha * x_ref[rows, :] + 0.5 * _dot(a_ref[rows, :], wd_ref[...])
        y = _layer_norm(y, g_ref[...], b_ref[...])
        for o_ref in o_refs:
            o_ref[rows, :] = y.astype(o_ref.dtype)


def _interleave_gate_up(wg, wu):
    *lead, D, F = wg.shape
    both = jnp.stack([wg.reshape(*lead, D, F // LANES, LANES), wu.reshape(*lead, D, F // LANES, LANES)], axis=-2)
    return both.reshape(*lead, D, 2 * F)


def _ffn_ln(x, wgu, wd, g, b, li, si, alpha, also_bf16=False, tm_act=1024, tf=1408, tm_down=256):
    T, D = x.shape
    F = wd.shape[-2]
    tm = _pick(T, tm_act)
    tf = tf if F % tf == 0 else _pick(F, 512)
    act = pl.pallas_call(
        _ffn_act_body,
        grid=(T // tm, F // tf),
        in_specs=[
            pl.BlockSpec((tm, D), lambda t, j: (t, 0)),
            pl.BlockSpec((None, None, D, 2 * tf), lambda t, j: (li, si, 0, j)),
        ],
        out_specs=pl.BlockSpec((tm, tf), lambda t, j: (t, j)),
        out_shape=jax.ShapeDtypeStruct((T, F), BF16),
        scratch_shapes=[pltpu.VMEM((tm, D), BF16)],
        compiler_params=_cparams("parallel", "arbitrary"),
        name="ffn_act",
    )(x, wgu)

    tm = _pick(T, tm_down)
    out_dtypes = (F32, BF16) if also_bf16 else (F32,)
    outs = pl.pallas_call(
        functools.partial(_ffn_down_ln_body, alpha=alpha),
        grid=(T // tm,),
        in_specs=[
            pl.BlockSpec((tm, F), lambda t: (t, 0)),
            pl.BlockSpec((tm, D), lambda t: (t, 0)),
            pl.BlockSpec((None, None, F, D), lambda t: (li, si, 0, 0), pipeline_mode=pl.Buffered(1)),
            pl.BlockSpec((1, D), lambda t: (0, 0)),
            pl.BlockSpec((1, D), lambda t: (0, 0)),
        ],
        out_specs=[pl.BlockSpec((tm, D), lambda t: (t, 0)) for _ in out_dtypes],
        out_shape=[jax.ShapeDtypeStruct((T, D), dt) for dt in out_dtypes],
        compiler_params=_cparams("parallel"),
        name="ffn_down_ln",
    )(act, x, wd, g, b)
    return tuple(outs) if also_bf16 else outs[0]


def _matmul_body(x_ref, w_ref, o_ref, xb_ref):
    @pl.when(pl.program_id(1) == 0)
    def _():
        xb_ref[...] = x_ref[...].astype(BF16)

    o_ref[...] = _dot(xb_ref[...], w_ref[...]).astype(o_ref.dtype)


def _matmul(x, w, li, out_dtype=F32, tm=1024, tn=1024):
    T, K = x.shape
    N = w.shape[-1]
    tm = _pick(T, tm)
    tn = _pick(N, tn)
    return pl.pallas_call(
        _matmul_body,
        grid=(T // tm, N // tn),
        in_specs=[
            pl.BlockSpec((tm, K), lambda t, j: (t, 0)),
            pl.BlockSpec((None, K, tn), lambda t, j: (li, 0, j)),
        ],
        out_specs=pl.BlockSpec((tm, tn), lambda t, j: (t, j)),
        out_shape=jax.ShapeDtypeStruct((T, N), out_dtype),
        scratch_shapes=[pltpu.VMEM((tm, K), BF16)],
        compiler_params=_cparams("parallel", "arbitrary"),
        name="in_proj",
    )(x, w)


def _proj_body(x_ref, w_ref, *o_refs, scale):
    r = _dot(x_ref[...], w_ref[...])
    for o_ref in o_refs:
        if len(o_ref.shape) == 4:
            bb, hpt, tl, _ = o_ref.shape
            for hh in range(hpt):
                o_ref[:, hh] = r[:, hh * LANES:(hh + 1) * LANES].reshape(bb, tl, LANES)
        elif o_ref.dtype == BF16:
            o_ref[...] = (r * scale).astype(BF16)
        else:
            o_ref[...] = r


def _proj(x16, w, li, col0, width, B, L, kinds, scale=1.0, tm=2048, tn=512):
    T, K = x16.shape
    tm = _pick(T, tm)
    assert width % tn == 0 and col0 % tn == 0 and tn % LANES == 0
    hpt = tn // LANES
    tl = min(L, tm)
    assert tm % tl == 0 and L % tl == 0
    bb, nl = tm // tl, L // tl
    specs, shapes = [], []
    for kind in kinds:
        if kind == "heads":
            specs.append(pl.BlockSpec((bb, hpt, tl, LANES), lambda t, j: (t // nl, j, t % nl, 0)))
            shapes.append(jax.ShapeDtypeStruct((B, width // LANES, L, LANES), F32))
        else:
            specs.append(pl.BlockSpec((tm, tn), lambda t, j: (t, j)))
            shapes.append(jax.ShapeDtypeStruct((T, width), BF16 if kind == "bf16" else F32))
    return pl.pallas_call(
        functools.partial(_proj_body, scale=scale),
        grid=(T // tm, width // tn),
        in_specs=[
            pl.BlockSpec((tm, K), lambda t, j: (t, 0)),
            pl.BlockSpec((None, K, tn), lambda t, j: (li, 0, col0 // tn + j)),
        ],
        out_specs=specs,
        out_shape=shapes,
        compiler_params=_cparams("parallel", "arbitrary"),
        name="in_proj_cols",
    )(x16, w)


def _out_ln_body(*refs, alpha, widths):
    n = len(widths)
    parts = refs[:n]
    x_ref, w_ref, g_ref, b_ref, o_ref = refs[n:]
    tm = x_ref.shape[0]
    n_split = 2 if tm % 16 == 0 else 1
    hm = tm // n_split
    for h in range(n_split):
        rows = slice(h * hm, (h + 1) * hm)
        acc = alpha * x_ref[rows, :]
        off = 0
        for p_ref, wd in zip(parts, widths):
            acc = acc + _dot(p_ref[rows, :], w_ref[off:off + wd, :])
            off += wd
        o_ref[rows, :] = _layer_norm(acc, g_ref[...], b_ref[...])


def _out_ln(parts, x, w, li, g, b, alpha, tm=512):
    T, D = x.shape
    tm = _pick(T, tm)
    widths = tuple(p.shape[1] for p in parts)
    kin = sum(widths)
    return pl.pallas_call(
        functools.partial(_out_ln_body, alpha=alpha, widths=widths),
        grid=(T // tm,),
        in_specs=[pl.BlockSpec((tm, wd), lambda t: (t, 0)) for wd in widths] + [
            pl.BlockSpec((tm, D), lambda t: (t, 0)),
            pl.BlockSpec((None, kin, D), lambda t: (li, 0, 0)),
            pl.BlockSpec((1, D), lambda t: (0, 0)),
            pl.BlockSpec((1, D), lambda t: (0, 0)),
        ],
        out_specs=pl.BlockSpec((tm, D), lambda t: (t, 0)),
        out_shape=jax.ShapeDtypeStruct((T, D), F32),
        compiler_params=_cparams("parallel"),
        name="out_proj_ln",
    )(*parts, x, w, g, b)


def _ple_body(x_ref, p_ref, wg_ref, wu_ref, o_ref, *, tn):
    xb = x_ref[...].astype(BF16)
    pb = p_ref[...].astype(BF16)
    D = o_ref.shape[1]
    for c in range(D // tn):
        sl = slice(c * tn, (c + 1) * tn)
        gate = jax.nn.sigmoid(_dot(xb, wg_ref[:, sl]))
        up = _dot(pb, wu_ref[:, sl])
        o_ref[:, sl] = x_ref[:, sl] + gate * up


def _ple(x, p, wg, wu, li, tm=512, tn=512):
    T, D = x.shape
    P = p.shape[2]
    tm = _pick(T, tm)
    return pl.pallas_call(
        functools.partial(_ple_body, tn=_pick(D, tn)),
        grid=(T // tm,),
        in_specs=[
            pl.BlockSpec((tm, D), lambda t: (t, 0)),
            pl.BlockSpec((None, tm, P), lambda t: (li, t, 0)),
            pl.BlockSpec((None, D, D), lambda t: (li, 0, 0)),
            pl.BlockSpec((None, P, D), lambda t: (li, 0, 0)),
        ],
        out_specs=pl.BlockSpec((tm, D), lambda t: (t, 0)),
        out_shape=jax.ShapeDtypeStruct((T, D), F32),
        compiler_params=_cparams("parallel"),
        name="ple_gate",
    )(x, p, wg, wu)


def _pool_body(u_ref, hist_ref, w_ref, sc_ref, o_ref, nh_ref, ext_ref, *, tl, start_pos):
    l = pl.program_id(1)
    nl = pl.num_programs(1)
    H = POOL_HIST + 1

    @pl.when(l == 0)
    def _():
        ext_ref[0:1, :] = jnp.zeros((1, ext_ref.shape[1]), F32)
        ext_ref[1:H, :] = hist_ref[...]

    @pl.when(l > 0)
    def _():
        ext_ref[0:H, :] = ext_ref[tl:tl + H, :]

    ext_ref[H:H + tl, :] = u_ref[...]

    pos = start_pos + l * tl + lax.broadcasted_iota(jnp.int32, (tl, 1), 0)
    gd = LANES
    for g, wnd in enumerate(POOL_WINDOWS):
        cs = slice(g * gd, (g + 1) * gd)
        s = ext_ref[H:H + tl, cs]
        cur = s
        for d in range(1, wnd):
            s = s + ext_ref[H - d:H - d + tl, cs]
        cnt = jnp.minimum(pos + 1, wnd).astype(F32)
        pooled = s / cnt - cur
        y = _dot(pooled.astype(BF16), w_ref[g]) * sc_ref[:, cs]
        o_ref[:, cs] = y.astype(o_ref.dtype)

    @pl.when(l == nl - 1)
    def _():
        nh_ref[...] = ext_ref[tl + 1:tl + H, :]


def _pool(u, hist, w, scale, start_pos, tl=512):
    B, L, PW = u.shape
    tl = _pick(L, tl)
    assert tl >= POOL_HIST + 1
    return pl.pallas_call(
        functools.partial(_pool_body, tl=tl, start_pos=start_pos),
        grid=(B, L // tl),
        in_specs=[
            pl.BlockSpec((None, tl, PW), lambda b, l: (b, l, 0)),
            pl.BlockSpec((None, POOL_HIST, PW), lambda b, l: (b, 0, 0)),
            pl.BlockSpec(w.shape, lambda b, l: (0, 0, 0)),
            pl.BlockSpec((1, PW), lambda b, l: (0, 0)),
        ],
        out_specs=[
            pl.BlockSpec((None, tl, PW), lambda b, l: (b, l, 0)),
            pl.BlockSpec((None, POOL_HIST, PW), lambda b, l: (b, 0, 0)),
        ],
        out_shape=[
            jax.ShapeDtypeStruct((B, L, PW), BF16),
            jax.ShapeDtypeStruct((B, POOL_HIST, PW), F32),
        ],
        scratch_shapes=[pltpu.VMEM((tl + POOL_HIST + 1, PW), F32)],
        compiler_params=_cparams("parallel", "arbitrary"),
        name="pool_mixer",
    )(u, hist, w, scale)


def _t5_bucket(rel):
    nb = N_BUCKETS // 2
    max_exact = nb // 2
    n = jnp.abs(rel)
    nf = jnp.maximum(n, 1).astype(jnp.float32)
    large = max_exact + (jnp.log(nf / max_exact) / math.log(MAX_DISTANCE / max_exact)
                         * (nb - max_exact)).astype(jnp.int32)
    large = jnp.minimum(large, nb - 1)
    return jnp.where(rel > 0, nb, 0) + jnp.where(n < max_exact, n, large)


def _attn_prep_body(tbl_ref, bkt_ref, lq1_ref, lk1_ref, lq2_ref, lk2_ref, bias_ref, lam_ref, *, tq, lam_init,
                    far_bucket):
    h = pl.program_id(0)
    bkt = bkt_ref[...]
    far = tbl_ref[far_bucket, h]
    acc = jnp.zeros(bkt.shape, F32)
    for b in range(N_BUCKETS):
        acc = jnp.where(bkt == b, tbl_ref[b, h] - far, acc)
    r = lax.broadcasted_iota(jnp.int32, bkt.shape, 1)
    c = lax.broadcasted_iota(jnp.int32, bkt.shape, 2)
    t = lax.broadcasted_iota(jnp.int32, bkt.shape, 0)
    visible = (t == 0) | ((c // CHUNK) <= (r // CHUNK))
    bias_ref[...] = jnp.where(visible, acc, NEG)
    e1 = jnp.exp(jnp.sum(lq1_ref[...] * lk1_ref[...], axis=-1, keepdims=True))
    e2 = jnp.exp(jnp.sum(lq2_ref[...] * lk2_ref[...], axis=-1, keepdims=True))
    lam_ref[...] = jnp.broadcast_to(e1 - e2 + lam_init, lam_ref.shape)


def _attn_prep(rel_bias, lq1, lk1, lq2, lk2, tq, lam_init):
    nbk, H = rel_bias.shape
    r = jnp.arange(tq, dtype=jnp.int32)[:, None]
    c = jnp.arange(tq, dtype=jnp.int32)[None, :]
    bkt = jnp.stack([_t5_bucket(c - r - tq), _t5_bucket(c - r)])
    far_bucket = N_BUCKETS // 2 - 1
    assert tq >= MAX_DISTANCE
    row = lambda a: a.reshape(1, -1).astype(F32)
    return pl.pallas_call(
        functools.partial(_attn_prep_body, tq=tq, lam_init=lam_init, far_bucket=far_bucket),
        grid=(H,),
        in_specs=[
            pl.BlockSpec(memory_space=pltpu.SMEM),
            pl.BlockSpec((2, tq, tq), lambda h: (0, 0, 0)),
            pl.BlockSpec((1, lq1.shape[-1]), lambda h: (0, 0)),
            pl.BlockSpec((1, lq1.shape[-1]), lambda h: (0, 0)),
            pl.BlockSpec((1, lq1.shape[-1]), lambda h: (0, 0)),
            pl.BlockSpec((1, lq1.shape[-1]), lambda h: (0, 0)),
        ],
        out_specs=[
            pl.BlockSpec((None, 2, tq, tq), lambda h: (h, 0, 0, 0)),
            pl.BlockSpec((8, LANES), lambda h: (0, 0)),
        ],
        out_shape=[
            jax.ShapeDtypeStruct((H, 2, tq, tq), F32),
            jax.ShapeDtypeStruct((8, LANES), F32),
        ],
        compiler_params=_cparams("arbitrary"),
        name="attn_prep",
    )(rel_bias.astype(F32), bkt, row(lq1), row(lk1), row(lq2), row(lk2))


def _split_q(q):
    lane = lax.broadcasted_iota(jnp.int32, q.shape, 1)
    half = q.shape[1] // 2
    zero = jnp.zeros_like(q)
    return jnp.concatenate([jnp.where(lane < half, q, zero), jnp.where(lane >= half, q, zero)], axis=0)


def _attend(qq, spans, s_ref, mx_ref, l_ref, acc_ref):
    rows = qq.shape[0]
    mx_ref[...] = jnp.full(mx_ref.shape, NEG, F32)
    for col, get_k, _, bias in spans:
        s = _dot_nt(qq, get_k())
        w = s.shape[1]
        if bias is not None:
            s = (s.reshape(2, rows // 2, w) + bias[None]).reshape(rows, w)
        s_ref[:, col:col + w] = s
        if w % LANES == 0:
            m = functools.reduce(jnp.maximum, [s[:, c:c + LANES] for c in range(0, w, LANES)])
            mx_ref[...] = jnp.maximum(mx_ref[...], m)
        else:
            mx_ref[:, 0:w] = jnp.maximum(mx_ref[:, 0:w], s)
    m_b = jnp.broadcast_to(jnp.max(mx_ref[...], axis=1, keepdims=True), mx_ref.shape)
    mx_ref[...] = m_b
    l_ref[...] = jnp.zeros_like(l_ref)
    acc_ref[...] = jnp.zeros_like(acc_ref)
    for col, get_k, get_v, _ in spans:
        w = get_v().shape[0]
        s = s_ref[:, col:col + w]
        m_b = mx_ref[...]
        if w % LANES == 0:
            ps = [jnp.exp(s[:, c:c + LANES] - m_b) for c in range(0, w, LANES)]
            l_ref[...] += functools.reduce(jnp.add, ps)
            p = ps[0] if len(ps) == 1 else jnp.concatenate(ps, axis=1)
        else:
            p = jnp.exp(s - m_b[:, 0:w])
            l_ref[:, 0:w] += p
        acc_ref[...] += _dot(p.astype(BF16), get_v())


def _attn_finish(lam_ref, g_ref, l_ref, acc_ref, tq, out_scale):
    lam = lam_ref[0:1, 0:1]
    o = acc_ref[...] / jnp.sum(l_ref[...], axis=1, keepdims=True)
    o = o[:tq] - lam * o[tq:]
    return o * lax.rsqrt(jnp.mean(o * o, axis=-1, keepdims=True) + EPS) * g_ref[...] * out_scale


ATTN_KEY_SPAN = 512
ATTN_SAMPLE_HEADS = 2


def _attn_prompt_body(q_ref, k_ref, v_ref, bias_ref, lam_ref, g_ref, o_ref, s_ref, mx_ref, l_ref, acc_ref, *,
                      tq, out_scale):
    L = q_ref.shape[0]

    def span(st, w, bias):
        return (st, lambda: k_ref[st:st + w, :], lambda: v_ref[st:st + w, :], bias)

    for qi in reversed(range(L // tq)):
        par = qi % 2
        far_end = max(qi - 1, 0) * tq
        spans = [span(st, min(ATTN_KEY_SPAN, far_end - st), None) for st in range(0, far_end, ATTN_KEY_SPAN)]
        if qi >= 1:
            spans.append(span((qi - 1) * tq, tq, bias_ref[0]))
        spans.append(span(qi * tq, tq, bias_ref[1]))
        qq = _split_q(q_ref[qi * tq:(qi + 1) * tq, :])
        _attend(qq, spans, s_ref.at[par], mx_ref.at[par], l_ref.at[par], acc_ref.at[par])
        y = _attn_finish(lam_ref, g_ref, l_ref.at[par], acc_ref.at[par], tq, out_scale)
        o_ref[qi * tq:(qi + 1) * tq, :] = y.astype(o_ref.dtype)


def _attn_prompt(q, k, v, bias, lam, g, out_scale):
    B, L, _ = q.shape
    H = q.shape[2] // LANES
    tq = bias.shape[-1]
    assert L % tq == 0 and tq % CHUNK == 0
    return pl.pallas_call(
        functools.partial(_attn_prompt_body, tq=tq, out_scale=out_scale),
        grid=(B, H),
        in_specs=[
            pl.BlockSpec((None, L, LANES), lambda b, h: (b, 0, h)),
            pl.BlockSpec((None, L, LANES), lambda b, h: (b, 0, h)),
            pl.BlockSpec((None, L, LANES), lambda b, h: (b, 0, h)),
            pl.BlockSpec((None, 2, tq, tq), lambda b, h: (h, 0, 0, 0)),
            pl.BlockSpec((8, LANES), lambda b, h: (0, 0)),
            pl.BlockSpec((1, LANES), lambda b, h: (0, 0)),
        ],
        out_specs=pl.BlockSpec((None, L, LANES), lambda b, h: (b, 0, h)),
        out_shape=jax.ShapeDtypeStruct((B, L, H * LANES), BF16),
        scratch_shapes=[
            pltpu.VMEM((2, 2 * tq, L), F32),
            pltpu.VMEM((2, 2 * tq, LANES), F32),
            pltpu.VMEM((2, 2 * tq, LANES), F32),
            pltpu.VMEM((2, 2 * tq, LANES), F32),
        ],
        compiler_params=_cparams("parallel", "parallel"),
        name="diff_attn_prompt",
    )(q, k, v, bias, lam, g)


def _attn_sample_body(q_ref, kn_ref, vn_ref, kc_ref, vc_ref, bprev_ref, bdiag_ref, lam_ref, g_ref, o_ref,
                      s_ref, mx_ref, l_ref, acc_ref, *, lq, tk, out_scale):
    hp, P = kc_ref.shape[0], kc_ref.shape[1]
    near = P - tk
    for i in range(hp):
        cs = slice(i * LANES, (i + 1) * LANES)

        def span(st, w, bias, i=i):
            return (st, lambda: kc_ref[i, st:st + w, :].astype(BF16),
                    lambda: vc_ref[i, st:st + w, :].astype(BF16), bias)

        spans = [span(st, min(ATTN_KEY_SPAN, near - st), None) for st in range(0, near, ATTN_KEY_SPAN)]
        spans.append(span(near, tk, bprev_ref[i]))
        spans.append((P, lambda cs=cs: kn_ref[:, cs], lambda cs=cs: vn_ref[:, cs], bdiag_ref[i, :, 0:lq]))
        _attend(_split_q(q_ref[:, cs]), spans, s_ref.at[i], mx_ref.at[i], l_ref.at[i], acc_ref.at[i])
        y = _attn_finish(lam_ref, g_ref, l_ref.at[i], acc_ref.at[i], lq, out_scale)
        o_ref[:, cs] = y.astype(o_ref.dtype)


def _attn_sample(q, k, v, k_cache, v_cache, bias, lam, g, out_scale):
    B, lq, _ = q.shape
    H = q.shape[2] // LANES
    P = k_cache.shape[2]
    tk = bias.shape[-1]
    assert lq == CHUNK and P % tk == 0 and P % CHUNK == 0 and lq <= tk
    hp = ATTN_SAMPLE_HEADS if H % ATTN_SAMPLE_HEADS == 0 else 1
    ng = H // hp
    return pl.pallas_call(
        functools.partial(_attn_sample_body, lq=lq, tk=tk, out_scale=out_scale),
        grid=(B, ng),
        in_specs=[
            pl.BlockSpec((None, lq, hp * LANES), lambda b, g: (b, 0, g)),
            pl.BlockSpec((None, lq, hp * LANES), lambda b, g: (b, 0, g)),
            pl.BlockSpec((None, lq, hp * LANES), lambda b, g: (b, 0, g)),
            pl.BlockSpec((None, hp, P, LANES), lambda b, g: (b, g, 0, 0)),
            pl.BlockSpec((None, hp, P, LANES), lambda b, g: (b, g, 0, 0)),
            pl.BlockSpec((hp, None, lq, tk), lambda b, g: (g, 0, 0, 0)),
            pl.BlockSpec((hp, None, lq, tk), lambda b, g: (g, 1, 0, 0)),
            pl.BlockSpec((8, LANES), lambda b, g: (0, 0)),
            pl.BlockSpec((1, LANES), lambda b, g: (0, 0)),
        ],
        out_specs=pl.BlockSpec((None, lq, hp * LANES), lambda b, g: (b, 0, g)),
        out_shape=jax.ShapeDtypeStruct((B, lq, H * LANES), BF16),
        scratch_shapes=[
            pltpu.VMEM((hp, 2 * lq, P + LANES), F32),
            pltpu.VMEM((hp, 2 * lq, LANES), F32),
            pltpu.VMEM((hp, 2 * lq, LANES), F32),
            pltpu.VMEM((hp, 2 * lq, LANES), F32),
        ],
        compiler_params=_cparams("parallel", "arbitrary"),
        name="diff_attn_sample",
    )(q, k, v, k_cache, v_cache, bias, bias, lam, g)


HGRN_PROJ_HEADS = 1
HGRN_HEADS_PER_STEP = 4
MIN_LOG2 = -150.0


def _hgrn_setup(lb_ref, lvl_ref, ck, layer):
    n_lev = ck.bit_length() - 1
    t_i = lax.broadcasted_iota(jnp.int32, (ck, ck), 0)
    s_i = lax.broadcasted_iota(jnp.int32, (ck, ck), 1)
    x = t_i ^ s_i
    hb = jnp.zeros((ck, ck), jnp.int32)
    for b in range(1, n_lev):
        hb = hb + (x >= (1 << b)).astype(jnp.int32)
    lvl_ref[...] = jnp.where(t_i > s_i, hb, -1)

    lg = lb_ref[...]
    e = jnp.exp(lg - jnp.max(lg, axis=0, keepdims=True))
    p = e / jnp.sum(e, axis=0, keepdims=True)
    cum = p[0:1]
    for d in range(1, layer + 1):
        cum = cum + p[d:d + 1]
    lb = cum - p[0:1]
    return lb, 1.0 - lb, lax.broadcasted_iota(jnp.int32, (ck, LANES), 0)


def _hgrn_chunk(hq, z, v, hg, consts, lvl_ref, ng_ref, st_ref):
    lb, one_mlb, row = consts
    ck = hq.shape[0]
    n_lev = ck.bit_length() - 1
    q = hq * jax.nn.sigmoid(hq)
    r = 1.0 / (1.0 + jnp.exp(z))
    k = one_mlb * r
    f = lb + one_mlb * (1.0 - r)
    g = jnp.maximum(jnp.log2(f), MIN_LOG2)
    vb = v.astype(BF16)

    pf = g
    tot = g
    lvl = lvl_ref[...]
    a = jnp.zeros((ck, ck), F32)
    for lev in range(n_lev):
        hsz = 1 << lev
        ql = (q * jnp.exp2(pf)).astype(BF16)
        kl = (k if lev == 0 else k * jnp.exp2(tot - pf)).astype(BF16)
        a = jnp.where(lvl == lev, _dot_nt(ql, kl), a)
        if hsz % 8 == 0:
            nb = ck // (2 * hsz)
            t4 = tot.reshape(nb, 2, hsz, LANES)
            p4 = pf.reshape(nb, 2, hsz, LANES)
            both = t4[:, 0:1] + t4[:, 1:2]
            tot = jnp.concatenate([both, both], axis=1).reshape(ck, LANES)
            pf = jnp.concatenate([p4[:, 0:1], p4[:, 1:2] + t4[:, 0:1]], axis=1).reshape(ck, LANES)
        else:
            second = (row & hsz) != 0
            up = pltpu.roll(tot, hsz, 0)
            dn = pltpu.roll(tot, ck - hsz, 0)
            pf = pf + jnp.where(second, up, 0.0)
            tot = tot + jnp.where(second, up, dn)
    st = st_ref[...]
    o = _dot(a.astype(BF16), vb) + jnp.sum(q * k, axis=1, keepdims=True) * v
    o = o + _dot_nt((q * jnp.exp2(pf)).astype(BF16), st.astype(BF16))
    kst = (k * jnp.exp2(tot - pf)).astype(BF16)
    st_ref[...] = st * jnp.exp2(tot[0:1, :]) + _dot(v.T.astype(BF16), kst)
    y = o * lax.rsqrt(jnp.mean(o * o, axis=-1, keepdims=True) + EPS) * ng_ref[...]
    return y * (hg * jax.nn.sigmoid(hg))


def _hgrn_body(hq_ref, hz_ref, hi_ref, hg_ref, lb_ref, ng_ref, s0_ref, o_ref, s_ref, st_ref, lvl_ref, *, ck,
               layer):
    l = pl.program_id(2)
    tl = hq_ref.shape[0]
    hp = st_ref.shape[0]

    @pl.when(l == 0)
    def _():
        for i in range(hp):
            st_ref[i] = s0_ref[i].T

    lb, one_mlb, row = _hgrn_setup(lb_ref, lvl_ref, ck, layer)

    def chunk(c, carry):
        rs = slice(c * ck, (c + 1) * ck) if isinstance(c, int) else pl.ds(pl.multiple_of(c * ck, ck), ck)
        for i in range(hp):
            cs = slice(i * LANES, (i + 1) * LANES)
            y = _hgrn_chunk(hq_ref[rs, cs], hz_ref[rs, cs], hi_ref[rs, cs], hg_ref[rs, cs],
                            (lb[:, cs], one_mlb[:, cs], row), lvl_ref, ng_ref, st_ref.at[i])
            o_ref[rs, cs] = y.astype(o_ref.dtype)
        return carry

    n_chunks = tl // ck
    if n_chunks == 1:
        chunk(0, 0)
    else:
        lax.fori_loop(0, n_chunks, chunk, 0, unroll=4 if n_chunks % 4 == 0 else 1)

    @pl.when(l == pl.num_programs(2) - 1)
    def _():
        for i in range(hp):
            s_ref[i] = st_ref[i].T


def _hgrn(h, lb_logits, norm_g, s0, layer, n_heads, tl=2048, ck=128):
    B, L, _ = h.shape
    H = n_heads
    tl = _pick(L, tl)
    ck = _pick(tl, ck)
    assert ck & (ck - 1) == 0 and ck >= 8
    depth = lb_logits.shape[0]
    hp = HGRN_HEADS_PER_STEP if (L <= LANES and H % HGRN_HEADS_PER_STEP == 0) else 1
    ng = H // hp
    blk = lambda part: pl.BlockSpec((None, tl, hp * LANES), lambda b, g, l: (b, l, part * ng + g))
    return pl.pallas_call(
        functools.partial(_hgrn_body, ck=ck, layer=layer),
        grid=(B, ng, L // tl),
        in_specs=[
            blk(0), blk(1), blk(2), blk(3),
            pl.BlockSpec((depth, hp * LANES), lambda b, g, l: (0, g)),
            pl.BlockSpec((1, LANES), lambda b, g, l: (0, 0)),
            pl.BlockSpec((None, hp, LANES, LANES), lambda b, g, l: (b, g, 0, 0)),
        ],
        out_specs=[
            pl.BlockSpec((None, tl, hp * LANES), lambda b, g, l: (b, l, g)),
            pl.BlockSpec((None, hp, LANES, LANES), lambda b, g, l: (b, g, 0, 0)),
        ],
        out_shape=[
            jax.ShapeDtypeStruct((B, L, H * LANES), BF16),
            jax.ShapeDtypeStruct((B, H, LANES, LANES), F32),
        ],
        scratch_shapes=[pltpu.VMEM((hp, LANES, LANES), F32), pltpu.VMEM((ck, ck), jnp.int32)],
        compiler_params=_cparams("parallel", "parallel", "arbitrary"),
        name="hgrn_scan",
    )(h, h, h, h, lb_logits, norm_g, s0)


def _hgrn_proj_body(x_ref, w_ref, lb_ref, ng_ref, s0_ref, o_ref, s_ref, st_ref, lvl_ref, h_ref, *, ck, layer,
                    rows):
    L = x_ref.shape[0]
    hp = w_ref.shape[0]
    lb, one_mlb, row = _hgrn_setup(lb_ref, lvl_ref, ck, layer)
    for i in range(hp):
        cs = slice(i * LANES, (i + 1) * LANES)
        consts = (lb[:, cs], one_mlb[:, cs], row)
        st = st_ref.at[i]
        st[...] = s0_ref[i].T

        def project(s, i=i):
            h_ref[i, s % 2] = _dot(x_ref[s * rows:(s + 1) * rows, :], w_ref[i])

        project(0)
        for s in range(L // rows):
            if (s + 1) * rows < L:
                project(s + 1)
            h = h_ref.at[i, s % 2]
            for c in range(rows // ck):
                rs = slice(c * ck, (c + 1) * ck)
                y = _hgrn_chunk(h[rs, 0:LANES], h[rs, LANES:2 * LANES], h[rs, 2 * LANES:3 * LANES],
                                h[rs, 3 * LANES:4 * LANES], consts, lvl_ref, ng_ref, st)
                o_ref[s * rows + c * ck:s * rows + (c + 1) * ck, cs] = y.astype(o_ref.dtype)
        s_ref[i] = st[...].T


def _hgrn_proj(x, w_heads, lb_logits, norm_g, s0, layer, ck=128, rows=512):
    B, L, D = x.shape
    H = w_heads.shape[0]
    assert L % rows == 0 and rows % ck == 0 and ck & (ck - 1) == 0
    depth = lb_logits.shape[0]
    hp = HGRN_PROJ_HEADS if H % HGRN_PROJ_HEADS == 0 else 1
    return pl.pallas_call(
        functools.partial(_hgrn_proj_body, ck=ck, layer=layer, rows=rows),
        grid=(B, H // hp),
        in_specs=[
            pl.BlockSpec((None, L, D), lambda b, g: (b, 0, 0)),
            pl.BlockSpec((hp, D, 4 * LANES), lambda b, g: (g, 0, 0)),
            pl.BlockSpec((depth, hp * LANES), lambda b, g: (0, g)),
            pl.BlockSpec((1, LANES), lambda b, g: (0, 0)),
            pl.BlockSpec((None, hp, LANES, LANES), lambda b, g: (b, g, 0, 0)),
        ],
        out_specs=[
            pl.BlockSpec((None, L, hp * LANES), lambda b, g: (b, 0, g)),
            pl.BlockSpec((None, hp, LANES, LANES), lambda b, g: (b, g, 0, 0)),
        ],
        out_shape=[
            jax.ShapeDtypeStruct((B, L, H * LANES), BF16),
            jax.ShapeDtypeStruct((B, H, LANES, LANES), F32),
        ],
        scratch_shapes=[
            pltpu.VMEM((hp, LANES, LANES), F32),
            pltpu.VMEM((ck, ck), jnp.int32),
            pltpu.VMEM((hp, 2, rows, 4 * LANES), F32),
        ],
        compiler_params=_cparams("parallel", "arbitrary"),
        name="hgrn_proj_scan",
    )(x, w_heads, lb_logits, norm_g, s0)


def _trunk(x, p, k_cache, v_cache, pool_hist, hg_state, W, attn_prep):
    B, L, D = x.shape
    depth = W["ln_g"].shape[0]
    alpha = (2 * depth) ** 0.25
    T = B * L
    xt = x.reshape(T, D)
    new_k, new_v, new_pool, new_s = [], [], [], []
    pool_width = W["pool_scale"].shape[-1]
    da_width = D - pool_width
    n_da_heads = da_width // LANES
    n_hg_heads = D // LANES
    lnrow = lambda a, i, s: a[i, s].reshape(1, D)
    for i in range(depth):
        fuse_proj = i % 2 == 1 and L % HGRN_PROJ_ROWS == 0
        want16 = fuse_proj or i % 2 == 0
        xt = _ffn_ln(xt, W["wgu"], W["wd"], lnrow(W["ln_g"], i, 0), lnrow(W["ln_b"], i, 0), i, 0, alpha,
                     also_bf16=want16)
        if want16:
            xt, xt16 = xt
        if i % 2 == 0:
            e = i // 2
            w_in = W["w_in_even"]
            (u,) = _proj(xt16, w_in, e, 0, pool_width, B, L, ("f32",))
            (q16,) = _proj(xt16, w_in, e, pool_width, da_width, B, L, ("bf16",), scale=(LANES // 2) ** -0.5)
            k, k16 = _proj(xt16, w_in, e, pool_width + da_width, da_width, B, L, ("heads", "bf16"))
            v, v16 = _proj(xt16, w_in, e, pool_width + 2 * da_width, da_width, B, L, ("heads", "bf16"))
            past = 0 if k_cache is None else k_cache.shape[2]
            pool_out, nh = _pool(u.reshape(B, L, pool_width), pool_hist[e], W["pool_w"][e],
                                 W["pool_scale"][e].reshape(1, pool_width), past)
            bias, lam = attn_prep[e]
            lam_init = 0.8 - 0.6 * math.exp(-0.3 * i)
            g = W["diff_norm_g"][e].reshape(1, LANES)
            q3, k3, v3 = (a.reshape(B, L, da_width) for a in (q16, k16, v16))
            if k_cache is None:
                o = _attn_prompt(q3, k3, v3, bias, lam, g, 1.0 - lam_init)
            else:
                kc = jnp.transpose(k_cache[e], (0, 2, 1, 3))
                vc = jnp.transpose(v_cache[e], (0, 2, 1, 3))
                o = _attn_sample(q3, k3, v3, kc, vc, bias, lam, g, 1.0 - lam_init)
            parts = [pool_out.reshape(T, pool_width), o.reshape(T, da_width)]
            w_out = W["w_out_even"]
            new_k.append(jnp.transpose(k, (0, 2, 1, 3)))
            new_v.append(jnp.transpose(v, (0, 2, 1, 3)))
            new_pool.append(nh)
            li = e
        else:
            od = i // 2
            ng = W["hgrn_norm_g"][od].reshape(1, LANES)
            if fuse_proj:
                o, s = _hgrn_proj(xt16.reshape(B, L, D), W["w_in_odd_heads"][od], W["lb_logits"], ng,
                                  hg_state[od], i, rows=HGRN_PROJ_ROWS)
            else:
                h = _matmul(xt, W["w_in_odd"], od)
                o, s = _hgrn(h.reshape(B, L, 4 * D), W["lb_logits"], ng, hg_state[od], i, n_hg_heads)
            parts = [o.reshape(T, D)]
            w_out = W["w_out_odd"]
            new_s.append(s)
            li = od
        xt = _out_ln(parts, xt, w_out, li, lnrow(W["ln_g"], i, 1), lnrow(W["ln_b"], i, 1), alpha)
        xt = _ffn_ln(xt, W["wgu"], W["wd"], lnrow(W["ln_g"], i, 2), lnrow(W["ln_b"], i, 2), i, 1, alpha)
        xt = _ple(xt, p.reshape(depth, T, -1), W["w_ple_gate"], W["w_ple_up"], i)
    return xt.reshape(B, L, D), jnp.stack(new_k), jnp.stack(new_v), jnp.stack(new_pool), jnp.stack(new_s)


ATTN_TILE = 256
HGRN_PROJ_ROWS = 512


def kernel(x_prompt, x_sample, cache_diff_k, cache_diff_v, state_pool, state_hgrn, p_prompt, p_sample, ln_g, ln_b, w_ffn_gate, w_ffn_up, w_ffn_down, w_ple_gate, w_ple_up, w_in_even, w_out_even, pool_w, pool_scale, lam_q1, lam_k1, lam_q2, lam_k2, diff_norm_g, rel_bias, w_in_odd, w_out_odd, hgrn_norm_g, hgrn_lb_logits):
    bf = lambda a: a.astype(BF16)
    W = dict(
        ln_g=ln_g.astype(F32), ln_b=ln_b.astype(F32),
        wgu=_interleave_gate_up(bf(w_ffn_gate), bf(w_ffn_up)), wd=bf(w_ffn_down),
        w_ple_gate=bf(w_ple_gate), w_ple_up=bf(w_ple_up),
        w_in_even=bf(w_in_even), w_out_even=bf(w_out_even),
        pool_w=bf(pool_w), pool_scale=pool_scale.astype(F32),
        diff_norm_g=diff_norm_g.astype(F32),
        w_in_odd=bf(w_in_odd), w_out_odd=bf(w_out_odd),
        hgrn_norm_g=hgrn_norm_g.astype(F32), lb_logits=hgrn_lb_logits.astype(F32),
    )
    n_even = w_in_even.shape[0]
    n_odd = w_in_odd.shape[0]
    d_model = w_in_odd.shape[1]
    n_hg = d_model // LANES
    W["w_in_odd_heads"] = jnp.transpose(W["w_in_odd"].reshape(n_odd, d_model, 4, n_hg, LANES),
                                        (0, 3, 1, 2, 4)).reshape(n_odd, n_hg, d_model, 4 * LANES)
    attn_prep = []
    for e in range(n_even):
        lam_init = 0.8 - 0.6 * math.exp(-0.3 * (2 * e))
        attn_prep.append(_attn_prep(rel_bias, lam_q1[e], lam_k1[e], lam_q2[e], lam_k2[e], ATTN_TILE, lam_init))

    B = x_prompt.shape[0]
    dt = x_prompt.dtype
    zero_pool = jnp.zeros((n_even, B) + state_pool.shape[2:], dt)
    zero_s = jnp.zeros((n_odd, B) + state_hgrn.shape[2:], dt)
    y_p, k_p, v_p, pool_p, s_p = _trunk(x_prompt, p_prompt, None, None, zero_pool, zero_s, W, attn_prep)
    y_s, k_s, v_s, pool_s, s_s = _trunk(x_sample, p_sample, cache_diff_k, cache_diff_v, state_pool, state_hgrn,
                                        W, attn_prep)
    return (y_p, y_s, k_p, v_p, k_s, v_s, pool_p, pool_s, s_p, s_s)
```

```python
import functools
import math

import jax
import jax.numpy as jnp
from jax import lax
from jax.experimental import pallas as pl
from jax.experimental.pallas import tpu as pltpu

F32 = jnp.float32
BF16 = jnp.bfloat16

CHUNK = 64
POOL_WINDOWS = (2, 4, 8, 16)
POOL_HIST = max(POOL_WINDOWS) - 1
N_BUCKETS = 32
MAX_DISTANCE = 128
EPS = 1e-5
NEG = -1e30
LANES = 128

VMEM_LIMIT = 56 * 1024 * 1024


def _cparams(*sem):
    return pltpu.CompilerParams(dimension_semantics=sem, vmem_limit_bytes=VMEM_LIMIT)


def _pick(n, pref):
    if n <= pref:
        return n
    t = pref
    while n % t:
        t //= 2
    return t


def _layer_norm(y, g, b):
    mu = jnp.mean(y, axis=-1, keepdims=True)
    d = y - mu
    var = jnp.mean(d * d, axis=-1, keepdims=True)
    return d * lax.rsqrt(var + EPS) * g + b


def _dot(a, b):
    return jnp.dot(a, b, preferred_element_type=F32)


def _dot_nt(a, b):
    return lax.dot_general(a, b, (((1,), (1,)), ((), ())), preferred_element_type=F32)


FFN_COL_CHUNK = 512
MXU_COLS = 2 * LANES


def _ffn_act_body(x_ref, wg_ref, wu_ref, a_ref, xb_ref):
    @pl.when(pl.program_id(1) == 0)
    def _():
        xb_ref[...] = x_ref[...].astype(BF16)

    def gated(hg, hu):
        return (hg * jax.nn.sigmoid(hg) * hu).astype(a_ref.dtype)

    xb = xb_ref[...]
    tf = a_ref.shape[1]
    main = tf // MXU_COLS * MXU_COLS
    for c0 in range(0, main, FFN_COL_CHUNK):
        cs = slice(c0, min(c0 + FFN_COL_CHUNK, main))
        a_ref[:, cs] = gated(_dot(xb, wg_ref[:, cs]), _dot(xb, wu_ref[:, cs]))
    if main < tf:
        assert tf - main == LANES
        h = _dot(xb, jnp.concatenate([wg_ref[:, main:tf], wu_ref[:, main:tf]], axis=1))
        a_ref[:, main:tf] = gated(h[:, :LANES], h[:, LANES:])


def _ffn_down_ln_body(a_ref, x_ref, wd_ref, g_ref, b_ref, *o_refs, alpha):
    tm = x_ref.shape[0]
    hm = tm // 2
    for rows in (slice(0, hm), slice(hm, tm)):
        y = alpha * x_ref[rows, :] + 0.5 * _dot(a_ref[rows, :], wd_ref[...])
        y = _layer_norm(y, g_ref[...], b_ref[...])
        for o_ref in o_refs:
            o_ref[rows, :] = y.astype(o_ref.dtype)


def _ffn_ln(x, wg, wu, wd, g, b, li, si, alpha, also_bf16=False, tm_act=1024, tf=1408, tm_down=256):
    T, D = x.shape
    F = wd.shape[-2]
    tm = _pick(T, tm_act)
    tf = tf if F % tf == 0 else _pick(F, 512)
    act = pl.pallas_call(
        _ffn_act_body,
        grid=(T // tm, F // tf),
        in_specs=[
            pl.BlockSpec((tm, D), lambda t, j: (t, 0)),
            pl.BlockSpec((None, None, D, tf), lambda t, j: (li, si, 0, j)),
            pl.BlockSpec((None, None, D, tf), lambda t, j: (li, si, 0, j)),
        ],
        out_specs=pl.BlockSpec((tm, tf), lambda t, j: (t, j)),
        out_shape=jax.ShapeDtypeStruct((T, F), BF16),
        scratch_shapes=[pltpu.VMEM((tm, D), BF16)],
        compiler_params=_cparams("parallel", "arbitrary"),
        name="ffn_act",
    )(x, wg, wu)

    tm = _pick(T, tm_down)
    out_dtypes = (F32, BF16) if also_bf16 else (F32,)
    outs = pl.pallas_call(
        functools.partial(_ffn_down_ln_body, alpha=alpha),
        grid=(T // tm,),
        in_specs=[
            pl.BlockSpec((tm, F), lambda t: (t, 0)),
            pl.BlockSpec((tm, D), lambda t: (t, 0)),
            pl.BlockSpec((None, None, F, D), lambda t: (li, si, 0, 0), pipeline_mode=pl.Buffered(1)),
            pl.BlockSpec((1, D), lambda t: (0, 0)),
            pl.BlockSpec((1, D), lambda t: (0, 0)),
        ],
        out_specs=[pl.BlockSpec((tm, D), lambda t: (t, 0)) for _ in out_dtypes],
        out_shape=[jax.ShapeDtypeStruct((T, D), dt) for dt in out_dtypes],
        compiler_params=_cparams("parallel"),
        name="ffn_down_ln",
    )(act, x, wd, g, b)
    return tuple(outs) if also_bf16 else outs[0]


def _matmul_body(x_ref, w_ref, o_ref, xb_ref):
    @pl.when(pl.program_id(1) == 0)
    def _():
        xb_ref[...] = x_ref[...].astype(BF16)

    o_ref[...] = _dot(xb_ref[...], w_ref[...]).astype(o_ref.dtype)


def _matmul(x, w, li, out_dtype=F32, tm=1024, tn=1024):
    T, K = x.shape
    N = w.shape[-1]
    tm = _pick(T, tm)
    tn = _pick(N, tn)
    return pl.pallas_call(
        _matmul_body,
        grid=(T // tm, N // tn),
        in_specs=[
            pl.BlockSpec((tm, K), lambda t, j: (t, 0)),
            pl.BlockSpec((None, K, tn), lambda t, j: (li, 0, j)),
        ],
        out_specs=pl.BlockSpec((tm, tn), lambda t, j: (t, j)),
        out_shape=jax.ShapeDtypeStruct((T, N), out_dtype),
        scratch_shapes=[pltpu.VMEM((tm, K), BF16)],
        compiler_params=_cparams("parallel", "arbitrary"),
        name="in_proj",
    )(x, w)


def _proj_body(x_ref, w_ref, *o_refs, scale):
    r = _dot(x_ref[...], w_ref[...])
    for o_ref in o_refs:
        if len(o_ref.shape) == 4:
            bb, hpt, tl, _ = o_ref.shape
            for hh in range(hpt):
                o_ref[:, hh] = r[:, hh * LANES:(hh + 1) * LANES].reshape(bb, tl, LANES)
        elif o_ref.dtype == BF16:
            o_ref[...] = (r * scale).astype(BF16)
        else:
            o_ref[...] = r


def _proj(x16, w, li, col0, width, B, L, kinds, scale=1.0, tm=2048, tn=512):
    T, K = x16.shape
    tm = _pick(T, tm)
    assert width % tn == 0 and col0 % tn == 0 and tn % LANES == 0
    hpt = tn // LANES
    tl = min(L, tm)
    assert tm % tl == 0 and L % tl == 0
    bb, nl = tm // tl, L // tl
    specs, shapes = [], []
    for kind in kinds:
        if kind == "heads":
            specs.append(pl.BlockSpec((bb, hpt, tl, LANES), lambda t, j: (t // nl, j, t % nl, 0)))
            shapes.append(jax.ShapeDtypeStruct((B, width // LANES, L, LANES), F32))
        else:
            specs.append(pl.BlockSpec((tm, tn), lambda t, j: (t, j)))
            shapes.append(jax.ShapeDtypeStruct((T, width), BF16 if kind == "bf16" else F32))
    return pl.pallas_call(
        functools.partial(_proj_body, scale=scale),
        grid=(T // tm, width // tn),
        in_specs=[
            pl.BlockSpec((tm, K), lambda t, j: (t, 0)),
            pl.BlockSpec((None, K, tn), lambda t, j: (li, 0, col0 // tn + j)),
        ],
        out_specs=specs,
        out_shape=shapes,
        compiler_params=_cparams("parallel", "arbitrary"),
        name="in_proj_cols",
    )(x16, w)


def _out_ln_body(*refs, alpha, widths):
    n = len(widths)
    parts = refs[:n]
    x_ref, w_ref, g_ref, b_ref, o_ref = refs[n:]
    tm = x_ref.shape[0]
    n_split = 2 if tm % 16 == 0 else 1
    hm = tm // n_split
    for h in range(n_split):
        rows = slice(h * hm, (h + 1) * hm)
        acc = alpha * x_ref[rows, :]
        off = 0
        for p_ref, wd in zip(parts, widths):
            acc = acc + _dot(p_ref[rows, :], w_ref[off:off + wd, :])
            off += wd
        o_ref[rows, :] = _layer_norm(acc, g_ref[...], b_ref[...])


def _out_ln(parts, x, w, li, g, b, alpha, tm=512):
    T, D = x.shape
    tm = _pick(T, tm)
    widths = tuple(p.shape[1] for p in parts)
    kin = sum(widths)
    return pl.pallas_call(
        functools.partial(_out_ln_body, alpha=alpha, widths=widths),
        grid=(T // tm,),
        in_specs=[pl.BlockSpec((tm, wd), lambda t: (t, 0)) for wd in widths] + [
            pl.BlockSpec((tm, D), lambda t: (t, 0)),
            pl.BlockSpec((None, kin, D), lambda t: (li, 0, 0)),
            pl.BlockSpec((1, D), lambda t: (0, 0)),
            pl.BlockSpec((1, D), lambda t: (0, 0)),
        ],
        out_specs=pl.BlockSpec((tm, D), lambda t: (t, 0)),
        out_shape=jax.ShapeDtypeStruct((T, D), F32),
        compiler_params=_cparams("parallel"),
        name="out_proj_ln",
    )(*parts, x, w, g, b)


def _ple_body(x_ref, p_ref, wg_ref, wu_ref, o_ref, *, tn):
    xb = x_ref[...].astype(BF16)
    pb = p_ref[...].astype(BF16)
    D = o_ref.shape[1]
    for c in range(D // tn):
        sl = slice(c * tn, (c + 1) * tn)
        gate = jax.nn.sigmoid(_dot(xb, wg_ref[:, sl]))
        up = _dot(pb, wu_ref[:, sl])
        o_ref[:, sl] = x_ref[:, sl] + gate * up


def _ple(x, p, wg, wu, li, tm=512, tn=512):
    T, D = x.shape
    P = p.shape[2]
    tm = _pick(T, tm)
    return pl.pallas_call(
        functools.partial(_ple_body, tn=_pick(D, tn)),
        grid=(T // tm,),
        in_specs=[
            pl.BlockSpec((tm, D), lambda t: (t, 0)),
            pl.BlockSpec((None, tm, P), lambda t: (li, t, 0)),
            pl.BlockSpec((None, D, D), lambda t: (li, 0, 0)),
            pl.BlockSpec((None, P, D), lambda t: (li, 0, 0)),
        ],
        out_specs=pl.BlockSpec((tm, D), lambda t: (t, 0)),
        out_shape=jax.ShapeDtypeStruct((T, D), F32),
        compiler_params=_cparams("parallel"),
        name="ple_gate",
    )(x, p, wg, wu)


def _pool_body(u_ref, hist_ref, w_ref, sc_ref, o_ref, nh_ref, ext_ref, *, tl, start_pos):
    l = pl.program_id(1)
    nl = pl.num_programs(1)
    H = POOL_HIST + 1

    @pl.when(l == 0)
    def _():
        ext_ref[0:1, :] = jnp.zeros((1, ext_ref.shape[1]), F32)
        ext_ref[1:H, :] = hist_ref[...]

    @pl.when(l > 0)
    def _():
        ext_ref[0:H, :] = ext_ref[tl:tl + H, :]

    ext_ref[H:H + tl, :] = u_ref[...]

    pos = start_pos + l * tl + lax.broadcasted_iota(jnp.int32, (tl, 1), 0)
    gd = LANES
    for g, wnd in enumerate(POOL_WINDOWS):
        cs = slice(g * gd, (g + 1) * gd)
        s = ext_ref[H:H + tl, cs]
        cur = s
        for d in range(1, wnd):
            s = s + ext_ref[H - d:H - d + tl, cs]
        cnt = jnp.minimum(pos + 1, wnd).astype(F32)
        pooled = s / cnt - cur
        y = _dot(pooled.astype(BF16), w_ref[g]) * sc_ref[:, cs]
        o_ref[:, cs] = y.astype(o_ref.dtype)

    @pl.when(l == nl - 1)
    def _():
        nh_ref[...] = ext_ref[tl + 1:tl + H, :]


def _pool(u, hist, w, scale, start_pos, tl=512):
    B, L, PW = u.shape
    tl = _pick(L, tl)
    assert tl >= POOL_HIST + 1
    return pl.pallas_call(
        functools.partial(_pool_body, tl=tl, start_pos=start_pos),
        grid=(B, L // tl),
        in_specs=[
            pl.BlockSpec((None, tl, PW), lambda b, l: (b, l, 0)),
            pl.BlockSpec((None, POOL_HIST, PW), lambda b, l: (b, 0, 0)),
            pl.BlockSpec(w.shape, lambda b, l: (0, 0, 0)),
            pl.BlockSpec((1, PW), lambda b, l: (0, 0)),
        ],
        out_specs=[
            pl.BlockSpec((None, tl, PW), lambda b, l: (b, l, 0)),
            pl.BlockSpec((None, POOL_HIST, PW), lambda b, l: (b, 0, 0)),
        ],
        out_shape=[
            jax.ShapeDtypeStruct((B, L, PW), BF16),
            jax.ShapeDtypeStruct((B, POOL_HIST, PW), F32),
        ],
        scratch_shapes=[pltpu.VMEM((tl + POOL_HIST + 1, PW), F32)],
        compiler_params=_cparams("parallel", "arbitrary"),
        name="pool_mixer",
    )(u, hist, w, scale)


def _t5_bucket(rel):
    nb = N_BUCKETS // 2
    max_exact = nb // 2
    n = jnp.abs(rel)
    nf = jnp.maximum(n, 1).astype(jnp.float32)
    large = max_exact + (jnp.log(nf / max_exact) / math.log(MAX_DISTANCE / max_exact)
                         * (nb - max_exact)).astype(jnp.int32)
    large = jnp.minimum(large, nb - 1)
    return jnp.where(rel > 0, nb, 0) + jnp.where(n < max_exact, n, large)


def _attn_prep_body(tbl_ref, bkt_ref, lq1_ref, lk1_ref, lq2_ref, lk2_ref, bias_ref, lam_ref, *, tq, lam_init,
                    far_bucket):
    h = pl.program_id(0)
    bkt = bkt_ref[...]
    far = tbl_ref[far_bucket, h]
    acc = jnp.zeros(bkt.shape, F32)
    for b in range(N_BUCKETS):
        acc = jnp.where(bkt == b, tbl_ref[b, h] - far, acc)
    r = lax.broadcasted_iota(jnp.int32, bkt.shape, 1)
    c = lax.broadcasted_iota(jnp.int32, bkt.shape, 2)
    t = lax.broadcasted_iota(jnp.int32, bkt.shape, 0)
    visible = (t == 0) | ((c // CHUNK) <= (r // CHUNK))
    bias_ref[...] = jnp.where(visible, acc, NEG)
    e1 = jnp.exp(jnp.sum(lq1_ref[...] * lk1_ref[...], axis=-1, keepdims=True))
    e2 = jnp.exp(jnp.sum(lq2_ref[...] * lk2_ref[...], axis=-1, keepdims=True))
    lam_ref[...] = jnp.broadcast_to(e1 - e2 + lam_init, lam_ref.shape)


def _attn_prep(rel_bias, lq1, lk1, lq2, lk2, tq, lam_init):
    nbk, H = rel_bias.shape
    r = jnp.arange(tq, dtype=jnp.int32)[:, None]
    c = jnp.arange(tq, dtype=jnp.int32)[None, :]
    bkt = jnp.stack([_t5_bucket(c - r - tq), _t5_bucket(c - r)])
    far_bucket = N_BUCKETS // 2 - 1
    assert tq >= MAX_DISTANCE
    row = lambda a: a.reshape(1, -1).astype(F32)
    return pl.pallas_call(
        functools.partial(_attn_prep_body, tq=tq, lam_init=lam_init, far_bucket=far_bucket),
        grid=(H,),
        in_specs=[
            pl.BlockSpec(memory_space=pltpu.SMEM),
            pl.BlockSpec((2, tq, tq), lambda h: (0, 0, 0)),
            pl.BlockSpec((1, lq1.shape[-1]), lambda h: (0, 0)),
            pl.BlockSpec((1, lq1.shape[-1]), lambda h: (0, 0)),
            pl.BlockSpec((1, lq1.shape[-1]), lambda h: (0, 0)),
            pl.BlockSpec((1, lq1.shape[-1]), lambda h: (0, 0)),
        ],
        out_specs=[
            pl.BlockSpec((None, 2, tq, tq), lambda h: (h, 0, 0, 0)),
            pl.BlockSpec((8, LANES), lambda h: (0, 0)),
        ],
        out_shape=[
            jax.ShapeDtypeStruct((H, 2, tq, tq), F32),
            jax.ShapeDtypeStruct((8, LANES), F32),
        ],
        compiler_params=_cparams("arbitrary"),
        name="attn_prep",
    )(rel_bias.astype(F32), bkt, row(lq1), row(lk1), row(lq2), row(lk2))


def _split_q(q):
    lane = lax.broadcasted_iota(jnp.int32, q.shape, 1)
    half = q.shape[1] // 2
    zero = jnp.zeros_like(q)
    return jnp.concatenate([jnp.where(lane < half, q, zero), jnp.where(lane >= half, q, zero)], axis=0)


def _attend(qq, spans, s_ref, mx_ref, l_ref, acc_ref):
    rows = qq.shape[0]
    mx_ref[...] = jnp.full(mx_ref.shape, NEG, F32)
    for col, get_k, _, bias in spans:
        s = _dot_nt(qq, get_k())
        w = s.shape[1]
        if bias is not None:
            s = (s.reshape(2, rows // 2, w) + bias[None]).reshape(rows, w)
        s_ref[:, col:col + w] = s
        if w % LANES == 0:
            m = functools.reduce(jnp.maximum, [s[:, c:c + LANES] for c in range(0, w, LANES)])
            mx_ref[...] = jnp.maximum(mx_ref[...], m)
        else:
            mx_ref[:, 0:w] = jnp.maximum(mx_ref[:, 0:w], s)
    m_b = jnp.broadcast_to(jnp.max(mx_ref[...], axis=1, keepdims=True), mx_ref.shape)
    mx_ref[...] = m_b
    l_ref[...] = jnp.zeros_like(l_ref)
    acc_ref[...] = jnp.zeros_like(acc_ref)
    for col, get_k, get_v, _ in spans:
        w = get_v().shape[0]
        s = s_ref[:, col:col + w]
        m_b = mx_ref[...]
        if w % LANES == 0:
            ps = [jnp.exp(s[:, c:c + LANES] - m_b) for c in range(0, w, LANES)]
            l_ref[...] += functools.reduce(jnp.add, ps)
            p = ps[0] if len(ps) == 1 else jnp.concatenate(ps, axis=1)
        else:
            p = jnp.exp(s - m_b[:, 0:w])
            l_ref[:, 0:w] += p
        acc_ref[...] += _dot(p.astype(BF16), get_v())


def _attn_finish(lam_ref, g_ref, l_ref, acc_ref, tq, out_scale):
    lam = lam_ref[0:1, 0:1]
    o = acc_ref[...] / jnp.sum(l_ref[...], axis=1, keepdims=True)
    o = o[:tq] - lam * o[tq:]
    return o * lax.rsqrt(jnp.mean(o * o, axis=-1, keepdims=True) + EPS) * g_ref[...] * out_scale


ATTN_KEY_SPAN = 512
ATTN_SAMPLE_HEADS = 2


def _attn_prompt_body(q_ref, k_ref, v_ref, bias_ref, lam_ref, g_ref, o_ref, s_ref, mx_ref, l_ref, acc_ref, *,
                      tq, out_scale):
    L = q_ref.shape[0]

    def span(st, w, bias):
        return (st, lambda: k_ref[st:st + w, :], lambda: v_ref[st:st + w, :], bias)

    for qi in reversed(range(L // tq)):
        par = qi % 2
        far_end = max(qi - 1, 0) * tq
        spans = [span(st, min(ATTN_KEY_SPAN, far_end - st), None) for st in range(0, far_end, ATTN_KEY_SPAN)]
        if qi >= 1:
            spans.append(span((qi - 1) * tq, tq, bias_ref[0]))
        spans.append(span(qi * tq, tq, bias_ref[1]))
        qq = _split_q(q_ref[qi * tq:(qi + 1) * tq, :])
        _attend(qq, spans, s_ref.at[par], mx_ref.at[par], l_ref.at[par], acc_ref.at[par])
        y = _attn_finish(lam_ref, g_ref, l_ref.at[par], acc_ref.at[par], tq, out_scale)
        o_ref[qi * tq:(qi + 1) * tq, :] = y.astype(o_ref.dtype)


def _attn_prompt(q, k, v, bias, lam, g, out_scale):
    B, L, _ = q.shape
    H = q.shape[2] // LANES
    tq = bias.shape[-1]
    assert L % tq == 0 and tq % CHUNK == 0
    return pl.pallas_call(
        functools.partial(_attn_prompt_body, tq=tq, out_scale=out_scale),
        grid=(B, H),
        in_specs=[
            pl.BlockSpec((None, L, LANES), lambda b, h: (b, 0, h)),
            pl.BlockSpec((None, L, LANES), lambda b, h: (b, 0, h)),
            pl.BlockSpec((None, L, LANES), lambda b, h: (b, 0, h)),
            pl.BlockSpec((None, 2, tq, tq), lambda b, h: (h, 0, 0, 0)),
            pl.BlockSpec((8, LANES), lambda b, h: (0, 0)),
            pl.BlockSpec((1, LANES), lambda b, h: (0, 0)),
        ],
        out_specs=pl.BlockSpec((None, L, LANES), lambda b, h: (b, 0, h)),
        out_shape=jax.ShapeDtypeStruct((B, L, H * LANES), BF16),
        scratch_shapes=[
            pltpu.VMEM((2, 2 * tq, L), F32),
            pltpu.VMEM((2, 2 * tq, LANES), F32),
            pltpu.VMEM((2, 2 * tq, LANES), F32),
            pltpu.VMEM((2, 2 * tq, LANES), F32),
        ],
        compiler_params=_cparams("parallel", "parallel"),
        name="diff_attn_prompt",
    )(q, k, v, bias, lam, g)


def _attn_sample_body(q_ref, kn_ref, vn_ref, kc_ref, vc_ref, bprev_ref, bdiag_ref, lam_ref, g_ref, o_ref,
                      s_ref, mx_ref, l_ref, acc_ref, *, lq, tk, out_scale):
    hp, P = kc_ref.shape[0], kc_ref.shape[1]
    near = P - tk
    for i in range(hp):
        cs = slice(i * LANES, (i + 1) * LANES)

        def span(st, w, bias, i=i):
            return (st, lambda: kc_ref[i, st:st + w, :].astype(BF16),
                    lambda: vc_ref[i, st:st + w, :].astype(BF16), bias)

        spans = [span(st, min(ATTN_KEY_SPAN, near - st), None) for st in range(0, near, ATTN_KEY_SPAN)]
        spans.append(span(near, tk, bprev_ref[i]))
        spans.append((P, lambda cs=cs: kn_ref[:, cs], lambda cs=cs: vn_ref[:, cs], bdiag_ref[i, :, 0:lq]))
        _attend(_split_q(q_ref[:, cs]), spans, s_ref.at[i], mx_ref.at[i], l_ref.at[i], acc_ref.at[i])
        y = _attn_finish(lam_ref, g_ref, l_ref.at[i], acc_ref.at[i], lq, out_scale)
        o_ref[:, cs] = y.astype(o_ref.dtype)


def _attn_sample(q, k, v, k_cache, v_cache, bias, lam, g, out_scale):
    B, lq, _ = q.shape
    H = q.shape[2] // LANES
    P = k_cache.shape[2]
    tk = bias.shape[-1]
    assert lq == CHUNK and P % tk == 0 and P % CHUNK == 0 and lq <= tk
    hp = ATTN_SAMPLE_HEADS if H % ATTN_SAMPLE_HEADS == 0 else 1
    ng = H // hp
    return pl.pallas_call(
        functools.partial(_attn_sample_body, lq=lq, tk=tk, out_scale=out_scale),
        grid=(B, ng),
        in_specs=[
            pl.BlockSpec((None, lq, hp * LANES), lambda b, g: (b, 0, g)),
            pl.BlockSpec((None, lq, hp * LANES), lambda b, g: (b, 0, g)),
            pl.BlockSpec((None, lq, hp * LANES), lambda b, g: (b, 0, g)),
            pl.BlockSpec((None, hp, P, LANES), lambda b, g: (b, g, 0, 0)),
            pl.BlockSpec((None, hp, P, LANES), lambda b, g: (b, g, 0, 0)),
            pl.BlockSpec((hp, None, lq, tk), lambda b, g: (g, 0, 0, 0)),
            pl.BlockSpec((hp, None, lq, tk), lambda b, g: (g, 1, 0, 0)),
            pl.BlockSpec((8, LANES), lambda b, g: (0, 0)),
            pl.BlockSpec((1, LANES), lambda b, g: (0, 0)),
        ],
        out_specs=pl.BlockSpec((None, lq, hp * LANES), lambda b, g: (b, 0, g)),
        out_shape=jax.ShapeDtypeStruct((B, lq, H * LANES), BF16),
        scratch_shapes=[
            pltpu.VMEM((hp, 2 * lq, P + LANES), F32),
            pltpu.VMEM((hp, 2 * lq, LANES), F32),
            pltpu.VMEM((hp, 2 * lq, LANES), F32),
            pltpu.VMEM((hp, 2 * lq, LANES), F32),
        ],
        compiler_params=_cparams("parallel", "arbitrary"),
        name="diff_attn_sample",
    )(q, k, v, k_cache, v_cache, bias, bias, lam, g)


HGRN_PROJ_HEADS = 1
HGRN_HEADS_PER_STEP = 4
MIN_LOG2 = -150.0


def _hgrn_setup(lb_ref, lvl_ref, ck, layer):
    n_lev = ck.bit_length() - 1
    t_i = lax.broadcasted_iota(jnp.int32, (ck, ck), 0)
    s_i = lax.broadcasted_iota(jnp.int32, (ck, ck), 1)
    x = t_i ^ s_i
    hb = jnp.zeros((ck, ck), jnp.int32)
    for b in range(1, n_lev):
        hb = hb + (x >= (1 << b)).astype(jnp.int32)
    lvl_ref[...] = jnp.where(t_i > s_i, hb, -1)

    lg = lb_ref[...]
    e = jnp.exp(lg - jnp.max(lg, axis=0, keepdims=True))
    p = e / jnp.sum(e, axis=0, keepdims=True)
    cum = p[0:1]
    for d in range(1, layer + 1):
        cum = cum + p[d:d + 1]
    lb = cum - p[0:1]
    return lb, 1.0 - lb, lax.broadcasted_iota(jnp.int32, (ck, LANES), 0)


def _hgrn_chunk(hq, z, v, hg, consts, lvl_ref, ng_ref, st_ref):
    lb, one_mlb, row = consts
    ck = hq.shape[0]
    n_lev = ck.bit_length() - 1
    q = hq * jax.nn.sigmoid(hq)
    r = 1.0 / (1.0 + jnp.exp(z))
    k = one_mlb * r
    f = lb + one_mlb * (1.0 - r)
    g = jnp.maximum(jnp.log2(f), MIN_LOG2)
    vb = v.astype(BF16)

    pf = g
    tot = g
    lvl = lvl_ref[...]
    a = jnp.zeros((ck, ck), F32)
    for lev in range(n_lev):
        hsz = 1 << lev
        ql = (q * jnp.exp2(pf)).astype(BF16)
        kl = (k if lev == 0 else k * jnp.exp2(tot - pf)).astype(BF16)
        a = jnp.where(lvl == lev, _dot_nt(ql, kl), a)
        if hsz % 8 == 0:
            nb = ck // (2 * hsz)
            t4 = tot.reshape(nb, 2, hsz, LANES)
            p4 = pf.reshape(nb, 2, hsz, LANES)
            both = t4[:, 0:1] + t4[:, 1:2]
            tot = jnp.concatenate([both, both], axis=1).reshape(ck, LANES)
            pf = jnp.concatenate([p4[:, 0:1], p4[:, 1:2] + t4[:, 0:1]], axis=1).reshape(ck, LANES)
        else:
            second = (row & hsz) != 0
            up = pltpu.roll(tot, hsz, 0)
            dn = pltpu.roll(tot, ck - hsz, 0)
            pf = pf + jnp.where(second, up, 0.0)
            tot = tot + jnp.where(second, up, dn)
    st = st_ref[...]
    o = _dot(a.astype(BF16), vb) + jnp.sum(q * k, axis=1, keepdims=True) * v
    o = o + _dot_nt((q * jnp.exp2(pf)).astype(BF16), st.astype(BF16))
    kst = (k * jnp.exp2(tot - pf)).astype(BF16)
    st_ref[...] = st * jnp.exp2(tot[0:1, :]) + _dot(v.T.astype(BF16), kst)
    y = o * lax.rsqrt(jnp.mean(o * o, axis=-1, keepdims=True) + EPS) * ng_ref[...]
    return y * (hg * jax.nn.sigmoid(hg))


def _hgrn_body(hq_ref, hz_ref, hi_ref, hg_ref, lb_ref, ng_ref, s0_ref, o_ref, s_ref, st_ref, lvl_ref, *, ck,
               layer):
    l = pl.program_id(2)
    tl = hq_ref.shape[0]
    hp = st_ref.shape[0]

    @pl.when(l == 0)
    def _():
        for i in range(hp):
            st_ref[i] = s0_ref[i].T

    lb, one_mlb, row = _hgrn_setup(lb_ref, lvl_ref, ck, layer)

    def chunk(c, carry):
        rs = slice(c * ck, (c + 1) * ck) if isinstance(c, int) else pl.ds(pl.multiple_of(c * ck, ck), ck)
        for i in range(hp):
            cs = slice(i * LANES, (i + 1) * LANES)
            y = _hgrn_chunk(hq_ref[rs, cs], hz_ref[rs, cs], hi_ref[rs, cs], hg_ref[rs, cs],
                            (lb[:, cs], one_mlb[:, cs], row), lvl_ref, ng_ref, st_ref.at[i])
            o_ref[rs, cs] = y.astype(o_ref.dtype)
        return carry

    n_chunks = tl // ck
    if n_chunks == 1:
        chunk(0, 0)
    else:
        lax.fori_loop(0, n_chunks, chunk, 0, unroll=4 if n_chunks % 4 == 0 else 1)

    @pl.when(l == pl.num_programs(2) - 1)
    def _():
        for i in range(hp):
            s_ref[i] = st_ref[i].T


def _hgrn(h, lb_logits, norm_g, s0, layer, n_heads, tl=2048, ck=128):
    B, L, _ = h.shape
    H = n_heads
    tl = _pick(L, tl)
    ck = _pick(tl, ck)
    assert ck & (ck - 1) == 0 and ck >= 8
    depth = lb_logits.shape[0]
    hp = HGRN_HEADS_PER_STEP if (L <= LANES and H % HGRN_HEADS_PER_STEP == 0) else 1
    ng = H // hp
    blk = lambda part: pl.BlockSpec((None, tl, hp * LANES), lambda b, g, l: (b, l, part * ng + g))
    return pl.pallas_call(
        functools.partial(_hgrn_body, ck=ck, layer=layer),
        grid=(B, ng, L // tl),
        in_specs=[
            blk(0), blk(1), blk(2), blk(3),
            pl.BlockSpec((depth, hp * LANES), lambda b, g, l: (0, g)),
            pl.BlockSpec((1, LANES), lambda b, g, l: (0, 0)),
            pl.BlockSpec((None, hp, LANES, LANES), lambda b, g, l: (b, g, 0, 0)),
        ],
        out_specs=[
            pl.BlockSpec((None, tl, hp * LANES), lambda b, g, l: (b, l, g)),
            pl.BlockSpec((None, hp, LANES, LANES), lambda b, g, l: (b, g, 0, 0)),
        ],
        out_shape=[
            jax.ShapeDtypeStruct((B, L, H * LANES), BF16),
            jax.ShapeDtypeStruct((B, H, LANES, LANES), F32),
        ],
        scratch_shapes=[pltpu.VMEM((hp, LANES, LANES), F32), pltpu.VMEM((ck, ck), jnp.int32)],
        compiler_params=_cparams("parallel", "parallel", "arbitrary"),
        name="hgrn_scan",
    )(h, h, h, h, lb_logits, norm_g, s0)


def _hgrn_proj_body(x_ref, w_ref, lb_ref, ng_ref, s0_ref, o_ref, s_ref, st_ref, lvl_ref, h_ref, *, ck, layer,
                    rows):
    L = x_ref.shape[0]
    hp = w_ref.shape[0]
    lb, one_mlb, row = _hgrn_setup(lb_ref, lvl_ref, ck, layer)
    for i in range(hp):
        cs = slice(i * LANES, (i + 1) * LANES)
        consts = (lb[:, cs], one_mlb[:, cs], row)
        st = st_ref.at[i]
        st[...] = s0_ref[i].T

        def project(s, i=i):
            h_ref[i, s % 2] = _dot(x_ref[s * rows:(s + 1) * rows, :], w_ref[i])

        project(0)
        for s in range(L // rows):
            if (s + 1) * rows < L:
                project(s + 1)
            h = h_ref.at[i, s % 2]
            for c in range(rows // ck):
                rs = slice(c * ck, (c + 1) * ck)
                y = _hgrn_chunk(h[rs, 0:LANES], h[rs, LANES:2 * LANES], h[rs, 2 * LANES:3 * LANES],
                                h[rs, 3 * LANES:4 * LANES], consts, lvl_ref, ng_ref, st)
                o_ref[s * rows + c * ck:s * rows + (c + 1) * ck, cs] = y.astype(o_ref.dtype)
        s_ref[i] = st[...].T


def _hgrn_proj(x, w_heads, lb_logits, norm_g, s0, layer, ck=128, rows=512):
    B, L, D = x.shape
    H = w_heads.shape[0]
    assert L % rows == 0 and rows % ck == 0 and ck & (ck - 1) == 0
    depth = lb_logits.shape[0]
    hp = HGRN_PROJ_HEADS if H % HGRN_PROJ_HEADS == 0 else 1
    return pl.pallas_call(
        functools.partial(_hgrn_proj_body, ck=ck, layer=layer, rows=rows),
        grid=(B, H // hp),
        in_specs=[
            pl.BlockSpec((None, L, D), lambda b, g: (b, 0, 0)),
            pl.BlockSpec((hp, D, 4 * LANES), lambda b, g: (g, 0, 0)),
            pl.BlockSpec((depth, hp * LANES), lambda b, g: (0, g)),
            pl.BlockSpec((1, LANES), lambda b, g: (0, 0)),
            pl.BlockSpec((None, hp, LANES, LANES), lambda b, g: (b, g, 0, 0)),
        ],
        out_specs=[
            pl.BlockSpec((None, L, hp * LANES), lambda b, g: (b, 0, g)),
            pl.BlockSpec((None, hp, LANES, LANES), lambda b, g: (b, g, 0, 0)),
        ],
        out_shape=[
            jax.ShapeDtypeStruct((B, L, H * LANES), BF16),
            jax.ShapeDtypeStruct((B, H, LANES, LANES), F32),
        ],
        scratch_shapes=[
            pltpu.VMEM((hp, LANES, LANES), F32),
            pltpu.VMEM((ck, ck), jnp.int32),
            pltpu.VMEM((hp, 2, rows, 4 * LANES), F32),
        ],
        compiler_params=_cparams("parallel", "arbitrary"),
        name="hgrn_proj_scan",
    )(x, w_heads, lb_logits, norm_g, s0)


def _trunk(x, p, k_cache, v_cache, pool_hist, hg_state, W, attn_prep):
    B, L, D = x.shape
    depth = W["ln_g"].shape[0]
    alpha = (2 * depth) ** 0.25
    T = B * L
    xt = x.reshape(T, D)
    new_k, new_v, new_pool, new_s = [], [], [], []
    pool_width = W["pool_scale"].shape[-1]
    da_width = D - pool_width
    n_da_heads = da_width // LANES
    n_hg_heads = D // LANES
    lnrow = lambda a, i, s: a[i, s].reshape(1, D)
    for i in range(depth):
        fuse_proj = i % 2 == 1 and L % HGRN_PROJ_ROWS == 0
        want16 = fuse_proj or i % 2 == 0
        xt = _ffn_ln(xt, W["wg"], W["wu"], W["wd"], lnrow(W["ln_g"], i, 0), lnrow(W["ln_b"], i, 0), i, 0, alpha,
                     also_bf16=want16)
        if want16:
            xt, xt16 = xt
        if i % 2 == 0:
            e = i // 2
            w_in = W["w_in_even"]
            (u,) = _proj(xt16, w_in, e, 0, pool_width, B, L, ("f32",))
            (q16,) = _proj(xt16, w_in, e, pool_width, da_width, B, L, ("bf16",), scale=(LANES // 2) ** -0.5)
            k, k16 = _proj(xt16, w_in, e, pool_width + da_width, da_width, B, L, ("heads", "bf16"))
            v, v16 = _proj(xt16, w_in, e, pool_width + 2 * da_width, da_width, B, L, ("heads", "bf16"))
            past = 0 if k_cache is None else k_cache.shape[2]
            pool_out, nh = _pool(u.reshape(B, L, pool_width), pool_hist[e], W["pool_w"][e],
                                 W["pool_scale"][e].reshape(1, pool_width), past)
            bias, lam = attn_prep[e]
            lam_init = 0.8 - 0.6 * math.exp(-0.3 * i)
            g = W["diff_norm_g"][e].reshape(1, LANES)
            q3, k3, v3 = (a.reshape(B, L, da_width) for a in (q16, k16, v16))
            if k_cache is None:
                o = _attn_prompt(q3, k3, v3, bias, lam, g, 1.0 - lam_init)
            else:
                kc = jnp.transpose(k_cache[e], (0, 2, 1, 3))
                vc = jnp.transpose(v_cache[e], (0, 2, 1, 3))
                o = _attn_sample(q3, k3, v3, kc, vc, bias, lam, g, 1.0 - lam_init)
            parts = [pool_out.reshape(T, pool_width), o.reshape(T, da_width)]
            w_out = W["w_out_even"]
            new_k.append(jnp.transpose(k, (0, 2, 1, 3)))
            new_v.append(jnp.transpose(v, (0, 2, 1, 3)))
            new_pool.append(nh)
            li = e
        else:
            od = i // 2
            ng = W["hgrn_norm_g"][od].reshape(1, LANES)
            if fuse_proj:
                o, s = _hgrn_proj(xt16.reshape(B, L, D), W["w_in_odd_heads"][od], W["lb_logits"], ng,
                                  hg_state[od], i, rows=HGRN_PROJ_ROWS)
            else:
                h = _matmul(xt, W["w_in_odd"], od)
                o, s = _hgrn(h.reshape(B, L, 4 * D), W["lb_logits"], ng, hg_state[od], i, n_hg_heads)
            parts = [o.reshape(T, D)]
            w_out = W["w_out_odd"]
            new_s.append(s)
            li = od
        xt = _out_ln(parts, xt, w_out, li, lnrow(W["ln_g"], i, 1), lnrow(W["ln_b"], i, 1), alpha)
        xt = _ffn_ln(xt, W["wg"], W["wu"], W["wd"], lnrow(W["ln_g"], i, 2), lnrow(W["ln_b"], i, 2), i, 1, alpha)
        xt = _ple(xt, p.reshape(depth, T, -1), W["w_ple_gate"], W["w_ple_up"], i)
    return xt.reshape(B, L, D), jnp.stack(new_k), jnp.stack(new_v), jnp.stack(new_pool), jnp.stack(new_s)


ATTN_TILE = 256
HGRN_PROJ_ROWS = 512


def kernel(x_prompt, x_sample, cache_diff_k, cache_diff_v, state_pool, state_hgrn, p_prompt, p_sample, ln_g, ln_b, w_ffn_gate, w_ffn_up, w_ffn_down, w_ple_gate, w_ple_up, w_in_even, w_out_even, pool_w, pool_scale, lam_q1, lam_k1, lam_q2, lam_k2, diff_norm_g, rel_bias, w_in_odd, w_out_odd, hgrn_norm_g, hgrn_lb_logits):
    bf = lambda a: a.astype(BF16)
    W = dict(
        ln_g=ln_g.astype(F32), ln_b=ln_b.astype(F32),
        wg=bf(w_ffn_gate), wu=bf(w_ffn_up), wd=bf(w_ffn_down),
        w_ple_gate=bf(w_ple_gate), w_ple_up=bf(w_ple_up),
        w_in_even=bf(w_in_even), w_out_even=bf(w_out_even),
        pool_w=bf(pool_w), pool_scale=pool_scale.astype(F32),
        diff_norm_g=diff_norm_g.astype(F32),
        w_in_odd=bf(w_in_odd), w_out_odd=bf(w_out_odd),
        hgrn_norm_g=hgrn_norm_g.astype(F32), lb_logits=hgrn_lb_logits.astype(F32),
    )
    n_even = w_in_even.shape[0]
    n_odd = w_in_odd.shape[0]
    d_model = w_in_odd.shape[1]
    n_hg = d_model // LANES
    W["w_in_odd_heads"] = jnp.transpose(W["w_in_odd"].reshape(n_odd, d_model, 4, n_hg, LANES),
                                        (0, 3, 1, 2, 4)).reshape(n_odd, n_hg, d_model, 4 * LANES)
    attn_prep = []
    for e in range(n_even):
        lam_init = 0.8 - 0.6 * math.exp(-0.3 * (2 * e))
        attn_prep.append(_attn_prep(rel_bias, lam_q1[e], lam_k1[e], lam_q2[e], lam_k2[e], ATTN_TILE, lam_init))

    B = x_prompt.shape[0]
    dt = x_prompt.dtype
    zero_pool = jnp.zeros((n_even, B) + state_pool.shape[2:], dt)
    zero_s = jnp.zeros((n_odd, B) + state_hgrn.shape[2:], dt)
    y_p, k_p, v_p, pool_p, s_p = _trunk(x_prompt, p_prompt, None, None, zero_pool, zero_s, W, attn_prep)
    y_s, k_s, v_s, pool_s, s_s = _trunk(x_sample, p_sample, cache_diff_k, cache_diff_v, state_pool, state_hgrn,
                                        W, attn_prep)
    return (y_p, y_s, k_p, v_p, k_s, v_s, pool_p, pool_s, s_p, s_s)
```

```python
import functools
import math

import jax
import jax.numpy as jnp
from jax import lax
from jax.experimental import pallas as pl
from jax.experimental.pallas import tpu as pltpu

F32 = jnp.float32
BF16 = jnp.bfloat16

CHUNK = 64
POOL_WINDOWS = (2, 4, 8, 16)
POOL_HIST = max(POOL_WINDOWS) - 1
N_BUCKETS = 32
MAX_DISTANCE = 128
EPS = 1e-5
NEG = -1e30
LANES = 128

VMEM_LIMIT = 56 * 1024 * 1024


def _cparams(*sem):
    return pltpu.CompilerParams(dimension_semantics=sem, vmem_limit_bytes=VMEM_LIMIT)


def _pick(n, pref):
    if n <= pref:
        return n
    t = pref
    while n % t:
        t //= 2
    return t


def _layer_norm(y, g, b):
    mu = jnp.mean(y, axis=-1, keepdims=True)
    d = y - mu
    var = jnp.mean(d * d, axis=-1, keepdims=True)
    return d * lax.rsqrt(var + EPS) * g + b


def _dot(a, b):
    return jnp.dot(a, b, preferred_element_type=F32)


def _dot_nt(a, b):
    return lax.dot_general(a, b, (((1,), (1,)), ((), ())), preferred_element_type=F32)


FFN_COL_CHUNK = 512
MXU_COLS = 2 * LANES


def _ffn_act_body(x_ref, wg_ref, wu_ref, a_ref, xb_ref):
    @pl.when(pl.program_id(1) == 0)
    def _():
        xb_ref[...] = x_ref[...].astype(BF16)

    def gated(hg, hu):
        return (hg * jax.nn.sigmoid(hg) * hu).astype(a_ref.dtype)

    xb = xb_ref[...]
    tf = a_ref.shape[1]
    main = tf // MXU_COLS * MXU_COLS
    for c0 in range(0, main, FFN_COL_CHUNK):
        cs = slice(c0, min(c0 + FFN_COL_CHUNK, main))
        a_ref[:, cs] = gated(_dot(xb, wg_ref[:, cs]), _dot(xb, wu_ref[:, cs]))
    if main < tf:
        assert tf - main == LANES
        h = _dot(xb, jnp.concatenate([wg_ref[:, main:tf], wu_ref[:, main:tf]], axis=1))
        a_ref[:, main:tf] = gated(h[:, :LANES], h[:, LANES:])


def _ffn_down_ln_body(a_ref, x_ref, wd_ref, g_ref, b_ref, *o_refs, alpha):
    tm = x_ref.shape[0]
    hm = tm // 2
    for rows in (slice(0, hm), slice(hm, tm)):
        y = alpha * x_ref[rows, :] + 0.5 * _dot(a_ref[rows, :], wd_ref[...])
        y = _layer_norm(y, g_ref[...], b_ref[...])
        for o_ref in o_refs:
            o_ref[rows, :] = y.astype(o_ref.dtype)


def _ffn_ln(x, wg, wu, wd, g, b, li, si, alpha, also_bf16=False, tm_act=1024, tf=1408, tm_down=256):
    T, D = x.shape
    F = wd.shape[-2]
    tm = _pick(T, tm_act)
    tf = tf if F % tf == 0 else _pick(F, 512)
    act = pl.pallas_call(
        _ffn_act_body,
        grid=(T // tm, F // tf),
        in_specs=[
            pl.BlockSpec((tm, D), lambda t, j: (t, 0)),
            pl.BlockSpec((None, None, D, tf), lambda t, j: (li, si, 0, j)),
            pl.BlockSpec((None, None, D, tf), lambda t, j: (li, si, 0, j)),
        ],
        out_specs=pl.BlockSpec((tm, tf), lambda t, j: (t, j)),
        out_shape=jax.ShapeDtypeStruct((T, F), BF16),
        scratch_shapes=[pltpu.VMEM((tm, D), BF16)],
        compiler_params=_cparams("parallel", "arbitrary"),
        name="ffn_act",
    )(x, wg, wu)

    tm = _pick(T, tm_down)
    out_dtypes = (F32, BF16) if also_bf16 else (F32,)
    outs = pl.pallas_call(
        functools.partial(_ffn_down_ln_body, alpha=alpha),
        grid=(T // tm,),
        in_specs=[
            pl.BlockSpec((tm, F), lambda t: (t, 0)),
            pl.BlockSpec((tm, D), lambda t: (t, 0)),
            pl.BlockSpec((None, None, F, D), lambda t: (li, si, 0, 0), pipeline_mode=pl.Buffered(1)),
            pl.BlockSpec((1, D), lambda t: (0, 0)),
            pl.BlockSpec((1, D), lambda t: (0, 0)),
        ],
        out_specs=[pl.BlockSpec((tm, D), lambda t: (t, 0)) for _ in out_dtypes],
        out_shape=[jax.ShapeDtypeStruct((T, D), dt) for dt in out_dtypes],
        compiler_params=_cparams("parallel"),
        name="ffn_down_ln",
    )(act, x, wd, g, b)
    return tuple(outs) if also_bf16 else outs[0]


def _matmul_body(x_ref, w_ref, o_ref, xb_ref):
    @pl.when(pl.program_id(1) == 0)
    def _():
        xb_ref[...] = x_ref[...].astype(BF16)

    o_ref[...] = _dot(xb_ref[...], w_ref[...]).astype(o_ref.dtype)


def _matmul(x, w, li, out_dtype=F32, tm=1024, tn=1024):
    T, K = x.shape
    N = w.shape[-1]
    tm = _pick(T, tm)
    tn = _pick(N, tn)
    return pl.pallas_call(
        _matmul_body,
        grid=(T // tm, N // tn),
        in_specs=[
            pl.BlockSpec((tm, K), lambda t, j: (t, 0)),
            pl.BlockSpec((None, K, tn), lambda t, j: (li, 0, j)),
        ],
        out_specs=pl.BlockSpec((tm, tn), lambda t, j: (t, j)),
        out_shape=jax.ShapeDtypeStruct((T, N), out_dtype),
        scratch_shapes=[pltpu.VMEM((tm, K), BF16)],
        compiler_params=_cparams("parallel", "arbitrary"),
        name="in_proj",
    )(x, w)


def _proj_body(x_ref, w_ref, *o_refs, scale):
    r = _dot(x_ref[...], w_ref[...])
    for o_ref in o_refs:
        if len(o_ref.shape) == 4:
            bb, hpt, tl, _ = o_ref.shape
            for hh in range(hpt):
                o_ref[:, hh] = r[:, hh * LANES:(hh + 1) * LANES].reshape(bb, tl, LANES)
        elif o_ref.dtype == BF16:
            o_ref[...] = (r * scale).astype(BF16)
        else:
            o_ref[...] = r


def _proj(x16, w, li, col0, width, B, L, kinds, scale=1.0, tm=2048, tn=512):
    T, K = x16.shape
    tm = _pick(T, tm)
    assert width % tn == 0 and col0 % tn == 0 and tn % LANES == 0
    hpt = tn // LANES
    tl = min(L, tm)
    assert tm % tl == 0 and L % tl == 0
    bb, nl = tm // tl, L // tl
    specs, shapes = [], []
    for kind in kinds:
        if kind == "heads":
            specs.append(pl.BlockSpec((bb, hpt, tl, LANES), lambda t, j: (t // nl, j, t % nl, 0)))
            shapes.append(jax.ShapeDtypeStruct((B, width // LANES, L, LANES), F32))
        else:
            specs.append(pl.BlockSpec((tm, tn), lambda t, j: (t, j)))
            shapes.append(jax.ShapeDtypeStruct((T, width), BF16 if kind == "bf16" else F32))
    return pl.pallas_call(
        functools.partial(_proj_body, scale=scale),
        grid=(T // tm, width // tn),
        in_specs=[
            pl.BlockSpec((tm, K), lambda t, j: (t, 0)),
            pl.BlockSpec((None, K, tn), lambda t, j: (li, 0, col0 // tn + j)),
        ],
        out_specs=specs,
        out_shape=shapes,
        compiler_params=_cparams("parallel", "arbitrary"),
        name="in_proj_cols",
    )(x16, w)


def _out_ln_body(*refs, alpha, widths):
    n = len(widths)
    parts = refs[:n]
    x_ref, w_ref, g_ref, b_ref, o_ref = refs[n:]
    tm = x_ref.shape[0]
    n_split = 2 if tm % 16 == 0 else 1
    hm = tm // n_split
    for h in range(n_split):
        rows = slice(h * hm, (h + 1) * hm)
        acc = alpha * x_ref[rows, :]
        off = 0
        for p_ref, wd in zip(parts, widths):
            acc = acc + _dot(p_ref[rows, :], w_ref[off:off + wd, :])
            off += wd
        o_ref[rows, :] = _layer_norm(acc, g_ref[...], b_ref[...])


def _out_ln(parts, x, w, li, g, b, alpha, tm=512):
    T, D = x.shape
    tm = _pick(T, tm)
    widths = tuple(p.shape[1] for p in parts)
    kin = sum(widths)
    return pl.pallas_call(
        functools.partial(_out_ln_body, alpha=alpha, widths=widths),
        grid=(T // tm,),
        in_specs=[pl.BlockSpec((tm, wd), lambda t: (t, 0)) for wd in widths] + [
            pl.BlockSpec((tm, D), lambda t: (t, 0)),
            pl.BlockSpec((None, kin, D), lambda t: (li, 0, 0)),
            pl.BlockSpec((1, D), lambda t: (0, 0)),
            pl.BlockSpec((1, D), lambda t: (0, 0)),
        ],
        out_specs=pl.BlockSpec((tm, D), lambda t: (t, 0)),
        out_shape=jax.ShapeDtypeStruct((T, D), F32),
        compiler_params=_cparams("parallel"),
        name="out_proj_ln",
    )(*parts, x, w, g, b)


def _ple_body(x_ref, p_ref, wg_ref, wu_ref, o_ref, *, tn):
    xb = x_ref[...].astype(BF16)
    pb = p_ref[...].astype(BF16)
    D = o_ref.shape[1]
    for c in range(D // tn):
        sl = slice(c * tn, (c + 1) * tn)
        gate = jax.nn.sigmoid(_dot(xb, wg_ref[:, sl]))
        up = _dot(pb, wu_ref[:, sl])
        o_ref[:, sl] = x_ref[:, sl] + gate * up


def _ple(x, p, wg, wu, li, tm=512, tn=512):
    T, D = x.shape
    P = p.shape[2]
    tm = _pick(T, tm)
    return pl.pallas_call(
        functools.partial(_ple_body, tn=_pick(D, tn)),
        grid=(T // tm,),
        in_specs=[
            pl.BlockSpec((tm, D), lambda t: (t, 0)),
            pl.BlockSpec((None, tm, P), lambda t: (li, t, 0)),
            pl.BlockSpec((None, D, D), lambda t: (li, 0, 0)),
            pl.BlockSpec((None, P, D), lambda t: (li, 0, 0)),
        ],
        out_specs=pl.BlockSpec((tm, D), lambda t: (t, 0)),
        out_shape=jax.ShapeDtypeStruct((T, D), F32),
        compiler_params=_cparams("parallel"),
        name="ple_gate",
    )(x, p, wg, wu)


def _pool_body(u_ref, hist_ref, w_ref, sc_ref, o_ref, nh_ref, ext_ref, *, tl, start_pos):
    l = pl.program_id(1)
    nl = pl.num_programs(1)
    H = POOL_HIST + 1

    @pl.when(l == 0)
    def _():
        ext_ref[0:1, :] = jnp.zeros((1, ext_ref.shape[1]), F32)
        ext_ref[1:H, :] = hist_ref[...]

    @pl.when(l > 0)
    def _():
        ext_ref[0:H, :] = ext_ref[tl:tl + H, :]

    ext_ref[H:H + tl, :] = u_ref[...]

    pos = start_pos + l * tl + lax.broadcasted_iota(jnp.int32, (tl, 1), 0)
    gd = LANES
    for g, wnd in enumerate(POOL_WINDOWS):
        cs = slice(g * gd, (g + 1) * gd)
        s = ext_ref[H:H + tl, cs]
        cur = s
        for d in range(1, wnd):
            s = s + ext_ref[H - d:H - d + tl, cs]
        cnt = jnp.minimum(pos + 1, wnd).astype(F32)
        pooled = s / cnt - cur
        y = _dot(pooled.astype(BF16), w_ref[g]) * sc_ref[:, cs]
        o_ref[:, cs] = y.astype(o_ref.dtype)

    @pl.when(l == nl - 1)
    def _():
        nh_ref[...] = ext_ref[tl + 1:tl + H, :]


def _pool(u, hist, w, scale, start_pos, tl=512):
    B, L, PW = u.shape
    tl = _pick(L, tl)
    assert tl >= POOL_HIST + 1
    return pl.pallas_call(
        functools.partial(_pool_body, tl=tl, start_pos=start_pos),
        grid=(B, L // tl),
        in_specs=[
            pl.BlockSpec((None, tl, PW), lambda b, l: (b, l, 0)),
            pl.BlockSpec((None, POOL_HIST, PW), lambda b, l: (b, 0, 0)),
            pl.BlockSpec(w.shape, lambda b, l: (0, 0, 0)),
            pl.BlockSpec((1, PW), lambda b, l: (0, 0)),
        ],
        out_specs=[
            pl.BlockSpec((None, tl, PW), lambda b, l: (b, l, 0)),
            pl.BlockSpec((None, POOL_HIST, PW), lambda b, l: (b, 0, 0)),
        ],
        out_shape=[
            jax.ShapeDtypeStruct((B, L, PW), BF16),
            jax.ShapeDtypeStruct((B, POOL_HIST, PW), F32),
        ],
        scratch_shapes=[pltpu.VMEM((tl + POOL_HIST + 1, PW), F32)],
        compiler_params=_cparams("parallel", "arbitrary"),
        name="pool_mixer",
    )(u, hist, w, scale)


def _t5_bucket(rel):
    nb = N_BUCKETS // 2
    max_exact = nb // 2
    n = jnp.abs(rel)
    nf = jnp.maximum(n, 1).astype(jnp.float32)
    large = max_exact + (jnp.log(nf / max_exact) / math.log(MAX_DISTANCE / max_exact)
                         * (nb - max_exact)).astype(jnp.int32)
    large = jnp.minimum(large, nb - 1)
    return jnp.where(rel > 0, nb, 0) + jnp.where(n < max_exact, n, large)


def _attn_prep_body(tbl_ref, bkt_ref, lq1_ref, lk1_ref, lq2_ref, lk2_ref, bias_ref, lam_ref, *, tq, lam_init,
                    far_bucket):
    h = pl.program_id(0)
    bkt = bkt_ref[...]
    far = tbl_ref[far_bucket, h]
    acc = jnp.zeros(bkt.shape, F32)
    for b in range(N_BUCKETS):
        acc = jnp.where(bkt == b, tbl_ref[b, h] - far, acc)
    r = lax.broadcasted_iota(jnp.int32, bkt.shape, 1)
    c = lax.broadcasted_iota(jnp.int32, bkt.shape, 2)
    t = lax.broadcasted_iota(jnp.int32, bkt.shape, 0)
    visible = (t == 0) | ((c // CHUNK) <= (r // CHUNK))
    bias_ref[...] = jnp.where(visible, acc, NEG)
    e1 = jnp.exp(jnp.sum(lq1_ref[...] * lk1_ref[...], axis=-1, keepdims=True))
    e2 = jnp.exp(jnp.sum(lq2_ref[...] * lk2_ref[...], axis=-1, keepdims=True))
    lam_ref[...] = jnp.broadcast_to(e1 - e2 + lam_init, lam_ref.shape)


def _attn_prep(rel_bias, lq1, lk1, lq2, lk2, tq, lam_init):
    nbk, H = rel_bias.shape
    r = jnp.arange(tq, dtype=jnp.int32)[:, None]
    c = jnp.arange(tq, dtype=jnp.int32)[None, :]
    bkt = jnp.stack([_t5_bucket(c - r - tq), _t5_bucket(c - r)])
    far_bucket = N_BUCKETS // 2 - 1
    assert tq >= MAX_DISTANCE
    row = lambda a: a.reshape(1, -1).astype(F32)
    return pl.pallas_call(
        functools.partial(_attn_prep_body, tq=tq, lam_init=lam_init, far_bucket=far_bucket),
        grid=(H,),
        in_specs=[
            pl.BlockSpec(memory_space=pltpu.SMEM),
            pl.BlockSpec((2, tq, tq), lambda h: (0, 0, 0)),
            pl.BlockSpec((1, lq1.shape[-1]), lambda h: (0, 0)),
            pl.BlockSpec((1, lq1.shape[-1]), lambda h: (0, 0)),
            pl.BlockSpec((1, lq1.shape[-1]), lambda h: (0, 0)),
            pl.BlockSpec((1, lq1.shape[-1]), lambda h: (0, 0)),
        ],
        out_specs=[
            pl.BlockSpec((None, 2, tq, tq), lambda h: (h, 0, 0, 0)),
            pl.BlockSpec((8, LANES), lambda h: (0, 0)),
        ],
        out_shape=[
            jax.ShapeDtypeStruct((H, 2, tq, tq), F32),
            jax.ShapeDtypeStruct((8, LANES), F32),
        ],
        compiler_params=_cparams("arbitrary"),
        name="attn_prep",
    )(rel_bias.astype(F32), bkt, row(lq1), row(lk1), row(lq2), row(lk2))


def _split_q(q):
    lane = lax.broadcasted_iota(jnp.int32, q.shape, 1)
    half = q.shape[1] // 2
    zero = jnp.zeros_like(q)
    return jnp.concatenate([jnp.where(lane < half, q, zero), jnp.where(lane >= half, q, zero)], axis=0)


def _attend(qq, spans, s_ref, mx_ref, l_ref, acc_ref):
    rows = qq.shape[0]
    mx_ref[...] = jnp.full(mx_ref.shape, NEG, F32)
    for col, get_k, _, bias in spans:
        s = _dot_nt(qq, get_k())
        w = s.shape[1]
        if bias is not None:
            s = (s.reshape(2, rows // 2, w) + bias[None]).reshape(rows, w)
        s_ref[:, col:col + w] = s
        if w % LANES == 0:
            m = functools.reduce(jnp.maximum, [s[:, c:c + LANES] for c in range(0, w, LANES)])
            mx_ref[...] = jnp.maximum(mx_ref[...], m)
        else:
            mx_ref[:, 0:w] = jnp.maximum(mx_ref[:, 0:w], s)
    m_b = jnp.broadcast_to(jnp.max(mx_ref[...], axis=1, keepdims=True), mx_ref.shape)
    mx_ref[...] = m_b
    l_ref[...] = jnp.zeros_like(l_ref)
    acc_ref[...] = jnp.zeros_like(acc_ref)
    for col, get_k, get_v, _ in spans:
        w = get_v().shape[0]
        s = s_ref[:, col:col + w]
        m_b = mx_ref[...]
        if w % LANES == 0:
            ps = [jnp.exp(s[:, c:c + LANES] - m_b) for c in range(0, w, LANES)]
            l_ref[...] += functools.reduce(jnp.add, ps)
            p = ps[0] if len(ps) == 1 else jnp.concatenate(ps, axis=1)
        else:
            p = jnp.exp(s - m_b[:, 0:w])
            l_ref[:, 0:w] += p
        acc_ref[...] += _dot(p.astype(BF16), get_v())


def _attn_finish(lam_ref, g_ref, l_ref, acc_ref, tq, out_scale):
    lam = lam_ref[0:1, 0:1]
    o = acc_ref[...] / jnp.sum(l_ref[...], axis=1, keepdims=True)
    o = o[:tq] - lam * o[tq:]
    return o * lax.rsqrt(jnp.mean(o * o, axis=-1, keepdims=True) + EPS) * g_ref[...] * out_scale


ATTN_KEY_SPAN = 512
ATTN_SAMPLE_HEADS = 4


def _attn_prompt_body(q_ref, k_ref, v_ref, bias_ref, lam_ref, g_ref, o_ref, s_ref, mx_ref, l_ref, acc_ref, *,
                      tq, out_scale):
    L = q_ref.shape[0]

    def span(st, w, bias):
        return (st, lambda: k_ref[st:st + w, :], lambda: v_ref[st:st + w, :], bias)

    for qi in reversed(range(L // tq)):
        par = qi % 2
        far_end = max(qi - 1, 0) * tq
        spans = [span(st, min(ATTN_KEY_SPAN, far_end - st), None) for st in range(0, far_end, ATTN_KEY_SPAN)]
        if qi >= 1:
            spans.append(span((qi - 1) * tq, tq, bias_ref[0]))
        spans.append(span(qi * tq, tq, bias_ref[1]))
        qq = _split_q(q_ref[qi * tq:(qi + 1) * tq, :])
        _attend(qq, spans, s_ref.at[par], mx_ref.at[par], l_ref.at[par], acc_ref.at[par])
        y = _attn_finish(lam_ref, g_ref, l_ref.at[par], acc_ref.at[par], tq, out_scale)
        o_ref[qi * tq:(qi + 1) * tq, :] = y.astype(o_ref.dtype)


def _attn_prompt(q, k, v, bias, lam, g, out_scale):
    B, L, _ = q.shape
    H = q.shape[2] // LANES
    tq = bias.shape[-1]
    assert L % tq == 0 and tq % CHUNK == 0
    return pl.pallas_call(
        functools.partial(_attn_prompt_body, tq=tq, out_scale=out_scale),
        grid=(B, H),
        in_specs=[
            pl.BlockSpec((None, L, LANES), lambda b, h: (b, 0, h)),
            pl.BlockSpec((None, L, LANES), lambda b, h: (b, 0, h)),
            pl.BlockSpec((None, L, LANES), lambda b, h: (b, 0, h)),
            pl.BlockSpec((None, 2, tq, tq), lambda b, h: (h, 0, 0, 0)),
            pl.BlockSpec((8, LANES), lambda b, h: (0, 0)),
            pl.BlockSpec((1, LANES), lambda b, h: (0, 0)),
        ],
        out_specs=pl.BlockSpec((None, L, LANES), lambda b, h: (b, 0, h)),
        out_shape=jax.ShapeDtypeStruct((B, L, H * LANES), BF16),
        scratch_shapes=[
            pltpu.VMEM((2, 2 * tq, L), F32),
            pltpu.VMEM((2, 2 * tq, LANES), F32),
            pltpu.VMEM((2, 2 * tq, LANES), F32),
            pltpu.VMEM((2, 2 * tq, LANES), F32),
        ],
        compiler_params=_cparams("parallel", "parallel"),
        name="diff_attn_prompt",
    )(q, k, v, bias, lam, g)


def _attn_sample_body(q_ref, kn_ref, vn_ref, kc_ref, vc_ref, bprev_ref, bdiag_ref, lam_ref, g_ref, o_ref,
                      s_ref, mx_ref, l_ref, acc_ref, *, lq, tk, out_scale):
    hp, P = kc_ref.shape[0], kc_ref.shape[1]
    near = P - tk
    for i in range(hp):
        cs = slice(i * LANES, (i + 1) * LANES)

        def span(st, w, bias, i=i):
            return (st, lambda: kc_ref[i, st:st + w, :].astype(BF16),
                    lambda: vc_ref[i, st:st + w, :].astype(BF16), bias)

        spans = [span(st, min(ATTN_KEY_SPAN, near - st), None) for st in range(0, near, ATTN_KEY_SPAN)]
        spans.append(span(near, tk, bprev_ref[i]))
        spans.append((P, lambda cs=cs: kn_ref[:, cs], lambda cs=cs: vn_ref[:, cs], bdiag_ref[i, :, 0:lq]))
        _attend(_split_q(q_ref[:, cs]), spans, s_ref.at[i], mx_ref.at[i], l_ref.at[i], acc_ref.at[i])
        y = _attn_finish(lam_ref, g_ref, l_ref.at[i], acc_ref.at[i], lq, out_scale)
        o_ref[:, cs] = y.astype(o_ref.dtype)


def _attn_sample(q, k, v, k_cache, v_cache, bias, lam, g, out_scale):
    B, lq, _ = q.shape
    H = q.shape[2] // LANES
    P = k_cache.shape[2]
    tk = bias.shape[-1]
    assert lq == CHUNK and P % tk == 0 and P % CHUNK == 0 and lq <= tk
    hp = ATTN_SAMPLE_HEADS if H % ATTN_SAMPLE_HEADS == 0 else 1
    ng = H // hp
    return pl.pallas_call(
        functools.partial(_attn_sample_body, lq=lq, tk=tk, out_scale=out_scale),
        grid=(B, ng),
        in_specs=[
            pl.BlockSpec((None, lq, hp * LANES), lambda b, g: (b, 0, g)),
            pl.BlockSpec((None, lq, hp * LANES), lambda b, g: (b, 0, g)),
            pl.BlockSpec((None, lq, hp * LANES), lambda b, g: (b, 0, g)),
            pl.BlockSpec((None, hp, P, LANES), lambda b, g: (b, g, 0, 0)),
            pl.BlockSpec((None, hp, P, LANES), lambda b, g: (b, g, 0, 0)),
            pl.BlockSpec((hp, None, lq, tk), lambda b, g: (g, 0, 0, 0)),
            pl.BlockSpec((hp, None, lq, tk), lambda b, g: (g, 1, 0, 0)),
            pl.BlockSpec((8, LANES), lambda b, g: (0, 0)),
            pl.BlockSpec((1, LANES), lambda b, g: (0, 0)),
        ],
        out_specs=pl.BlockSpec((None, lq, hp * LANES), lambda b, g: (b, 0, g)),
        out_shape=jax.ShapeDtypeStruct((B, lq, H * LANES), BF16),
        scratch_shapes=[
            pltpu.VMEM((hp, 2 * lq, P + LANES), F32),
            pltpu.VMEM((hp, 2 * lq, LANES), F32),
            pltpu.VMEM((hp, 2 * lq, LANES), F32),
            pltpu.VMEM((hp, 2 * lq, LANES), F32),
        ],
        compiler_params=_cparams("parallel", "arbitrary"),
        name="diff_attn_sample",
    )(q, k, v, k_cache, v_cache, bias, bias, lam, g)


HGRN_PROJ_HEADS = 1
HGRN_HEADS_PER_STEP = 4
MIN_LOG2 = -150.0


def _hgrn_setup(lb_ref, lvl_ref, ck, layer):
    n_lev = ck.bit_length() - 1
    t_i = lax.broadcasted_iota(jnp.int32, (ck, ck), 0)
    s_i = lax.broadcasted_iota(jnp.int32, (ck, ck), 1)
    x = t_i ^ s_i
    hb = jnp.zeros((ck, ck), jnp.int32)
    for b in range(1, n_lev):
        hb = hb + (x >= (1 << b)).astype(jnp.int32)
    lvl_ref[...] = jnp.where(t_i > s_i, hb, -1)

    lg = lb_ref[...]
    e = jnp.exp(lg - jnp.max(lg, axis=0, keepdims=True))
    p = e / jnp.sum(e, axis=0, keepdims=True)
    cum = p[0:1]
    for d in range(1, layer + 1):
        cum = cum + p[d:d + 1]
    lb = cum - p[0:1]
    return lb, 1.0 - lb, lax.broadcasted_iota(jnp.int32, (ck, LANES), 0)


def _hgrn_chunk(hq, z, v, hg, consts, lvl_ref, ng_ref, st_ref):
    lb, one_mlb, row = consts
    ck = hq.shape[0]
    n_lev = ck.bit_length() - 1
    q = hq * jax.nn.sigmoid(hq)
    r = 1.0 / (1.0 + jnp.exp(z))
    k = one_mlb * r
    f = lb + one_mlb * (1.0 - r)
    g = jnp.maximum(jnp.log2(f), MIN_LOG2)
    vb = v.astype(BF16)

    pf = g
    tot = g
    lvl = lvl_ref[...]
    a = jnp.zeros((ck, ck), F32)
    for lev in range(n_lev):
        hsz = 1 << lev
        kl = (k if lev == 0 else k * jnp.exp2(tot - pf)).astype(BF16)
        if hsz % 8 == 0:
            nb = ck // (2 * hsz)
            late = lambda m: m.reshape(nb, 2, hsz, m.shape[-1])[:, 1:2]
            ql = (late(q) * jnp.exp2(late(pf))).astype(BF16).reshape(ck // 2, LANES)
            p_lev = _dot_nt(ql, kl).reshape(nb, 1, hsz, ck)
            a4 = a.reshape(nb, 2, hsz, ck)
            a = jnp.concatenate([a4[:, 0:1], jnp.where(late(lvl) == lev, p_lev, a4[:, 1:2])],
                                axis=1).reshape(ck, ck)
            t4 = tot.reshape(nb, 2, hsz, LANES)
            p4 = pf.reshape(nb, 2, hsz, LANES)
            both = t4[:, 0:1] + t4[:, 1:2]
            tot = jnp.concatenate([both, both], axis=1).reshape(ck, LANES)
            pf = jnp.concatenate([p4[:, 0:1], p4[:, 1:2] + t4[:, 0:1]], axis=1).reshape(ck, LANES)
        else:
            ql = (q * jnp.exp2(pf)).astype(BF16)
            a = jnp.where(lvl == lev, _dot_nt(ql, kl), a)
            second = (row & hsz) != 0
            up = pltpu.roll(tot, hsz, 0)
            dn = pltpu.roll(tot, ck - hsz, 0)
            pf = pf + jnp.where(second, up, 0.0)
            tot = tot + jnp.where(second, up, dn)
    st = st_ref[...]
    o = _dot(a.astype(BF16), vb) + jnp.sum(q * k, axis=1, keepdims=True) * v
    o = o + _dot_nt((q * jnp.exp2(pf)).astype(BF16), st.astype(BF16))
    kst = (k * jnp.exp2(tot - pf)).astype(BF16)
    st_ref[...] = st * jnp.exp2(tot[0:1, :]) + _dot(v.T.astype(BF16), kst)
    y = o * lax.rsqrt(jnp.mean(o * o, axis=-1, keepdims=True) + EPS) * ng_ref[...]
    return y * (hg * jax.nn.sigmoid(hg))


def _hgrn_body(hq_ref, hz_ref, hi_ref, hg_ref, lb_ref, ng_ref, s0_ref, o_ref, s_ref, st_ref, lvl_ref, *, ck,
               layer):
    l = pl.program_id(2)
    tl = hq_ref.shape[0]
    hp = st_ref.shape[0]

    @pl.when(l == 0)
    def _():
        for i in range(hp):
            st_ref[i] = s0_ref[i].T

    lb, one_mlb, row = _hgrn_setup(lb_ref, lvl_ref, ck, layer)

    def chunk(c, carry):
        rs = slice(c * ck, (c + 1) * ck) if isinstance(c, int) else pl.ds(pl.multiple_of(c * ck, ck), ck)
        for i in range(hp):
            cs = slice(i * LANES, (i + 1) * LANES)
            y = _hgrn_chunk(hq_ref[rs, cs], hz_ref[rs, cs], hi_ref[rs, cs], hg_ref[rs, cs],
                            (lb[:, cs], one_mlb[:, cs], row), lvl_ref, ng_ref, st_ref.at[i])
            o_ref[rs, cs] = y.astype(o_ref.dtype)
        return carry

    n_chunks = tl // ck
    if n_chunks == 1:
        chunk(0, 0)
    else:
        lax.fori_loop(0, n_chunks, chunk, 0, unroll=4 if n_chunks % 4 == 0 else 1)

    @pl.when(l == pl.num_programs(2) - 1)
    def _():
        for i in range(hp):
            s_ref[i] = st_ref[i].T


def _hgrn(h, lb_logits, norm_g, s0, layer, n_heads, tl=2048, ck=128):
    B, L, _ = h.shape
    H = n_heads
    tl = _pick(L, tl)
    ck = _pick(tl, ck)
    assert ck & (ck - 1) == 0 and ck >= 8
    depth = lb_logits.shape[0]
    hp = HGRN_HEADS_PER_STEP if (L <= LANES and H % HGRN_HEADS_PER_STEP == 0) else 1
    ng = H // hp
    blk = lambda part: pl.BlockSpec((None, tl, hp * LANES), lambda b, g, l: (b, l, part * ng + g))
    return pl.pallas_call(
        functools.partial(_hgrn_body, ck=ck, layer=layer),
        grid=(B, ng, L // tl),
        in_specs=[
            blk(0), blk(1), blk(2), blk(3),
            pl.BlockSpec((depth, hp * LANES), lambda b, g, l: (0, g)),
            pl.BlockSpec((1, LANES), lambda b, g, l: (0, 0)),
            pl.BlockSpec((None, hp, LANES, LANES), lambda b, g, l: (b, g, 0, 0)),
        ],
        out_specs=[
            pl.BlockSpec((None, tl, hp * LANES), lambda b, g, l: (b, l, g)),
            pl.BlockSpec((None, hp, LANES, LANES), lambda b, g, l: (b, g, 0, 0)),
        ],
        out_shape=[
            jax.ShapeDtypeStruct((B, L, H * LANES), BF16),
            jax.ShapeDtypeStruct((B, H, LANES, LANES), F32),
        ],
        scratch_shapes=[pltpu.VMEM((hp, LANES, LANES), F32), pltpu.VMEM((ck, ck), jnp.int32)],
        compiler_params=_cparams("parallel", "parallel", "arbitrary"),
        name="hgrn_scan",
    )(h, h, h, h, lb_logits, norm_g, s0)


def _hgrn_proj_body(x_ref, w_ref, lb_ref, ng_ref, s0_ref, o_ref, s_ref, st_ref, lvl_ref, h_ref, *, ck, layer,
                    rows):
    L = x_ref.shape[0]
    hp = w_ref.shape[0]
    lb, one_mlb, row = _hgrn_setup(lb_ref, lvl_ref, ck, layer)
    n_proj = L // rows

    def project(i, s):
        h_ref[i, s % 2] = _dot(x_ref[s * rows:(s + 1) * rows, :], w_ref[i])

    project(0, 0)
    for i in range(hp):
        cs = slice(i * LANES, (i + 1) * LANES)
        consts = (lb[:, cs], one_mlb[:, cs], row)
        st = st_ref.at[i]
        st[...] = s0_ref[i].T
        for s in range(n_proj):
            if s + 1 < n_proj:
                project(i, s + 1)
            elif i + 1 < hp:
                project(i + 1, 0)
            h = h_ref.at[i, s % 2]
            for c in range(rows // ck):
                rs = slice(c * ck, (c + 1) * ck)
                y = _hgrn_chunk(h[rs, 0:LANES], h[rs, LANES:2 * LANES], h[rs, 2 * LANES:3 * LANES],
                                h[rs, 3 * LANES:4 * LANES], consts, lvl_ref, ng_ref, st)
                o_ref[s * rows + c * ck:s * rows + (c + 1) * ck, cs] = y.astype(o_ref.dtype)
        s_ref[i] = st[...].T


def _hgrn_proj(x, w_heads, lb_logits, norm_g, s0, layer, ck=128, rows=512):
    B, L, D = x.shape
    H = w_heads.shape[0]
    assert L % rows == 0 and rows % ck == 0 and ck & (ck - 1) == 0
    depth = lb_logits.shape[0]
    hp = HGRN_PROJ_HEADS if H % HGRN_PROJ_HEADS == 0 else 1
    return pl.pallas_call(
        functools.partial(_hgrn_proj_body, ck=ck, layer=layer, rows=rows),
        grid=(B, H // hp),
        in_specs=[
            pl.BlockSpec((None, L, D), lambda b, g: (b, 0, 0)),
            pl.BlockSpec((hp, D, 4 * LANES), lambda b, g: (g, 0, 0)),
            pl.BlockSpec((depth, hp * LANES), lambda b, g: (0, g)),
            pl.BlockSpec((1, LANES), lambda b, g: (0, 0)),
            pl.BlockSpec((None, hp, LANES, LANES), lambda b, g: (b, g, 0, 0)),
        ],
        out_specs=[
            pl.BlockSpec((None, L, hp * LANES), lambda b, g: (b, 0, g)),
            pl.BlockSpec((None, hp, LANES, LANES), lambda b, g: (b, g, 0, 0)),
        ],
        out_shape=[
            jax.ShapeDtypeStruct((B, L, H * LANES), BF16),
            jax.ShapeDtypeStruct((B, H, LANES, LANES), F32),
        ],
        scratch_shapes=[
            pltpu.VMEM((hp, LANES, LANES), F32),
            pltpu.VMEM((ck, ck), jnp.int32),
            pltpu.VMEM((hp, 2, rows, 4 * LANES), F32),
        ],
        compiler_params=_cparams("parallel", "arbitrary"),
        name="hgrn_proj_scan",
    )(x, w_heads, lb_logits, norm_g, s0)


def _trunk(x, p, k_cache, v_cache, pool_hist, hg_state, W, attn_prep):
    B, L, D = x.shape
    depth = W["ln_g"].shape[0]
    alpha = (2 * depth) ** 0.25
    T = B * L
    xt = x.reshape(T, D)
    new_k, new_v, new_pool, new_s = [], [], [], []
    pool_width = W["pool_scale"].shape[-1]
    da_width = D - pool_width
    n_da_heads = da_width // LANES
    n_hg_heads = D // LANES
    lnrow = lambda a, i, s: a[i, s].reshape(1, D)
    for i in range(depth):
        fuse_proj = i % 2 == 1 and L % HGRN_PROJ_ROWS == 0
        want16 = fuse_proj or i % 2 == 0
        xt = _ffn_ln(xt, W["wg"], W["wu"], W["wd"], lnrow(W["ln_g"], i, 0), lnrow(W["ln_b"], i, 0), i, 0, alpha,
                     also_bf16=want16)
        if want16:
            xt, xt16 = xt
        if i % 2 == 0:
            e = i // 2
            w_in = W["w_in_even"]
            (u,) = _proj(xt16, w_in, e, 0, pool_width, B, L, ("f32",))
            (q16,) = _proj(xt16, w_in, e, pool_width, da_width, B, L, ("bf16",), scale=(LANES // 2) ** -0.5)
            k, k16 = _proj(xt16, w_in, e, pool_width + da_width, da_width, B, L, ("heads", "bf16"))
            v, v16 = _proj(xt16, w_in, e, pool_width + 2 * da_width, da_width, B, L, ("heads", "bf16"))
            past = 0 if k_cache is None else k_cache.shape[2]
            pool_out, nh = _pool(u.reshape(B, L, pool_width), pool_hist[e], W["pool_w"][e],
                                 W["pool_scale"][e].reshape(1, pool_width), past)
            bias, lam = attn_prep[e]
            lam_init = 0.8 - 0.6 * math.exp(-0.3 * i)
            g = W["diff_norm_g"][e].reshape(1, LANES)
            q3, k3, v3 = (a.reshape(B, L, da_width) for a in (q16, k16, v16))
            if k_cache is None:
                o = _attn_prompt(q3, k3, v3, bias, lam, g, 1.0 - lam_init)
            else:
                kc = jnp.transpose(k_cache[e], (0, 2, 1, 3))
                vc = jnp.transpose(v_cache[e], (0, 2, 1, 3))
                o = _attn_sample(q3, k3, v3, kc, vc, bias, lam, g, 1.0 - lam_init)
            parts = [pool_out.reshape(T, pool_width), o.reshape(T, da_width)]
            w_out = W["w_out_even"]
            new_k.append(jnp.transpose(k, (0, 2, 1, 3)))
            new_v.append(jnp.transpose(v, (0, 2, 1, 3)))
            new_pool.append(nh)
            li = e
        else:
            od = i // 2
            ng = W["hgrn_norm_g"][od].reshape(1, LANES)
            if fuse_proj:
                o, s = _hgrn_proj(xt16.reshape(B, L, D), W["w_in_odd_heads"][od], W["lb_logits"], ng,
                                  hg_state[od], i, rows=HGRN_PROJ_ROWS)
            else:
                h = _matmul(xt, W["w_in_odd"], od)
                o, s = _hgrn(h.reshape(B, L, 4 * D), W["lb_logits"], ng, hg_state[od], i, n_hg_heads)
            parts = [o.reshape(T, D)]
            w_out = W["w_out_odd"]
            new_s.append(s)
            li = od
        xt = _out_ln(parts, xt, w_out, li, lnrow(W["ln_g"], i, 1), lnrow(W["ln_b"], i, 1), alpha)
        xt = _ffn_ln(xt, W["wg"], W["wu"], W["wd"], lnrow(W["ln_g"], i, 2), lnrow(W["ln_b"], i, 2), i, 1, alpha)
        xt = _ple(xt, p.reshape(depth, T, -1), W["w_ple_gate"], W["w_ple_up"], i)
    return xt.reshape(B, L, D), jnp.stack(new_k), jnp.stack(new_v), jnp.stack(new_pool), jnp.stack(new_s)


ATTN_TILE = 256
HGRN_PROJ_ROWS = 512


def kernel(x_prompt, x_sample, cache_diff_k, cache_diff_v, state_pool, state_hgrn, p_prompt, p_sample, ln_g, ln_b, w_ffn_gate, w_ffn_up, w_ffn_down, w_ple_gate, w_ple_up, w_in_even, w_out_even, pool_w, pool_scale, lam_q1, lam_k1, lam_q2, lam_k2, diff_norm_g, rel_bias, w_in_odd, w_out_odd, hgrn_norm_g, hgrn_lb_logits):
    bf = lambda a: a.astype(BF16)
    W = dict(
        ln_g=ln_g.astype(F32), ln_b=ln_b.astype(F32),
        wg=bf(w_ffn_gate), wu=bf(w_ffn_up), wd=bf(w_ffn_down),
        w_ple_gate=bf(w_ple_gate), w_ple_up=bf(w_ple_up),
        w_in_even=bf(w_in_even), w_out_even=bf(w_out_even),
        pool_w=bf(pool_w), pool_scale=pool_scale.astype(F32),
        diff_norm_g=diff_norm_g.astype(F32),
        w_in_odd=bf(w_in_odd), w_out_odd=bf(w_out_odd),
        hgrn_norm_g=hgrn_norm_g.astype(F32), lb_logits=hgrn_lb_logits.astype(F32),
    )
    n_even = w_in_even.shape[0]
    n_odd = w_in_odd.shape[0]
    d_model = w_in_odd.shape[1]
    n_hg = d_model // LANES
    W["w_in_odd_heads"] = jnp.transpose(W["w_in_odd"].reshape(n_odd, d_model, 4, n_hg, LANES),
                                        (0, 3, 1, 2, 4)).reshape(n_odd, n_hg, d_model, 4 * LANES)
    attn_prep = []
    for e in range(n_even):
        lam_init = 0.8 - 0.6 * math.exp(-0.3 * (2 * e))
        attn_prep.append(_attn_prep(rel_bias, lam_q1[e], lam_k1[e], lam_q2[e], lam_k2[e], ATTN_TILE, lam_init))

    B = x_prompt.shape[0]
    dt = x_prompt.dtype
    zero_pool = jnp.zeros((n_even, B) + state_pool.shape[2:], dt)
    zero_s = jnp.zeros((n_odd, B) + state_hgrn.shape[2:], dt)
    y_p, k_p, v_p, pool_p, s_p = _trunk(x_prompt, p_prompt, None, None, zero_pool, zero_s, W, attn_prep)
    y_s, k_s, v_s, pool_s, s_s = _trunk(x_sample, p_sample, cache_diff_k, cache_diff_v, state_pool, state_hgrn,
                                        W, attn_prep)
    return (y_p, y_s, k_p, v_p, k_s, v_s, pool_p, pool_s, s_p, s_s)
```

```python
import functools
import math

import jax
import jax.numpy as jnp
from jax import lax
from jax.experimental import pallas as pl
from jax.experimental.pallas import tpu as pltpu

F32 = jnp.float32
BF16 = jnp.bfloat16

CHUNK = 64
POOL_WINDOWS = (2, 4, 8, 16)
POOL_HIST = max(POOL_WINDOWS) - 1
N_BUCKETS = 32
MAX_DISTANCE = 128
EPS = 1e-5
NEG = -1e30
LANES = 128
SUBLANES = 8

VMEM_LIMIT = 56 * 1024 * 1024


def _cparams(*sem):
    return pltpu.CompilerParams(dimension_semantics=sem, vmem_limit_bytes=VMEM_LIMIT)


def _pick(n, pref):
    if n <= pref:
        return n
    t = pref
    while n % t:
        t //= 2
    return t


def _layer_norm(y, g, b):
    mu = jnp.mean(y, axis=-1, keepdims=True)
    d = y - mu
    var = jnp.mean(d * d, axis=-1, keepdims=True)
    return d * lax.rsqrt(var + EPS) * g + b


def _dot(a, b):
    return jnp.dot(a, b, preferred_element_type=F32)


def _dot_nt(a, b):
    return lax.dot_general(a, b, (((1,), (1,)), ((), ())), preferred_element_type=F32)


FFN_COL_CHUNK = 512
MXU_COLS = 2 * LANES


def _ffn_act_body(x_ref, wg_ref, wu_ref, a_ref, xb_ref):
    @pl.when(pl.program_id(1) == 0)
    def _():
        xb_ref[...] = x_ref[...].astype(BF16)

    def gated(hg, hu):
        return (hg * jax.nn.sigmoid(hg) * hu).astype(a_ref.dtype)

    xb = xb_ref[...]
    tf = a_ref.shape[1]
    main = tf // MXU_COLS * MXU_COLS
    for c0 in range(0, main, FFN_COL_CHUNK):
        cs = slice(c0, min(c0 + FFN_COL_CHUNK, main))
        a_ref[:, cs] = gated(_dot(xb, wg_ref[:, cs]), _dot(xb, wu_ref[:, cs]))
    if main < tf:
        assert tf - main == LANES
        h = _dot(xb, jnp.concatenate([wg_ref[:, main:tf], wu_ref[:, main:tf]], axis=1))
        a_ref[:, main:tf] = gated(h[:, :LANES], h[:, LANES:])


def _ffn_down_ln_body(a_ref, x_ref, wd_ref, g_ref, b_ref, *o_refs, alpha):
    tm = x_ref.shape[0]
    hm = tm // 2
    for rows in (slice(0, hm), slice(hm, tm)):
        y = alpha * x_ref[rows, :] + 0.5 * _dot(a_ref[rows, :], wd_ref[...])
        y = _layer_norm(y, g_ref[...], b_ref[...])
        for o_ref in o_refs:
            o_ref[rows, :] = y.astype(o_ref.dtype)


def _ffn_ln(x, wg, wu, wd, g, b, li, si, alpha, also_bf16=False, tm_act=1024, tf=1408, tm_down=256):
    T, D = x.shape
    F = wd.shape[-2]
    tm = _pick(T, tm_act)
    tf = tf if F % tf == 0 else _pick(F, 512)
    act = pl.pallas_call(
        _ffn_act_body,
        grid=(T // tm, F // tf),
        in_specs=[
            pl.BlockSpec((tm, D), lambda t, j: (t, 0)),
            pl.BlockSpec((None, None, D, tf), lambda t, j: (li, si, 0, j)),
            pl.BlockSpec((None, None, D, tf), lambda t, j: (li, si, 0, j)),
        ],
        out_specs=pl.BlockSpec((tm, tf), lambda t, j: (t, j)),
        out_shape=jax.ShapeDtypeStruct((T, F), BF16),
        scratch_shapes=[pltpu.VMEM((tm, D), BF16)],
        compiler_params=_cparams("parallel", "arbitrary"),
        name="ffn_act",
    )(x, wg, wu)

    tm = _pick(T, tm_down)
    out_dtypes = (F32, BF16) if also_bf16 else (F32,)
    outs = pl.pallas_call(
        functools.partial(_ffn_down_ln_body, alpha=alpha),
        grid=(T // tm,),
        in_specs=[
            pl.BlockSpec((tm, F), lambda t: (t, 0)),
            pl.BlockSpec((tm, D), lambda t: (t, 0)),
            pl.BlockSpec((None, None, F, D), lambda t: (li, si, 0, 0), pipeline_mode=pl.Buffered(1)),
            pl.BlockSpec((1, D), lambda t: (0, 0)),
            pl.BlockSpec((1, D), lambda t: (0, 0)),
        ],
        out_specs=[pl.BlockSpec((tm, D), lambda t: (t, 0)) for _ in out_dtypes],
        out_shape=[jax.ShapeDtypeStruct((T, D), dt) for dt in out_dtypes],
        compiler_params=_cparams("parallel"),
        name="ffn_down_ln",
    )(act, x, wd, g, b)
    return tuple(outs) if also_bf16 else outs[0]


def _matmul_body(x_ref, w_ref, o_ref, xb_ref):
    @pl.when(pl.program_id(1) == 0)
    def _():
        xb_ref[...] = x_ref[...].astype(BF16)

    o_ref[...] = _dot(xb_ref[...], w_ref[...]).astype(o_ref.dtype)


def _matmul(x, w, li, out_dtype=F32, tm=1024, tn=1024):
    T, K = x.shape
    N = w.shape[-1]
    tm = _pick(T, tm)
    tn = _pick(N, tn)
    return pl.pallas_call(
        _matmul_body,
        grid=(T // tm, N // tn),
        in_specs=[
            pl.BlockSpec((tm, K), lambda t, j: (t, 0)),
            pl.BlockSpec((None, K, tn), lambda t, j: (li, 0, j)),
        ],
        out_specs=pl.BlockSpec((tm, tn), lambda t, j: (t, j)),
        out_shape=jax.ShapeDtypeStruct((T, N), out_dtype),
        scratch_shapes=[pltpu.VMEM((tm, K), BF16)],
        compiler_params=_cparams("parallel", "arbitrary"),
        name="in_proj",
    )(x, w)


def _proj_body(x_ref, w_ref, *o_refs, scale):
    r = _dot(x_ref[...], w_ref[...])
    for o_ref in o_refs:
        if len(o_ref.shape) == 4:
            bb, hpt, tl, _ = o_ref.shape
            for hh in range(hpt):
                o_ref[:, hh] = r[:, hh * LANES:(hh + 1) * LANES].reshape(bb, tl, LANES)
        elif o_ref.dtype == BF16:
            o_ref[...] = (r * scale).astype(BF16)
        else:
            o_ref[...] = r


def _proj(x16, w, li, col0, width, B, L, kinds, scale=1.0, tm=2048, tn=512):
    T, K = x16.shape
    tm = _pick(T, tm)
    assert width % tn == 0 and col0 % tn == 0 and tn % LANES == 0
    hpt = tn // LANES
    tl = min(L, tm)
    assert tm % tl == 0 and L % tl == 0
    bb, nl = tm // tl, L // tl
    specs, shapes = [], []
    for kind in kinds:
        if kind == "heads":
            specs.append(pl.BlockSpec((bb, hpt, tl, LANES), lambda t, j: (t // nl, j, t % nl, 0)))
            shapes.append(jax.ShapeDtypeStruct((B, width // LANES, L, LANES), F32))
        else:
            specs.append(pl.BlockSpec((tm, tn), lambda t, j: (t, j)))
            shapes.append(jax.ShapeDtypeStruct((T, width), BF16 if kind == "bf16" else F32))
    return pl.pallas_call(
        functools.partial(_proj_body, scale=scale),
        grid=(T // tm, width // tn),
        in_specs=[
            pl.BlockSpec((tm, K), lambda t, j: (t, 0)),
            pl.BlockSpec((None, K, tn), lambda t, j: (li, 0, col0 // tn + j)),
        ],
        out_specs=specs,
        out_shape=shapes,
        compiler_params=_cparams("parallel", "arbitrary"),
        name="in_proj_cols",
    )(x16, w)


def _out_ln_body(*refs, alpha, widths):
    n = len(widths)
    parts = refs[:n]
    x_ref, w_ref, g_ref, b_ref, o_ref = refs[n:]
    tm = x_ref.shape[0]
    n_split = 2 if tm % 16 == 0 else 1
    hm = tm // n_split
    for h in range(n_split):
        rows = slice(h * hm, (h + 1) * hm)
        acc = alpha * x_ref[rows, :]
        off = 0
        for p_ref, wd in zip(parts, widths):
            acc = acc + _dot(p_ref[rows, :], w_ref[off:off + wd, :])
            off += wd
        o_ref[rows, :] = _layer_norm(acc, g_ref[...], b_ref[...])


def _out_ln(parts, x, w, li, g, b, alpha, tm=512):
    T, D = x.shape
    tm = _pick(T, tm)
    widths = tuple(p.shape[1] for p in parts)
    kin = sum(widths)
    return pl.pallas_call(
        functools.partial(_out_ln_body, alpha=alpha, widths=widths),
        grid=(T // tm,),
        in_specs=[pl.BlockSpec((tm, wd), lambda t: (t, 0)) for wd in widths] + [
            pl.BlockSpec((tm, D), lambda t: (t, 0)),
            pl.BlockSpec((None, kin, D), lambda t: (li, 0, 0)),
            pl.BlockSpec((1, D), lambda t: (0, 0)),
            pl.BlockSpec((1, D), lambda t: (0, 0)),
        ],
        out_specs=pl.BlockSpec((tm, D), lambda t: (t, 0)),
        out_shape=jax.ShapeDtypeStruct((T, D), F32),
        compiler_params=_cparams("parallel"),
        name="out_proj_ln",
    )(*parts, x, w, g, b)


def _ple_body(x_ref, p_ref, wg_ref, wu_ref, o_ref, *, tn):
    xb = x_ref[...].astype(BF16)
    pb = p_ref[...].astype(BF16)
    D = o_ref.shape[1]
    for c in range(D // tn):
        sl = slice(c * tn, (c + 1) * tn)
        gate = jax.nn.sigmoid(_dot(xb, wg_ref[:, sl]))
        up = _dot(pb, wu_ref[:, sl])
        o_ref[:, sl] = x_ref[:, sl] + gate * up


def _ple(x, p, wg, wu, li, tm=512, tn=512):
    T, D = x.shape
    P = p.shape[2]
    tm = _pick(T, tm)
    return pl.pallas_call(
        functools.partial(_ple_body, tn=_pick(D, tn)),
        grid=(T // tm,),
        in_specs=[
            pl.BlockSpec((tm, D), lambda t: (t, 0)),
            pl.BlockSpec((None, tm, P), lambda t: (li, t, 0)),
            pl.BlockSpec((None, D, D), lambda t: (li, 0, 0)),
            pl.BlockSpec((None, P, D), lambda t: (li, 0, 0)),
        ],
        out_specs=pl.BlockSpec((tm, D), lambda t: (t, 0)),
        out_shape=jax.ShapeDtypeStruct((T, D), F32),
        compiler_params=_cparams("parallel"),
        name="ple_gate",
    )(x, p, wg, wu)


def _pool_body(u_ref, hist_ref, w_ref, sc_ref, o_ref, nh_ref, ext_ref, *, tl, start_pos):
    l = pl.program_id(1)
    nl = pl.num_programs(1)
    H = POOL_HIST + 1

    @pl.when(l == 0)
    def _():
        ext_ref[0:1, :] = jnp.zeros((1, ext_ref.shape[1]), F32)
        ext_ref[1:H, :] = hist_ref[...]

    @pl.when(l > 0)
    def _():
        ext_ref[0:H, :] = ext_ref[tl:tl + H, :]

    ext_ref[H:H + tl, :] = u_ref[...]

    pos = start_pos + l * tl + lax.broadcasted_iota(jnp.int32, (tl, 1), 0)
    gd = LANES
    for g, wnd in enumerate(POOL_WINDOWS):
        cs = slice(g * gd, (g + 1) * gd)
        s = ext_ref[H:H + tl, cs]
        cur = s
        for d in range(1, wnd):
            s = s + ext_ref[H - d:H - d + tl, cs]
        cnt = jnp.minimum(pos + 1, wnd).astype(F32)
        pooled = s / cnt - cur
        y = _dot(pooled.astype(BF16), w_ref[g]) * sc_ref[:, cs]
        o_ref[:, cs] = y.astype(o_ref.dtype)

    @pl.when(l == nl - 1)
    def _():
        nh_ref[...] = ext_ref[tl + 1:tl + H, :]


def _pool(u, hist, w, scale, start_pos, tl=512):
    B, L, PW = u.shape
    tl = _pick(L, tl)
    assert tl >= POOL_HIST + 1
    return pl.pallas_call(
        functools.partial(_pool_body, tl=tl, start_pos=start_pos),
        grid=(B, L // tl),
        in_specs=[
            pl.BlockSpec((None, tl, PW), lambda b, l: (b, l, 0)),
            pl.BlockSpec((None, POOL_HIST, PW), lambda b, l: (b, 0, 0)),
            pl.BlockSpec(w.shape, lambda b, l: (0, 0, 0)),
            pl.BlockSpec((1, PW), lambda b, l: (0, 0)),
        ],
        out_specs=[
            pl.BlockSpec((None, tl, PW), lambda b, l: (b, l, 0)),
            pl.BlockSpec((None, POOL_HIST, PW), lambda b, l: (b, 0, 0)),
        ],
        out_shape=[
            jax.ShapeDtypeStruct((B, L, PW), BF16),
            jax.ShapeDtypeStruct((B, POOL_HIST, PW), F32),
        ],
        scratch_shapes=[pltpu.VMEM((tl + POOL_HIST + 1, PW), F32)],
        compiler_params=_cparams("parallel", "arbitrary"),
        name="pool_mixer",
    )(u, hist, w, scale)


def _t5_bucket(rel):
    nb = N_BUCKETS // 2
    max_exact = nb // 2
    n = jnp.abs(rel)
    nf = jnp.maximum(n, 1).astype(jnp.float32)
    large = max_exact + (jnp.log(nf / max_exact) / math.log(MAX_DISTANCE / max_exact)
                         * (nb - max_exact)).astype(jnp.int32)
    large = jnp.minimum(large, nb - 1)
    return jnp.where(rel > 0, nb, 0) + jnp.where(n < max_exact, n, large)


def _attn_prep_body(tbl_ref, bkt_ref, lq1_ref, lk1_ref, lq2_ref, lk2_ref, bias_ref, lam_ref, *, tq, lam_init,
                    far_bucket):
    h = pl.program_id(0)
    bkt = bkt_ref[...]
    far = tbl_ref[far_bucket, h]
    acc = jnp.zeros(bkt.shape, F32)
    for b in range(N_BUCKETS):
        acc = jnp.where(bkt == b, tbl_ref[b, h] - far, acc)
    r = lax.broadcasted_iota(jnp.int32, bkt.shape, 1)
    c = lax.broadcasted_iota(jnp.int32, bkt.shape, 2)
    t = lax.broadcasted_iota(jnp.int32, bkt.shape, 0)
    visible = (t == 0) | ((c // CHUNK) <= (r // CHUNK))
    bias_ref[...] = jnp.where(visible, acc, NEG)
    e1 = jnp.exp(jnp.sum(lq1_ref[...] * lk1_ref[...], axis=-1, keepdims=True))
    e2 = jnp.exp(jnp.sum(lq2_ref[...] * lk2_ref[...], axis=-1, keepdims=True))
    lam_ref[...] = jnp.broadcast_to(e1 - e2 + lam_init, lam_ref.shape)


def _attn_prep(rel_bias, lq1, lk1, lq2, lk2, tq, lam_init):
    nbk, H = rel_bias.shape
    r = jnp.arange(tq, dtype=jnp.int32)[:, None]
    c = jnp.arange(tq, dtype=jnp.int32)[None, :]
    bkt = jnp.stack([_t5_bucket(c - r - tq), _t5_bucket(c - r)])
    far_bucket = N_BUCKETS // 2 - 1
    assert tq >= MAX_DISTANCE
    row = lambda a: a.reshape(1, -1).astype(F32)
    return pl.pallas_call(
        functools.partial(_attn_prep_body, tq=tq, lam_init=lam_init, far_bucket=far_bucket),
        grid=(H,),
        in_specs=[
            pl.BlockSpec(memory_space=pltpu.SMEM),
            pl.BlockSpec((2, tq, tq), lambda h: (0, 0, 0)),
            pl.BlockSpec((1, lq1.shape[-1]), lambda h: (0, 0)),
            pl.BlockSpec((1, lq1.shape[-1]), lambda h: (0, 0)),
            pl.BlockSpec((1, lq1.shape[-1]), lambda h: (0, 0)),
            pl.BlockSpec((1, lq1.shape[-1]), lambda h: (0, 0)),
        ],
        out_specs=[
            pl.BlockSpec((None, 2, tq, tq), lambda h: (h, 0, 0, 0)),
            pl.BlockSpec((8, LANES), lambda h: (0, 0)),
        ],
        out_shape=[
            jax.ShapeDtypeStruct((H, 2, tq, tq), F32),
            jax.ShapeDtypeStruct((8, LANES), F32),
        ],
        compiler_params=_cparams("arbitrary"),
        name="attn_prep",
    )(rel_bias.astype(F32), bkt, row(lq1), row(lk1), row(lq2), row(lk2))


def _split_q(q):
    lane = lax.broadcasted_iota(jnp.int32, q.shape, 1)
    half = q.shape[1] // 2
    zero = jnp.zeros_like(q)
    return jnp.concatenate([jnp.where(lane < half, q, zero), jnp.where(lane >= half, q, zero)], axis=0)


def _attend(qq, spans, s_ref, mx_ref, l_ref, acc_ref):
    rows = qq.shape[0]
    mx_ref[...] = jnp.full(mx_ref.shape, NEG, F32)
    for col, get_k, _, bias in spans:
        s = _dot_nt(qq, get_k())
        w = s.shape[1]
        if bias is not None:
            s = (s.reshape(2, rows // 2, w) + bias[None]).reshape(rows, w)
        s_ref[:, col:col + w] = s
        if w % LANES == 0:
            m = functools.reduce(jnp.maximum, [s[:, c:c + LANES] for c in range(0, w, LANES)])
            mx_ref[...] = jnp.maximum(mx_ref[...], m)
        else:
            mx_ref[:, 0:w] = jnp.maximum(mx_ref[:, 0:w], s)
    m_b = jnp.broadcast_to(jnp.max(mx_ref[...], axis=1, keepdims=True), mx_ref.shape)
    mx_ref[...] = m_b
    l_ref[...] = jnp.zeros_like(l_ref)
    acc_ref[...] = jnp.zeros_like(acc_ref)
    for col, get_k, get_v, _ in spans:
        w = get_v().shape[0]
        s = s_ref[:, col:col + w]
        m_b = mx_ref[...]
        if w % LANES == 0:
            ps = [jnp.exp(s[:, c:c + LANES] - m_b) for c in range(0, w, LANES)]
            l_ref[...] += functools.reduce(jnp.add, ps)
            p = ps[0] if len(ps) == 1 else jnp.concatenate(ps, axis=1)
        else:
            p = jnp.exp(s - m_b[:, 0:w])
            l_ref[:, 0:w] += p
        acc_ref[...] += _dot(p.astype(BF16), get_v())


def _attn_finish(lam_ref, g_ref, l_ref, acc_ref, tq, out_scale):
    lam = lam_ref[0:1, 0:1]
    o = acc_ref[...] / jnp.sum(l_ref[...], axis=1, keepdims=True)
    o = o[:tq] - lam * o[tq:]
    return o * lax.rsqrt(jnp.mean(o * o, axis=-1, keepdims=True) + EPS) * g_ref[...] * out_scale


ATTN_KEY_SPAN = 512
ATTN_SAMPLE_HEADS = 4


def _attn_prompt_body(q_ref, k_ref, v_ref, bias_ref, lam_ref, g_ref, o_ref, s_ref, mx_ref, l_ref, acc_ref, *,
                      tq, out_scale):
    L = q_ref.shape[0]

    def span(st, w, bias):
        return (st, lambda: k_ref[st:st + w, :], lambda: v_ref[st:st + w, :], bias)

    for qi in reversed(range(L // tq)):
        par = qi % 2
        far_end = max(qi - 1, 0) * tq
        spans = [span(st, min(ATTN_KEY_SPAN, far_end - st), None) for st in range(0, far_end, ATTN_KEY_SPAN)]
        if qi >= 1:
            spans.append(span((qi - 1) * tq, tq, bias_ref[0]))
        spans.append(span(qi * tq, tq, bias_ref[1]))
        qq = _split_q(q_ref[qi * tq:(qi + 1) * tq, :])
        _attend(qq, spans, s_ref.at[par], mx_ref.at[par], l_ref.at[par], acc_ref.at[par])
        y = _attn_finish(lam_ref, g_ref, l_ref.at[par], acc_ref.at[par], tq, out_scale)
        o_ref[qi * tq:(qi + 1) * tq, :] = y.astype(o_ref.dtype)


def _attn_prompt(q, k, v, bias, lam, g, out_scale):
    B, L, _ = q.shape
    H = q.shape[2] // LANES
    tq = bias.shape[-1]
    assert L % tq == 0 and tq % CHUNK == 0
    return pl.pallas_call(
        functools.partial(_attn_prompt_body, tq=tq, out_scale=out_scale),
        grid=(B, H),
        in_specs=[
            pl.BlockSpec((None, L, LANES), lambda b, h: (b, 0, h)),
            pl.BlockSpec((None, L, LANES), lambda b, h: (b, 0, h)),
            pl.BlockSpec((None, L, LANES), lambda b, h: (b, 0, h)),
            pl.BlockSpec((None, 2, tq, tq), lambda b, h: (h, 0, 0, 0)),
            pl.BlockSpec((8, LANES), lambda b, h: (0, 0)),
            pl.BlockSpec((1, LANES), lambda b, h: (0, 0)),
        ],
        out_specs=pl.BlockSpec((None, L, LANES), lambda b, h: (b, 0, h)),
        out_shape=jax.ShapeDtypeStruct((B, L, H * LANES), BF16),
        scratch_shapes=[
            pltpu.VMEM((2, 2 * tq, L), F32),
            pltpu.VMEM((2, 2 * tq, LANES), F32),
            pltpu.VMEM((2, 2 * tq, LANES), F32),
            pltpu.VMEM((2, 2 * tq, LANES), F32),
        ],
        compiler_params=_cparams("parallel", "parallel"),
        name="diff_attn_prompt",
    )(q, k, v, bias, lam, g)


def _attn_sample_body(q_ref, kn_ref, vn_ref, kc_ref, vc_ref, bprev_ref, bdiag_ref, lam_ref, g_ref, o_ref,
                      s_ref, mx_ref, l_ref, acc_ref, *, lq, tk, out_scale):
    hp, P = kc_ref.shape[0], kc_ref.shape[1]
    near = P - tk
    for i in range(hp):
        cs = slice(i * LANES, (i + 1) * LANES)

        def span(st, w, bias, i=i):
            return (st, lambda: kc_ref[i, st:st + w, :].astype(BF16),
                    lambda: vc_ref[i, st:st + w, :].astype(BF16), bias)

        spans = [span(st, min(ATTN_KEY_SPAN, near - st), None) for st in range(0, near, ATTN_KEY_SPAN)]
        spans.append(span(near, tk, bprev_ref[i]))
        spans.append((P, lambda cs=cs: kn_ref[:, cs], lambda cs=cs: vn_ref[:, cs], bdiag_ref[i, :, 0:lq]))
        _attend(_split_q(q_ref[:, cs]), spans, s_ref.at[i], mx_ref.at[i], l_ref.at[i], acc_ref.at[i])
        y = _attn_finish(lam_ref, g_ref, l_ref.at[i], acc_ref.at[i], lq, out_scale)
        o_ref[:, cs] = y.astype(o_ref.dtype)


def _attn_sample(q, k, v, k_cache, v_cache, bias, lam, g, out_scale):
    B, lq, _ = q.shape
    H = q.shape[2] // LANES
    P = k_cache.shape[2]
    tk = bias.shape[-1]
    assert lq == CHUNK and P % tk == 0 and P % CHUNK == 0 and lq <= tk
    hp = ATTN_SAMPLE_HEADS if H % ATTN_SAMPLE_HEADS == 0 else 1
    ng = H // hp
    return pl.pallas_call(
        functools.partial(_attn_sample_body, lq=lq, tk=tk, out_scale=out_scale),
        grid=(B, ng),
        in_specs=[
            pl.BlockSpec((None, lq, hp * LANES), lambda b, g: (b, 0, g)),
            pl.BlockSpec((None, lq, hp * LANES), lambda b, g: (b, 0, g)),
            pl.BlockSpec((None, lq, hp * LANES), lambda b, g: (b, 0, g)),
            pl.BlockSpec((None, hp, P, LANES), lambda b, g: (b, g, 0, 0)),
            pl.BlockSpec((None, hp, P, LANES), lambda b, g: (b, g, 0, 0)),
            pl.BlockSpec((hp, None, lq, tk), lambda b, g: (g, 0, 0, 0)),
            pl.BlockSpec((hp, None, lq, tk), lambda b, g: (g, 1, 0, 0)),
            pl.BlockSpec((8, LANES), lambda b, g: (0, 0)),
            pl.BlockSpec((1, LANES), lambda b, g: (0, 0)),
        ],
        out_specs=pl.BlockSpec((None, lq, hp * LANES), lambda b, g: (b, 0, g)),
        out_shape=jax.ShapeDtypeStruct((B, lq, H * LANES), BF16),
        scratch_shapes=[
            pltpu.VMEM((hp, 2 * lq, P + LANES), F32),
            pltpu.VMEM((hp, 2 * lq, LANES), F32),
            pltpu.VMEM((hp, 2 * lq, LANES), F32),
            pltpu.VMEM((hp, 2 * lq, LANES), F32),
        ],
        compiler_params=_cparams("parallel", "arbitrary"),
        name="diff_attn_sample",
    )(q, k, v, k_cache, v_cache, bias, bias, lam, g)


HGRN_PROJ_HEADS = 1
HGRN_HEADS_PER_STEP = 4
MIN_LOG2 = -150.0


def _hgrn_setup(lb_ref, lvl_ref, ck, layer):
    n_lev = ck.bit_length() - 1
    t_i = lax.broadcasted_iota(jnp.int32, (ck, ck), 0)
    s_i = lax.broadcasted_iota(jnp.int32, (ck, ck), 1)
    x = t_i ^ s_i
    hb = jnp.zeros((ck, ck), jnp.int32)
    for b in range(1, n_lev):
        hb = hb + (x >= (1 << b)).astype(jnp.int32)
    lvl_ref[...] = jnp.where(t_i > s_i, hb, -1)

    lg = lb_ref[...]
    e = jnp.exp(lg - jnp.max(lg, axis=0, keepdims=True))
    p = e / jnp.sum(e, axis=0, keepdims=True)
    cum = p[0:1]
    for d in range(1, layer + 1):
        cum = cum + p[d:d + 1]
    lb = cum - p[0:1]
    return lb, 1.0 - lb, lax.broadcasted_iota(jnp.int32, (ck, LANES), 0)


def _hgrn_chunk(hq, z, v, hg, consts, lvl_ref, ng_ref, st_ref):
    lb, one_mlb, row = consts
    ck = hq.shape[0]
    n_lev = ck.bit_length() - 1
    q = hq * jax.nn.sigmoid(hq)
    r = 1.0 / (1.0 + jnp.exp(z))
    k = one_mlb * r
    f = lb + one_mlb * (1.0 - r)
    g = jnp.maximum(jnp.log2(f), MIN_LOG2)
    vb = v.astype(BF16)

    pf = g
    tot = g
    lvl = lvl_ref[...]
    a = jnp.zeros((ck, ck), F32)
    for lev in range(n_lev):
        hsz = 1 << lev
        kl = (k if lev == 0 else k * jnp.exp2(tot - pf)).astype(BF16)
        if hsz % SUBLANES == 0:
            nb = ck // (2 * hsz)
            late = lambda m: m.reshape(nb, 2, hsz, m.shape[-1])[:, 1:2]
            ql = (late(q) * jnp.exp2(late(pf))).astype(BF16).reshape(ck // 2, LANES)
            p_lev = _dot_nt(ql, kl).reshape(nb, 1, hsz, ck)
            a4 = a.reshape(nb, 2, hsz, ck)
            a = jnp.concatenate([a4[:, 0:1], jnp.where(late(lvl) == lev, p_lev, a4[:, 1:2])],
                                axis=1).reshape(ck, ck)
            t4 = tot.reshape(nb, 2, hsz, LANES)
            p4 = pf.reshape(nb, 2, hsz, LANES)
            both = t4[:, 0:1] + t4[:, 1:2]
            tot = jnp.concatenate([both, both], axis=1).reshape(ck, LANES)
            pf = jnp.concatenate([p4[:, 0:1], p4[:, 1:2] + t4[:, 0:1]], axis=1).reshape(ck, LANES)
        else:
            ql = (q * jnp.exp2(pf)).astype(BF16)
            a = jnp.where(lvl == lev, _dot_nt(ql, kl), a)
            second = (row & hsz) != 0
            t3 = tot.reshape(ck // SUBLANES, SUBLANES, LANES)
            other = pltpu.roll(t3, hsz, 1)
            if 2 * hsz < SUBLANES:
                other = jnp.where(second.reshape(t3.shape), other, pltpu.roll(t3, SUBLANES - hsz, 1))
            other = other.reshape(ck, LANES)
            pf = pf + jnp.where(second, other, 0.0)
            tot = tot + other
    st = st_ref[...]
    o = _dot(a.astype(BF16), vb) + jnp.sum(q * k, axis=1, keepdims=True) * v
    o = o + _dot_nt((q * jnp.exp2(pf)).astype(BF16), st.astype(BF16))
    kst = (k * jnp.exp2(tot - pf)).astype(BF16)
    st_ref[...] = st * jnp.exp2(tot[0:1, :]) + _dot(v.T.astype(BF16), kst)
    y = o * lax.rsqrt(jnp.mean(o * o, axis=-1, keepdims=True) + EPS) * ng_ref[...]
    return y * (hg * jax.nn.sigmoid(hg))


def _hgrn_body(hq_ref, hz_ref, hi_ref, hg_ref, lb_ref, ng_ref, s0_ref, o_ref, s_ref, st_ref, lvl_ref, *, ck,
               layer):
    l = pl.program_id(2)
    tl = hq_ref.shape[0]
    hp = st_ref.shape[0]

    @pl.when(l == 0)
    def _():
        for i in range(hp):
            st_ref[i] = s0_ref[i].T

    lb, one_mlb, row = _hgrn_setup(lb_ref, lvl_ref, ck, layer)

    def chunk(c, carry):
        rs = slice(c * ck, (c + 1) * ck) if isinstance(c, int) else pl.ds(pl.multiple_of(c * ck, ck), ck)
        for i in range(hp):
            cs = slice(i * LANES, (i + 1) * LANES)
            y = _hgrn_chunk(hq_ref[rs, cs], hz_ref[rs, cs], hi_ref[rs, cs], hg_ref[rs, cs],
                            (lb[:, cs], one_mlb[:, cs], row), lvl_ref, ng_ref, st_ref.at[i])
            o_ref[rs, cs] = y.astype(o_ref.dtype)
        return carry

    n_chunks = tl // ck
    if n_chunks == 1:
        chunk(0, 0)
    else:
        lax.fori_loop(0, n_chunks, chunk, 0, unroll=4 if n_chunks % 4 == 0 else 1)

    @pl.when(l == pl.num_programs(2) - 1)
    def _():
        for i in range(hp):
            s_ref[i] = st_ref[i].T


def _hgrn(h, lb_logits, norm_g, s0, layer, n_heads, tl=2048, ck=128):
    B, L, _ = h.shape
    H = n_heads
    tl = _pick(L, tl)
    ck = _pick(tl, ck)
    assert ck & (ck - 1) == 0 and ck >= 8
    depth = lb_logits.shape[0]
    hp = HGRN_HEADS_PER_STEP if (L <= LANES and H % HGRN_HEADS_PER_STEP == 0) else 1
    ng = H // hp
    blk = lambda part: pl.BlockSpec((None, tl, hp * LANES), lambda b, g, l: (b, l, part * ng + g))
    return pl.pallas_call(
        functools.partial(_hgrn_body, ck=ck, layer=layer),
        grid=(B, ng, L // tl),
        in_specs=[
            blk(0), blk(1), blk(2), blk(3),
            pl.BlockSpec((depth, hp * LANES), lambda b, g, l: (0, g)),
            pl.BlockSpec((1, LANES), lambda b, g, l: (0, 0)),
            pl.BlockSpec((None, hp, LANES, LANES), lambda b, g, l: (b, g, 0, 0)),
        ],
        out_specs=[
            pl.BlockSpec((None, tl, hp * LANES), lambda b, g, l: (b, l, g)),
            pl.BlockSpec((None, hp, LANES, LANES), lambda b, g, l: (b, g, 0, 0)),
        ],
        out_shape=[
            jax.ShapeDtypeStruct((B, L, H * LANES), BF16),
            jax.ShapeDtypeStruct((B, H, LANES, LANES), F32),
        ],
        scratch_shapes=[pltpu.VMEM((hp, LANES, LANES), F32), pltpu.VMEM((ck, ck), jnp.int32)],
        compiler_params=_cparams("parallel", "parallel", "arbitrary"),
        name="hgrn_scan",
    )(h, h, h, h, lb_logits, norm_g, s0)


def _hgrn_proj_body(x_ref, w_ref, lb_ref, ng_ref, s0_ref, o_ref, s_ref, st_ref, lvl_ref, h_ref, *, ck, layer,
                    rows):
    L = x_ref.shape[0]
    hp = w_ref.shape[0]
    lb, one_mlb, row = _hgrn_setup(lb_ref, lvl_ref, ck, layer)
    n_proj = L // rows

    def project(i, s):
        h_ref[i, s % 2] = _dot(x_ref[s * rows:(s + 1) * rows, :], w_ref[i])

    project(0, 0)
    for i in range(hp):
        cs = slice(i * LANES, (i + 1) * LANES)
        consts = (lb[:, cs], one_mlb[:, cs], row)
        st = st_ref.at[i]
        st[...] = s0_ref[i].T
        for s in range(n_proj):
            if s + 1 < n_proj:
                project(i, s + 1)
            elif i + 1 < hp:
                project(i + 1, 0)
            h = h_ref.at[i, s % 2]
            for c in range(rows // ck):
                rs = slice(c * ck, (c + 1) * ck)
                y = _hgrn_chunk(h[rs, 0:LANES], h[rs, LANES:2 * LANES], h[rs, 2 * LANES:3 * LANES],
                                h[rs, 3 * LANES:4 * LANES], consts, lvl_ref, ng_ref, st)
                o_ref[s * rows + c * ck:s * rows + (c + 1) * ck, cs] = y.astype(o_ref.dtype)
        s_ref[i] = st[...].T


def _hgrn_proj(x, w_heads, lb_logits, norm_g, s0, layer, ck=128, rows=512):
    B, L, D = x.shape
    H = w_heads.shape[0]
    assert L % rows == 0 and rows % ck == 0 and ck & (ck - 1) == 0
    depth = lb_logits.shape[0]
    hp = HGRN_PROJ_HEADS if H % HGRN_PROJ_HEADS == 0 else 1
    return pl.pallas_call(
        functools.partial(_hgrn_proj_body, ck=ck, layer=layer, rows=rows),
        grid=(B, H // hp),
        in_specs=[
            pl.BlockSpec((None, L, D), lambda b, g: (b, 0, 0)),
            pl.BlockSpec((hp, D, 4 * LANES), lambda b, g: (g, 0, 0)),
            pl.BlockSpec((depth, hp * LANES), lambda b, g: (0, g)),
            pl.BlockSpec((1, LANES), lambda b, g: (0, 0)),
            pl.BlockSpec((None, hp, LANES, LANES), lambda b, g: (b, g, 0, 0)),
        ],
        out_specs=[
            pl.BlockSpec((None, L, hp * LANES), lambda b, g: (b, 0, g)),
            pl.BlockSpec((None, hp, LANES, LANES), lambda b, g: (b, g, 0, 0)),
        ],
        out_shape=[
            jax.ShapeDtypeStruct((B, L, H * LANES), BF16),
            jax.ShapeDtypeStruct((B, H, LANES, LANES), F32),
        ],
        scratch_shapes=[
            pltpu.VMEM((hp, LANES, LANES), F32),
            pltpu.VMEM((ck, ck), jnp.int32),
            pltpu.VMEM((hp, 2, rows, 4 * LANES), F32),
        ],
        compiler_params=_cparams("parallel", "arbitrary"),
        name="hgrn_proj_scan",
    )(x, w_heads, lb_logits, norm_g, s0)


def _trunk(x, p, k_cache, v_cache, pool_hist, hg_state, W, attn_prep):
    B, L, D = x.shape
    depth = W["ln_g"].shape[0]
    alpha = (2 * depth) ** 0.25
    T = B * L
    xt = x.reshape(T, D)
    new_k, new_v, new_pool, new_s = [], [], [], []
    pool_width = W["pool_scale"].shape[-1]
    da_width = D - pool_width
    n_da_heads = da_width // LANES
    n_hg_heads = D // LANES
    lnrow = lambda a, i, s: a[i, s].reshape(1, D)
    for i in range(depth):
        fuse_proj = i % 2 == 1 and L % HGRN_PROJ_ROWS == 0
        want16 = fuse_proj or i % 2 == 0
        xt = _ffn_ln(xt, W["wg"], W["wu"], W["wd"], lnrow(W["ln_g"], i, 0), lnrow(W["ln_b"], i, 0), i, 0, alpha,
                     also_bf16=want16)
        if want16:
            xt, xt16 = xt
        if i % 2 == 0:
            e = i // 2
            w_in = W["w_in_even"]
            (u,) = _proj(xt16, w_in, e, 0, pool_width, B, L, ("f32",))
            (q16,) = _proj(xt16, w_in, e, pool_width, da_width, B, L, ("bf16",), scale=(LANES // 2) ** -0.5)
            k, k16 = _proj(xt16, w_in, e, pool_width + da_width, da_width, B, L, ("heads", "bf16"))
            v, v16 = _proj(xt16, w_in, e, pool_width + 2 * da_width, da_width, B, L, ("heads", "bf16"))
            past = 0 if k_cache is None else k_cache.shape[2]
            pool_out, nh = _pool(u.reshape(B, L, pool_width), pool_hist[e], W["pool_w"][e],
                                 W["pool_scale"][e].reshape(1, pool_width), past)
            bias, lam = attn_prep[e]
            lam_init = 0.8 - 0.6 * math.exp(-0.3 * i)
            g = W["diff_norm_g"][e].reshape(1, LANES)
            q3, k3, v3 = (a.reshape(B, L, da_width) for a in (q16, k16, v16))
            if k_cache is None:
                o = _attn_prompt(q3, k3, v3, bias, lam, g, 1.0 - lam_init)
            else:
                kc = jnp.transpose(k_cache[e], (0, 2, 1, 3))
                vc = jnp.transpose(v_cache[e], (0, 2, 1, 3))
                o = _attn_sample(q3, k3, v3, kc, vc, bias, lam, g, 1.0 - lam_init)
            parts = [pool_out.reshape(T, pool_width), o.reshape(T, da_width)]
            w_out = W["w_out_even"]
            new_k.append(jnp.transpose(k, (0, 2, 1, 3)))
            new_v.append(jnp.transpose(v, (0, 2, 1, 3)))
            new_pool.append(nh)
            li = e
        else:
            od = i // 2
            ng = W["hgrn_norm_g"][od].reshape(1, LANES)
            if fuse_proj:
                o, s = _hgrn_proj(xt16.reshape(B, L, D), W["w_in_odd_heads"][od], W["lb_logits"], ng,
                                  hg_state[od], i, rows=HGRN_PROJ_ROWS)
            else:
                h = _matmul(xt, W["w_in_odd"], od)
                o, s = _hgrn(h.reshape(B, L, 4 * D), W["lb_logits"], ng, hg_state[od], i, n_hg_heads)
            parts = [o.reshape(T, D)]
            w_out = W["w_out_odd"]
            new_s.append(s)
            li = od
        xt = _out_ln(parts, xt, w_out, li, lnrow(W["ln_g"], i, 1), lnrow(W["ln_b"], i, 1), alpha)
        xt = _ffn_ln(xt, W["wg"], W["wu"], W["wd"], lnrow(W["ln_g"], i, 2), lnrow(W["ln_b"], i, 2), i, 1, alpha)
        xt = _ple(xt, p.reshape(depth, T, -1), W["w_ple_gate"], W["w_ple_up"], i)
    return xt.reshape(B, L, D), jnp.stack(new_k), jnp.stack(new_v), jnp.stack(new_pool), jnp.stack(new_s)


ATTN_TILE = 256
HGRN_PROJ_ROWS = 512


def kernel(x_prompt, x_sample, cache_diff_k, cache_diff_v, state_pool, state_hgrn, p_prompt, p_sample, ln_g, ln_b, w_ffn_gate, w_ffn_up, w_ffn_down, w_ple_gate, w_ple_up, w_in_even, w_out_even, pool_w, pool_scale, lam_q1, lam_k1, lam_q2, lam_k2, diff_norm_g, rel_bias, w_in_odd, w_out_odd, hgrn_norm_g, hgrn_lb_logits):
    bf = lambda a: a.astype(BF16)
    W = dict(
        ln_g=ln_g.astype(F32), ln_b=ln_b.astype(F32),
        wg=bf(w_ffn_gate), wu=bf(w_ffn_up), wd=bf(w_ffn_down),
        w_ple_gate=bf(w_ple_gate), w_ple_up=bf(w_ple_up),
        w_in_even=bf(w_in_even), w_out_even=bf(w_out_even),
        pool_w=bf(pool_w), pool_scale=pool_scale.astype(F32),
        diff_norm_g=diff_norm_g.astype(F32),
        w_in_odd=bf(w_in_odd), w_out_odd=bf(w_out_odd),
        hgrn_norm_g=hgrn_norm_g.astype(F32), lb_logits=hgrn_lb_logits.astype(F32),
    )
    n_even = w_in_even.shape[0]
    n_odd = w_in_odd.shape[0]
    d_model = w_in_odd.shape[1]
    n_hg = d_model // LANES
    W["w_in_odd_heads"] = jnp.transpose(W["w_in_odd"].reshape(n_odd, d_model, 4, n_hg, LANES),
                                        (0, 3, 1, 2, 4)).reshape(n_odd, n_hg, d_model, 4 * LANES)
    attn_prep = []
    for e in range(n_even):
        lam_init = 0.8 - 0.6 * math.exp(-0.3 * (2 * e))
        attn_prep.append(_attn_prep(rel_bias, lam_q1[e], lam_k1[e], lam_q2[e], lam_k2[e], ATTN_TILE, lam_init))

    B = x_prompt.shape[0]
    dt = x_prompt.dtype
    zero_pool = jnp.zeros((n_even, B) + state_pool.shape[2:], dt)
    zero_s = jnp.zeros((n_odd, B) + state_hgrn.shape[2:], dt)
    y_p, k_p, v_p, pool_p, s_p = _trunk(x_prompt, p_prompt, None, None, zero_pool, zero_s, W, attn_prep)
    y_s, k_s, v_s, pool_s, s_s = _trunk(x_sample, p_sample, cache_diff_k, cache_diff_v, state_pool, state_hgrn,
                                        W, attn_prep)
    return (y_p, y_s, k_p, v_p, k_s, v_s, pool_p, pool_s, s_p, s_s)
```

```python
import functools
import math

import jax
import jax.numpy as jnp
from jax import lax
from jax.experimental import pallas as pl
from jax.experimental.pallas import tpu as pltpu

F32 = jnp.float32
BF16 = jnp.bfloat16

CHUNK = 64
POOL_WINDOWS = (2, 4, 8, 16)
POOL_HIST = max(POOL_WINDOWS) - 1
N_BUCKETS = 32
MAX_DISTANCE = 128
EPS = 1e-5
NEG = -1e30
LOG2E = math.log2(math.e)
LANES = 128
SUBLANES = 8

VMEM_LIMIT = 56 * 1024 * 1024


def _cparams(*sem):
    return pltpu.CompilerParams(dimension_semantics=sem, vmem_limit_bytes=VMEM_LIMIT)


def _pick(n, pref):
    if n <= pref:
        return n
    t = pref
    while n % t:
        t //= 2
    return t


def _layer_norm(y, g, b):
    mu = jnp.mean(y, axis=-1, keepdims=True)
    d = y - mu
    var = jnp.mean(d * d, axis=-1, keepdims=True)
    return d * lax.rsqrt(var + EPS) * g + b


def _dot(a, b):
    return jnp.dot(a, b, preferred_element_type=F32)


def _dot_nt(a, b):
    return lax.dot_general(a, b, (((1,), (1,)), ((), ())), preferred_element_type=F32)


FFN_COL_CHUNK = 512
MXU_COLS = 2 * LANES


def _ffn_act_body(x_ref, wg_ref, wu_ref, a_ref, xb_ref):
    @pl.when(pl.program_id(1) == 0)
    def _():
        xb_ref[...] = x_ref[...].astype(BF16)

    def gated(hg, hu):
        return (hg * jax.nn.sigmoid(hg) * hu).astype(a_ref.dtype)

    xb = xb_ref[...]
    tf = a_ref.shape[1]
    main = tf // MXU_COLS * MXU_COLS
    for c0 in range(0, main, FFN_COL_CHUNK):
        cs = slice(c0, min(c0 + FFN_COL_CHUNK, main))
        a_ref[:, cs] = gated(_dot(xb, wg_ref[:, cs]), _dot(xb, wu_ref[:, cs]))
    if main < tf:
        assert tf - main == LANES
        h = _dot(xb, jnp.concatenate([wg_ref[:, main:tf], wu_ref[:, main:tf]], axis=1))
        a_ref[:, main:tf] = gated(h[:, :LANES], h[:, LANES:])


def _ffn_down_ln_body(a_ref, x_ref, wd_ref, g_ref, b_ref, *o_refs, alpha):
    tm = x_ref.shape[0]
    hm = tm // 2
    for rows in (slice(0, hm), slice(hm, tm)):
        y = alpha * x_ref[rows, :] + 0.5 * _dot(a_ref[rows, :], wd_ref[...])
        y = _layer_norm(y, g_ref[...], b_ref[...])
        for o_ref in o_refs:
            o_ref[rows, :] = y.astype(o_ref.dtype)


def _ffn_ln(x, wg, wu, wd, g, b, li, si, alpha, also_bf16=False, tm_act=1024, tf=1408, tm_down=256):
    T, D = x.shape
    F = wd.shape[-2]
    tm = _pick(T, tm_act)
    tf = tf if F % tf == 0 else _pick(F, 512)
    act = pl.pallas_call(
        _ffn_act_body,
        grid=(T // tm, F // tf),
        in_specs=[
            pl.BlockSpec((tm, D), lambda t, j: (t, 0)),
            pl.BlockSpec((None, None, D, tf), lambda t, j: (li, si, 0, j)),
            pl.BlockSpec((None, None, D, tf), lambda t, j: (li, si, 0, j)),
        ],
        out_specs=pl.BlockSpec((tm, tf), lambda t, j: (t, j)),
        out_shape=jax.ShapeDtypeStruct((T, F), BF16),
        scratch_shapes=[pltpu.VMEM((tm, D), BF16)],
        compiler_params=_cparams("parallel", "arbitrary"),
        name="ffn_act",
    )(x, wg, wu)

    tm = _pick(T, tm_down)
    out_dtypes = (F32, BF16) if also_bf16 else (F32,)
    outs = pl.pallas_call(
        functools.partial(_ffn_down_ln_body, alpha=alpha),
        grid=(T // tm,),
        in_specs=[
            pl.BlockSpec((tm, F), lambda t: (t, 0)),
            pl.BlockSpec((tm, D), lambda t: (t, 0)),
            pl.BlockSpec((None, None, F, D), lambda t: (li, si, 0, 0), pipeline_mode=pl.Buffered(1)),
            pl.BlockSpec((1, D), lambda t: (0, 0)),
            pl.BlockSpec((1, D), lambda t: (0, 0)),
        ],
        out_specs=[pl.BlockSpec((tm, D), lambda t: (t, 0)) for _ in out_dtypes],
        out_shape=[jax.ShapeDtypeStruct((T, D), dt) for dt in out_dtypes],
        compiler_params=_cparams("parallel"),
        name="ffn_down_ln",
    )(act, x, wd, g, b)
    return tuple(outs) if also_bf16 else outs[0]


def _matmul_body(x_ref, w_ref, o_ref, xb_ref):
    @pl.when(pl.program_id(1) == 0)
    def _():
        xb_ref[...] = x_ref[...].astype(BF16)

    o_ref[...] = _dot(xb_ref[...], w_ref[...]).astype(o_ref.dtype)


def _matmul(x, w, li, out_dtype=F32, tm=1024, tn=1024):
    T, K = x.shape
    N = w.shape[-1]
    tm = _pick(T, tm)
    tn = _pick(N, tn)
    return pl.pallas_call(
        _matmul_body,
        grid=(T // tm, N // tn),
        in_specs=[
            pl.BlockSpec((tm, K), lambda t, j: (t, 0)),
            pl.BlockSpec((None, K, tn), lambda t, j: (li, 0, j)),
        ],
        out_specs=pl.BlockSpec((tm, tn), lambda t, j: (t, j)),
        out_shape=jax.ShapeDtypeStruct((T, N), out_dtype),
        scratch_shapes=[pltpu.VMEM((tm, K), BF16)],
        compiler_params=_cparams("parallel", "arbitrary"),
        name="in_proj",
    )(x, w)


def _proj_body(x_ref, w_ref, *o_refs, scale):
    r = _dot(x_ref[...], w_ref[...])
    for o_ref in o_refs:
        if len(o_ref.shape) == 4:
            bb, hpt, tl, _ = o_ref.shape
            for hh in range(hpt):
                o_ref[:, hh] = r[:, hh * LANES:(hh + 1) * LANES].reshape(bb, tl, LANES)
        elif o_ref.dtype == BF16:
            o_ref[...] = (r * scale).astype(BF16)
        else:
            o_ref[...] = r


def _proj(x16, w, li, col0, width, B, L, kinds, scale=1.0, tm=2048, tn=512):
    T, K = x16.shape
    tm = _pick(T, tm)
    assert width % tn == 0 and col0 % tn == 0 and tn % LANES == 0
    hpt = tn // LANES
    tl = min(L, tm)
    assert tm % tl == 0 and L % tl == 0
    bb, nl = tm // tl, L // tl
    specs, shapes = [], []
    for kind in kinds:
        if kind == "heads":
            specs.append(pl.BlockSpec((bb, hpt, tl, LANES), lambda t, j: (t // nl, j, t % nl, 0)))
            shapes.append(jax.ShapeDtypeStruct((B, width // LANES, L, LANES), F32))
        else:
            specs.append(pl.BlockSpec((tm, tn), lambda t, j: (t, j)))
            shapes.append(jax.ShapeDtypeStruct((T, width), BF16 if kind == "bf16" else F32))
    return pl.pallas_call(
        functools.partial(_proj_body, scale=scale),
        grid=(T // tm, width // tn),
        in_specs=[
            pl.BlockSpec((tm, K), lambda t, j: (t, 0)),
            pl.BlockSpec((None, K, tn), lambda t, j: (li, 0, col0 // tn + j)),
        ],
        out_specs=specs,
        out_shape=shapes,
        compiler_params=_cparams("parallel", "arbitrary"),
        name="in_proj_cols",
    )(x16, w)


def _out_ln_body(*refs, alpha, widths):
    n = len(widths)
    parts = refs[:n]
    x_ref, w_ref, g_ref, b_ref, o_ref = refs[n:]
    tm = x_ref.shape[0]
    n_split = 2 if tm % 16 == 0 else 1
    hm = tm // n_split
    for h in range(n_split):
        rows = slice(h * hm, (h + 1) * hm)
        acc = alpha * x_ref[rows, :]
        off = 0
        for p_ref, wd in zip(parts, widths):
            acc = acc + _dot(p_ref[rows, :], w_ref[off:off + wd, :])
            off += wd
        o_ref[rows, :] = _layer_norm(acc, g_ref[...], b_ref[...])


def _out_ln(parts, x, w, li, g, b, alpha, tm=512):
    T, D = x.shape
    tm = _pick(T, tm)
    widths = tuple(p.shape[1] for p in parts)
    kin = sum(widths)
    return pl.pallas_call(
        functools.partial(_out_ln_body, alpha=alpha, widths=widths),
        grid=(T // tm,),
        in_specs=[pl.BlockSpec((tm, wd), lambda t: (t, 0)) for wd in widths] + [
            pl.BlockSpec((tm, D), lambda t: (t, 0)),
            pl.BlockSpec((None, kin, D), lambda t: (li, 0, 0)),
            pl.BlockSpec((1, D), lambda t: (0, 0)),
            pl.BlockSpec((1, D), lambda t: (0, 0)),
        ],
        out_specs=pl.BlockSpec((tm, D), lambda t: (t, 0)),
        out_shape=jax.ShapeDtypeStruct((T, D), F32),
        compiler_params=_cparams("parallel"),
        name="out_proj_ln",
    )(*parts, x, w, g, b)


def _ple_body(x_ref, p_ref, wg_ref, wu_ref, o_ref, *, tn):
    xb = x_ref[...].astype(BF16)
    pb = p_ref[...].astype(BF16)
    D = o_ref.shape[1]
    for c in range(D // tn):
        sl = slice(c * tn, (c + 1) * tn)
        gate = jax.nn.sigmoid(_dot(xb, wg_ref[:, sl]))
        up = _dot(pb, wu_ref[:, sl])
        o_ref[:, sl] = x_ref[:, sl] + gate * up


def _ple(x, p, wg, wu, li, tm=512, tn=512):
    T, D = x.shape
    P = p.shape[2]
    tm = _pick(T, tm)
    return pl.pallas_call(
        functools.partial(_ple_body, tn=_pick(D, tn)),
        grid=(T // tm,),
        in_specs=[
            pl.BlockSpec((tm, D), lambda t: (t, 0)),
            pl.BlockSpec((None, tm, P), lambda t: (li, t, 0)),
            pl.BlockSpec((None, D, D), lambda t: (li, 0, 0)),
            pl.BlockSpec((None, P, D), lambda t: (li, 0, 0)),
        ],
        out_specs=pl.BlockSpec((tm, D), lambda t: (t, 0)),
        out_shape=jax.ShapeDtypeStruct((T, D), F32),
        compiler_params=_cparams("parallel"),
        name="ple_gate",
    )(x, p, wg, wu)


def _pool_body(u_ref, hist_ref, w_ref, sc_ref, o_ref, nh_ref, ext_ref, *, tl, start_pos):
    l = pl.program_id(1)
    nl = pl.num_programs(1)
    H = POOL_HIST + 1

    @pl.when(l == 0)
    def _():
        ext_ref[0:1, :] = jnp.zeros((1, ext_ref.shape[1]), F32)
        ext_ref[1:H, :] = hist_ref[...]

    @pl.when(l > 0)
    def _():
        ext_ref[0:H, :] = ext_ref[tl:tl + H, :]

    ext_ref[H:H + tl, :] = u_ref[...]

    pos = start_pos + l * tl + lax.broadcasted_iota(jnp.int32, (tl, 1), 0)
    gd = LANES
    for g, wnd in enumerate(POOL_WINDOWS):
        cs = slice(g * gd, (g + 1) * gd)
        s = ext_ref[H:H + tl, cs]
        cur = s
        for d in range(1, wnd):
            s = s + ext_ref[H - d:H - d + tl, cs]
        cnt = jnp.minimum(pos + 1, wnd).astype(F32)
        pooled = s / cnt - cur
        y = _dot(pooled.astype(BF16), w_ref[g]) * sc_ref[:, cs]
        o_ref[:, cs] = y.astype(o_ref.dtype)

    @pl.when(l == nl - 1)
    def _():
        nh_ref[...] = ext_ref[tl + 1:tl + H, :]


def _pool(u, hist, w, scale, start_pos, tl=512):
    B, L, PW = u.shape
    tl = _pick(L, tl)
    assert tl >= POOL_HIST + 1
    return pl.pallas_call(
        functools.partial(_pool_body, tl=tl, start_pos=start_pos),
        grid=(B, L // tl),
        in_specs=[
            pl.BlockSpec((None, tl, PW), lambda b, l: (b, l, 0)),
            pl.BlockSpec((None, POOL_HIST, PW), lambda b, l: (b, 0, 0)),
            pl.BlockSpec(w.shape, lambda b, l: (0, 0, 0)),
            pl.BlockSpec((1, PW), lambda b, l: (0, 0)),
        ],
        out_specs=[
            pl.BlockSpec((None, tl, PW), lambda b, l: (b, l, 0)),
            pl.BlockSpec((None, POOL_HIST, PW), lambda b, l: (b, 0, 0)),
        ],
        out_shape=[
            jax.ShapeDtypeStruct((B, L, PW), BF16),
            jax.ShapeDtypeStruct((B, POOL_HIST, PW), F32),
        ],
        scratch_shapes=[pltpu.VMEM((tl + POOL_HIST + 1, PW), F32)],
        compiler_params=_cparams("parallel", "arbitrary"),
        name="pool_mixer",
    )(u, hist, w, scale)


def _t5_bucket(rel):
    nb = N_BUCKETS // 2
    max_exact = nb // 2
    n = jnp.abs(rel)
    nf = jnp.maximum(n, 1).astype(jnp.float32)
    large = max_exact + (jnp.log(nf / max_exact) / math.log(MAX_DISTANCE / max_exact)
                         * (nb - max_exact)).astype(jnp.int32)
    large = jnp.minimum(large, nb - 1)
    return jnp.where(rel > 0, nb, 0) + jnp.where(n < max_exact, n, large)


def _attn_prep_body(tbl_ref, bkt_ref, lq1_ref, lk1_ref, lq2_ref, lk2_ref, bias_ref, lam_ref, *, tq, lam_init,
                    far_bucket):
    h = pl.program_id(0)
    bkt = bkt_ref[...]
    far = tbl_ref[far_bucket, h]
    acc = jnp.zeros(bkt.shape, F32)
    for b in range(N_BUCKETS):
        acc = jnp.where(bkt == b, (tbl_ref[b, h] - far) * LOG2E, acc)
    r = lax.broadcasted_iota(jnp.int32, bkt.shape, 1)
    c = lax.broadcasted_iota(jnp.int32, bkt.shape, 2)
    t = lax.broadcasted_iota(jnp.int32, bkt.shape, 0)
    visible = (t == 0) | ((c // CHUNK) <= (r // CHUNK))
    bias_ref[...] = jnp.where(visible, acc, NEG)
    e1 = jnp.exp(jnp.sum(lq1_ref[...] * lk1_ref[...], axis=-1, keepdims=True))
    e2 = jnp.exp(jnp.sum(lq2_ref[...] * lk2_ref[...], axis=-1, keepdims=True))
    lam_ref[...] = jnp.broadcast_to(e1 - e2 + lam_init, lam_ref.shape)


def _attn_prep(rel_bias, lq1, lk1, lq2, lk2, tq, lam_init):
    nbk, H = rel_bias.shape
    r = jnp.arange(tq, dtype=jnp.int32)[:, None]
    c = jnp.arange(tq, dtype=jnp.int32)[None, :]
    bkt = jnp.stack([_t5_bucket(c - r - tq), _t5_bucket(c - r)])
    far_bucket = N_BUCKETS // 2 - 1
    assert tq >= MAX_DISTANCE
    row = lambda a: a.reshape(1, -1).astype(F32)
    return pl.pallas_call(
        functools.partial(_attn_prep_body, tq=tq, lam_init=lam_init, far_bucket=far_bucket),
        grid=(H,),
        in_specs=[
            pl.BlockSpec(memory_space=pltpu.SMEM),
            pl.BlockSpec((2, tq, tq), lambda h: (0, 0, 0)),
            pl.BlockSpec((1, lq1.shape[-1]), lambda h: (0, 0)),
            pl.BlockSpec((1, lq1.shape[-1]), lambda h: (0, 0)),
            pl.BlockSpec((1, lq1.shape[-1]), lambda h: (0, 0)),
            pl.BlockSpec((1, lq1.shape[-1]), lambda h: (0, 0)),
        ],
        out_specs=[
            pl.BlockSpec((None, 2, tq, tq), lambda h: (h, 0, 0, 0)),
            pl.BlockSpec((8, LANES), lambda h: (0, 0)),
        ],
        out_shape=[
            jax.ShapeDtypeStruct((H, 2, tq, tq), F32),
            jax.ShapeDtypeStruct((8, LANES), F32),
        ],
        compiler_params=_cparams("arbitrary"),
        name="attn_prep",
    )(rel_bias.astype(F32), bkt, row(lq1), row(lk1), row(lq2), row(lk2))


def _split_q(q):
    lane = lax.broadcasted_iota(jnp.int32, q.shape, 1)
    half = q.shape[1] // 2
    zero = jnp.zeros_like(q)
    return jnp.concatenate([jnp.where(lane < half, q, zero), jnp.where(lane >= half, q, zero)], axis=0)


def _attend(qq, spans, s_ref, mx_ref, l_ref, acc_ref):
    rows = qq.shape[0]
    mx_ref[...] = jnp.full(mx_ref.shape, NEG, F32)
    for col, get_k, _, bias in spans:
        s = _dot_nt(qq, get_k())
        w = s.shape[1]
        if bias is not None:
            s = (s.reshape(2, rows // 2, w) + bias[None]).reshape(rows, w)
        s_ref[:, col:col + w] = s
        if w % LANES == 0:
            m = functools.reduce(jnp.maximum, [s[:, c:c + LANES] for c in range(0, w, LANES)])
            mx_ref[...] = jnp.maximum(mx_ref[...], m)
        else:
            mx_ref[:, 0:w] = jnp.maximum(mx_ref[:, 0:w], s)
    m_b = jnp.broadcast_to(jnp.max(mx_ref[...], axis=1, keepdims=True), mx_ref.shape)
    mx_ref[...] = m_b
    l_ref[...] = jnp.zeros_like(l_ref)
    acc_ref[...] = jnp.zeros_like(acc_ref)
    for col, get_k, get_v, _ in spans:
        w = get_v().shape[0]
        s = s_ref[:, col:col + w]
        m_b = mx_ref[...]
        if w % LANES == 0:
            ps = [jnp.exp2(s[:, c:c + LANES] - m_b) for c in range(0, w, LANES)]
            l_ref[...] += functools.reduce(jnp.add, ps)
            p = ps[0] if len(ps) == 1 else jnp.concatenate(ps, axis=1)
        else:
            p = jnp.exp2(s - m_b[:, 0:w])
            l_ref[:, 0:w] += p
        acc_ref[...] += _dot(p.astype(BF16), get_v())


def _attn_finish(lam_ref, g_ref, l_ref, acc_ref, tq, out_scale):
    lam = lam_ref[0:1, 0:1]
    o = acc_ref[...] / jnp.sum(l_ref[...], axis=1, keepdims=True)
    o = o[:tq] - lam * o[tq:]
    return o * lax.rsqrt(jnp.mean(o * o, axis=-1, keepdims=True) + EPS) * g_ref[...] * out_scale


ATTN_KEY_SPAN = 512
ATTN_SAMPLE_HEADS = 4


def _attn_prompt_body(q_ref, k_ref, v_ref, bias_ref, lam_ref, g_ref, o_ref, s_ref, mx_ref, l_ref, acc_ref, *,
                      tq, out_scale):
    L = q_ref.shape[0]

    def span(st, w, bias):
        return (st, lambda: k_ref[st:st + w, :], lambda: v_ref[st:st + w, :], bias)

    for qi in reversed(range(L // tq)):
        par = qi % 2
        far_end = max(qi - 1, 0) * tq
        spans = [span(st, min(ATTN_KEY_SPAN, far_end - st), None) for st in range(0, far_end, ATTN_KEY_SPAN)]
        if qi >= 1:
            spans.append(span((qi - 1) * tq, tq, bias_ref[0]))
        spans.append(span(qi * tq, tq, bias_ref[1]))
        qq = _split_q(q_ref[qi * tq:(qi + 1) * tq, :])
        _attend(qq, spans, s_ref.at[par], mx_ref.at[par], l_ref.at[par], acc_ref.at[par])
        y = _attn_finish(lam_ref, g_ref, l_ref.at[par], acc_ref.at[par], tq, out_scale)
        o_ref[qi * tq:(qi + 1) * tq, :] = y.astype(o_ref.dtype)


def _attn_prompt(q, k, v, bias, lam, g, out_scale):
    B, L, _ = q.shape
    H = q.shape[2] // LANES
    tq = bias.shape[-1]
    assert L % tq == 0 and tq % CHUNK == 0
    return pl.pallas_call(
        functools.partial(_attn_prompt_body, tq=tq, out_scale=out_scale),
        grid=(B, H),
        in_specs=[
            pl.BlockSpec((None, L, LANES), lambda b, h: (b, 0, h)),
            pl.BlockSpec((None, L, LANES), lambda b, h: (b, 0, h)),
            pl.BlockSpec((None, L, LANES), lambda b, h: (b, 0, h)),
            pl.BlockSpec((None, 2, tq, tq), lambda b, h: (h, 0, 0, 0)),
            pl.BlockSpec((8, LANES), lambda b, h: (0, 0)),
            pl.BlockSpec((1, LANES), lambda b, h: (0, 0)),
        ],
        out_specs=pl.BlockSpec((None, L, LANES), lambda b, h: (b, 0, h)),
        out_shape=jax.ShapeDtypeStruct((B, L, H * LANES), BF16),
        scratch_shapes=[
            pltpu.VMEM((2, 2 * tq, L), F32),
            pltpu.VMEM((2, 2 * tq, LANES), F32),
            pltpu.VMEM((2, 2 * tq, LANES), F32),
            pltpu.VMEM((2, 2 * tq, LANES), F32),
        ],
        compiler_params=_cparams("parallel", "parallel"),
        name="diff_attn_prompt",
    )(q, k, v, bias, lam, g)


def _attn_sample_body(q_ref, kn_ref, vn_ref, kc_ref, vc_ref, bprev_ref, bdiag_ref, lam_ref, g_ref, o_ref,
                      s_ref, mx_ref, l_ref, acc_ref, *, lq, tk, out_scale):
    hp, P = kc_ref.shape[0], kc_ref.shape[1]
    near = P - tk
    for i in range(hp):
        cs = slice(i * LANES, (i + 1) * LANES)

        def span(st, w, bias, i=i):
            return (st, lambda: kc_ref[i, st:st + w, :].astype(BF16),
                    lambda: vc_ref[i, st:st + w, :].astype(BF16), bias)

        spans = [span(st, min(ATTN_KEY_SPAN, near - st), None) for st in range(0, near, ATTN_KEY_SPAN)]
        spans.append(span(near, tk, bprev_ref[i]))
        spans.append((P, lambda cs=cs: kn_ref[:, cs], lambda cs=cs: vn_ref[:, cs], bdiag_ref[i, :, 0:lq]))
        _attend(_split_q(q_ref[:, cs]), spans, s_ref.at[i], mx_ref.at[i], l_ref.at[i], acc_ref.at[i])
        y = _attn_finish(lam_ref, g_ref, l_ref.at[i], acc_ref.at[i], lq, out_scale)
        o_ref[:, cs] = y.astype(o_ref.dtype)


def _attn_sample(q, k, v, k_cache, v_cache, bias, lam, g, out_scale):
    B, lq, _ = q.shape
    H = q.shape[2] // LANES
    P = k_cache.shape[2]
    tk = bias.shape[-1]
    assert lq == CHUNK and P % tk == 0 and P % CHUNK == 0 and lq <= tk
    hp = ATTN_SAMPLE_HEADS if H % ATTN_SAMPLE_HEADS == 0 else 1
    ng = H // hp
    return pl.pallas_call(
        functools.partial(_attn_sample_body, lq=lq, tk=tk, out_scale=out_scale),
        grid=(B, ng),
        in_specs=[
            pl.BlockSpec((None, lq, hp * LANES), lambda b, g: (b, 0, g)),
            pl.BlockSpec((None, lq, hp * LANES), lambda b, g: (b, 0, g)),
            pl.BlockSpec((None, lq, hp * LANES), lambda b, g: (b, 0, g)),
            pl.BlockSpec((None, hp, P, LANES), lambda b, g: (b, g, 0, 0)),
            pl.BlockSpec((None, hp, P, LANES), lambda b, g: (b, g, 0, 0)),
            pl.BlockSpec((hp, None, lq, tk), lambda b, g: (g, 0, 0, 0)),
            pl.BlockSpec((hp, None, lq, tk), lambda b, g: (g, 1, 0, 0)),
            pl.BlockSpec((8, LANES), lambda b, g: (0, 0)),
            pl.BlockSpec((1, LANES), lambda b, g: (0, 0)),
        ],
        out_specs=pl.BlockSpec((None, lq, hp * LANES), lambda b, g: (b, 0, g)),
        out_shape=jax.ShapeDtypeStruct((B, lq, H * LANES), BF16),
        scratch_shapes=[
            pltpu.VMEM((hp, 2 * lq, P + LANES), F32),
            pltpu.VMEM((hp, 2 * lq, LANES), F32),
            pltpu.VMEM((hp, 2 * lq, LANES), F32),
            pltpu.VMEM((hp, 2 * lq, LANES), F32),
        ],
        compiler_params=_cparams("parallel", "arbitrary"),
        name="diff_attn_sample",
    )(q, k, v, k_cache, v_cache, bias, bias, lam, g)


HGRN_PROJ_HEADS = 1
HGRN_HEADS_PER_STEP = 4
MIN_LOG2 = -150.0


def _hgrn_setup(lb_ref, lvl_ref, ck, layer):
    n_lev = ck.bit_length() - 1
    t_i = lax.broadcasted_iota(jnp.int32, (ck, ck), 0)
    s_i = lax.broadcasted_iota(jnp.int32, (ck, ck), 1)
    x = t_i ^ s_i
    hb = jnp.zeros((ck, ck), jnp.int32)
    for b in range(1, n_lev):
        hb = hb + (x >= (1 << b)).astype(jnp.int32)
    lvl_ref[...] = jnp.where(t_i > s_i, hb, -1)

    lg = lb_ref[...]
    e = jnp.exp(lg - jnp.max(lg, axis=0, keepdims=True))
    p = e / jnp.sum(e, axis=0, keepdims=True)
    cum = p[0:1]
    for d in range(1, layer + 1):
        cum = cum + p[d:d + 1]
    lb = cum - p[0:1]
    return lb, 1.0 - lb, lax.broadcasted_iota(jnp.int32, (ck, LANES), 0)


def _hgrn_chunk(hq, z, v, hg, consts, lvl_ref, ng_ref, st_ref):
    lb, one_mlb, row = consts
    ck = hq.shape[0]
    n_lev = ck.bit_length() - 1
    q = hq * jax.nn.sigmoid(hq)
    r = 1.0 / (1.0 + jnp.exp(z))
    k = one_mlb * r
    f = lb + one_mlb * (1.0 - r)
    g = jnp.maximum(jnp.log2(f), MIN_LOG2)
    vb = v.astype(BF16)

    pf = g
    tot = g
    lvl = lvl_ref[...]
    a = jnp.zeros((ck, ck), F32)
    for lev in range(n_lev):
        hsz = 1 << lev
        kl = (k if lev == 0 else k * jnp.exp2(tot - pf)).astype(BF16)
        if hsz % SUBLANES == 0:
            nb = ck // (2 * hsz)
            late = lambda m: m.reshape(nb, 2, hsz, m.shape[-1])[:, 1:2]
            ql = (late(q) * jnp.exp2(late(pf))).astype(BF16).reshape(ck // 2, LANES)
            p_lev = _dot_nt(ql, kl).reshape(nb, 1, hsz, ck)
            a4 = a.reshape(nb, 2, hsz, ck)
            a = jnp.concatenate([a4[:, 0:1], jnp.where(late(lvl) == lev, p_lev, a4[:, 1:2])],
                                axis=1).reshape(ck, ck)
            t4 = tot.reshape(nb, 2, hsz, LANES)
            p4 = pf.reshape(nb, 2, hsz, LANES)
            both = t4[:, 0:1] + t4[:, 1:2]
            tot = jnp.concatenate([both, both], axis=1).reshape(ck, LANES)
            pf = jnp.concatenate([p4[:, 0:1], p4[:, 1:2] + t4[:, 0:1]], axis=1).reshape(ck, LANES)
        else:
            ql = (q * jnp.exp2(pf)).astype(BF16)
            a = jnp.where(lvl == lev, _dot_nt(ql, kl), a)
            second = (row & hsz) != 0
            t3 = tot.reshape(ck // SUBLANES, SUBLANES, LANES)
            other = pltpu.roll(t3, hsz, 1)
            if 2 * hsz < SUBLANES:
                other = jnp.where(second.reshape(t3.shape), other, pltpu.roll(t3, SUBLANES - hsz, 1))
            other = other.reshape(ck, LANES)
            pf = pf + jnp.where(second, other, 0.0)
            tot = tot + other
    st = st_ref[...]
    o = _dot(a.astype(BF16), vb) + jnp.sum(q * k, axis=1, keepdims=True) * v
    o = o + _dot_nt((q * jnp.exp2(pf)).astype(BF16), st.astype(BF16))
    kst = (k * jnp.exp2(tot - pf)).astype(BF16)
    st_ref[...] = st * jnp.exp2(tot[0:1, :]) + _dot(v.T.astype(BF16), kst)
    y = o * lax.rsqrt(jnp.mean(o * o, axis=-1, keepdims=True) + EPS) * ng_ref[...]
    return y * (hg * jax.nn.sigmoid(hg))


def _hgrn_body(hq_ref, hz_ref, hi_ref, hg_ref, lb_ref, ng_ref, s0_ref, o_ref, s_ref, st_ref, lvl_ref, *, ck,
               layer):
    l = pl.program_id(2)
    tl = hq_ref.shape[0]
    hp = st_ref.shape[0]

    @pl.when(l == 0)
    def _():
        for i in range(hp):
            st_ref[i] = s0_ref[i].T

    lb, one_mlb, row = _hgrn_setup(lb_ref, lvl_ref, ck, layer)

    def chunk(c, carry):
        rs = slice(c * ck, (c + 1) * ck) if isinstance(c, int) else pl.ds(pl.multiple_of(c * ck, ck), ck)
        for i in range(hp):
            cs = slice(i * LANES, (i + 1) * LANES)
            y = _hgrn_chunk(hq_ref[rs, cs], hz_ref[rs, cs], hi_ref[rs, cs], hg_ref[rs, cs],
                            (lb[:, cs], one_mlb[:, cs], row), lvl_ref, ng_ref, st_ref.at[i])
            o_ref[rs, cs] = y.astype(o_ref.dtype)
        return carry

    n_chunks = tl // ck
    if n_chunks == 1:
        chunk(0, 0)
    else:
        lax.fori_loop(0, n_chunks, chunk, 0, unroll=4 if n_chunks % 4 == 0 else 1)

    @pl.when(l == pl.num_programs(2) - 1)
    def _():
        for i in range(hp):
            s_ref[i] = st_ref[i].T


def _hgrn(h, lb_logits, norm_g, s0, layer, n_heads, tl=2048, ck=128):
    B, L, _ = h.shape
    H = n_heads
    tl = _pick(L, tl)
    ck = _pick(tl, ck)
    assert ck & (ck - 1) == 0 and ck >= 8
    depth = lb_logits.shape[0]
    hp = HGRN_HEADS_PER_STEP if (L <= LANES and H % HGRN_HEADS_PER_STEP == 0) else 1
    ng = H // hp
    blk = lambda part: pl.BlockSpec((None, tl, hp * LANES), lambda b, g, l: (b, l, part * ng + g))
    return pl.pallas_call(
        functools.partial(_hgrn_body, ck=ck, layer=layer),
        grid=(B, ng, L // tl),
        in_specs=[
            blk(0), blk(1), blk(2), blk(3),
            pl.BlockSpec((depth, hp * LANES), lambda b, g, l: (0, g)),
            pl.BlockSpec((1, LANES), lambda b, g, l: (0, 0)),
            pl.BlockSpec((None, hp, LANES, LANES), lambda b, g, l: (b, g, 0, 0)),
        ],
        out_specs=[
            pl.BlockSpec((None, tl, hp * LANES), lambda b, g, l: (b, l, g)),
            pl.BlockSpec((None, hp, LANES, LANES), lambda b, g, l: (b, g, 0, 0)),
        ],
        out_shape=[
            jax.ShapeDtypeStruct((B, L, H * LANES), BF16),
            jax.ShapeDtypeStruct((B, H, LANES, LANES), F32),
        ],
        scratch_shapes=[pltpu.VMEM((hp, LANES, LANES), F32), pltpu.VMEM((ck, ck), jnp.int32)],
        compiler_params=_cparams("parallel", "parallel", "arbitrary"),
        name="hgrn_scan",
    )(h, h, h, h, lb_logits, norm_g, s0)


def _hgrn_proj_body(x_ref, w_ref, lb_ref, ng_ref, s0_ref, o_ref, s_ref, st_ref, lvl_ref, h_ref, *, ck, layer,
                    rows):
    L = x_ref.shape[0]
    hp = w_ref.shape[0]
    lb, one_mlb, row = _hgrn_setup(lb_ref, lvl_ref, ck, layer)
    n_proj = L // rows

    def project(i, s):
        h_ref[i, s % 2] = _dot(x_ref[s * rows:(s + 1) * rows, :], w_ref[i])

    project(0, 0)
    for i in range(hp):
        cs = slice(i * LANES, (i + 1) * LANES)
        consts = (lb[:, cs], one_mlb[:, cs], row)
        st = st_ref.at[i]
        st[...] = s0_ref[i].T
        for s in range(n_proj):
            if s + 1 < n_proj:
                project(i, s + 1)
            elif i + 1 < hp:
                project(i + 1, 0)
            h = h_ref.at[i, s % 2]
            for c in range(rows // ck):
                rs = slice(c * ck, (c + 1) * ck)
                y = _hgrn_chunk(h[rs, 0:LANES], h[rs, LANES:2 * LANES], h[rs, 2 * LANES:3 * LANES],
                                h[rs, 3 * LANES:4 * LANES], consts, lvl_ref, ng_ref, st)
                o_ref[s * rows + c * ck:s * rows + (c + 1) * ck, cs] = y.astype(o_ref.dtype)
        s_ref[i] = st[...].T


def _hgrn_proj(x, w_heads, lb_logits, norm_g, s0, layer, ck=128, rows=512):
    B, L, D = x.shape
    H = w_heads.shape[0]
    assert L % rows == 0 and rows % ck == 0 and ck & (ck - 1) == 0
    depth = lb_logits.shape[0]
    hp = HGRN_PROJ_HEADS if H % HGRN_PROJ_HEADS == 0 else 1
    return pl.pallas_call(
        functools.partial(_hgrn_proj_body, ck=ck, layer=layer, rows=rows),
        grid=(B, H // hp),
        in_specs=[
            pl.BlockSpec((None, L, D), lambda b, g: (b, 0, 0)),
            pl.BlockSpec((hp, D, 4 * LANES), lambda b, g: (g, 0, 0)),
            pl.BlockSpec((depth, hp * LANES), lambda b, g: (0, g)),
            pl.BlockSpec((1, LANES), lambda b, g: (0, 0)),
            pl.BlockSpec((None, hp, LANES, LANES), lambda b, g: (b, g, 0, 0)),
        ],
        out_specs=[
            pl.BlockSpec((None, L, hp * LANES), lambda b, g: (b, 0, g)),
            pl.BlockSpec((None, hp, LANES, LANES), lambda b, g: (b, g, 0, 0)),
        ],
        out_shape=[
            jax.ShapeDtypeStruct((B, L, H * LANES), BF16),
            jax.ShapeDtypeStruct((B, H, LANES, LANES), F32),
        ],
        scratch_shapes=[
            pltpu.VMEM((hp, LANES, LANES), F32),
            pltpu.VMEM((ck, ck), jnp.int32),
            pltpu.VMEM((hp, 2, rows, 4 * LANES), F32),
        ],
        compiler_params=_cparams("parallel", "arbitrary"),
        name="hgrn_proj_scan",
    )(x, w_heads, lb_logits, norm_g, s0)


def _trunk(x, p, k_cache, v_cache, pool_hist, hg_state, W, attn_prep):
    B, L, D = x.shape
    depth = W["ln_g"].shape[0]
    alpha = (2 * depth) ** 0.25
    T = B * L
    xt = x.reshape(T, D)
    new_k, new_v, new_pool, new_s = [], [], [], []
    pool_width = W["pool_scale"].shape[-1]
    da_width = D - pool_width
    n_da_heads = da_width // LANES
    n_hg_heads = D // LANES
    lnrow = lambda a, i, s: a[i, s].reshape(1, D)
    for i in range(depth):
        fuse_proj = i % 2 == 1 and L % HGRN_PROJ_ROWS == 0
        want16 = fuse_proj or i % 2 == 0
        xt = _ffn_ln(xt, W["wg"], W["wu"], W["wd"], lnrow(W["ln_g"], i, 0), lnrow(W["ln_b"], i, 0), i, 0, alpha,
                     also_bf16=want16)
        if want16:
            xt, xt16 = xt
        if i % 2 == 0:
            e = i // 2
            w_in = W["w_in_even"]
            (u,) = _proj(xt16, w_in, e, 0, pool_width, B, L, ("f32",))
            (q16,) = _proj(xt16, w_in, e, pool_width, da_width, B, L, ("bf16",),
                           scale=(LANES // 2) ** -0.5 * LOG2E)
            k, k16 = _proj(xt16, w_in, e, pool_width + da_width, da_width, B, L, ("heads", "bf16"))
            v, v16 = _proj(xt16, w_in, e, pool_width + 2 * da_width, da_width, B, L, ("heads", "bf16"))
            past = 0 if k_cache is None else k_cache.shape[2]
            pool_out, nh = _pool(u.reshape(B, L, pool_width), pool_hist[e], W["pool_w"][e],
                                 W["pool_scale"][e].reshape(1, pool_width), past)
            bias, lam = attn_prep[e]
            lam_init = 0.8 - 0.6 * math.exp(-0.3 * i)
            g = W["diff_norm_g"][e].reshape(1, LANES)
            q3, k3, v3 = (a.reshape(B, L, da_width) for a in (q16, k16, v16))
            if k_cache is None:
                o = _attn_prompt(q3, k3, v3, bias, lam, g, 1.0 - lam_init)
            else:
                kc = jnp.transpose(k_cache[e], (0, 2, 1, 3))
                vc = jnp.transpose(v_cache[e], (0, 2, 1, 3))
                o = _attn_sample(q3, k3, v3, kc, vc, bias, lam, g, 1.0 - lam_init)
            parts = [pool_out.reshape(T, pool_width), o.reshape(T, da_width)]
            w_out = W["w_out_even"]
            new_k.append(jnp.transpose(k, (0, 2, 1, 3)))
            new_v.append(jnp.transpose(v, (0, 2, 1, 3)))
            new_pool.append(nh)
            li = e
        else:
            od = i // 2
            ng = W["hgrn_norm_g"][od].reshape(1, LANES)
            if fuse_proj:
                o, s = _hgrn_proj(xt16.reshape(B, L, D), W["w_in_odd_heads"][od], W["lb_logits"], ng,
                                  hg_state[od], i, rows=HGRN_PROJ_ROWS)
            else:
                h = _matmul(xt, W["w_in_odd"], od)
                o, s = _hgrn(h.reshape(B, L, 4 * D), W["lb_logits"], ng, hg_state[od], i, n_hg_heads)
            parts = [o.reshape(T, D)]
            w_out = W["w_out_odd"]
            new_s.append(s)
            li = od
        xt = _out_ln(parts, xt, w_out, li, lnrow(W["ln_g"], i, 1), lnrow(W["ln_b"], i, 1), alpha)
        xt = _ffn_ln(xt, W["wg"], W["wu"], W["wd"], lnrow(W["ln_g"], i, 2), lnrow(W["ln_b"], i, 2), i, 1, alpha)
        xt = _ple(xt, p.reshape(depth, T, -1), W["w_ple_gate"], W["w_ple_up"], i)
    return xt.reshape(B, L, D), jnp.stack(new_k), jnp.stack(new_v), jnp.stack(new_pool), jnp.stack(new_s)


ATTN_TILE = 256
HGRN_PROJ_ROWS = 512


def kernel(x_prompt, x_sample, cache_diff_k, cache_diff_v, state_pool, state_hgrn, p_prompt, p_sample, ln_g, ln_b, w_ffn_gate, w_ffn_up, w_ffn_down, w_ple_gate, w_ple_up, w_in_even, w_out_even, pool_w, pool_scale, lam_q1, lam_k1, lam_q2, lam_k2, diff_norm_g, rel_bias, w_in_odd, w_out_odd, hgrn_norm_g, hgrn_lb_logits):
    bf = lambda a: a.astype(BF16)
    W = dict(
        ln_g=ln_g.astype(F32), ln_b=ln_b.astype(F32),
        wg=bf(w_ffn_gate), wu=bf(w_ffn_up), wd=bf(w_ffn_down),
        w_ple_gate=bf(w_ple_gate), w_ple_up=bf(w_ple_up),
        w_in_even=bf(w_in_even), w_out_even=bf(w_out_even),
        pool_w=bf(pool_w), pool_scale=pool_scale.astype(F32),
        diff_norm_g=diff_norm_g.astype(F32),
        w_in_odd=bf(w_in_odd), w_out_odd=bf(w_out_odd),
        hgrn_norm_g=hgrn_norm_g.astype(F32), lb_logits=hgrn_lb_logits.astype(F32),
    )
    n_even = w_in_even.shape[0]
    n_odd = w_in_odd.shape[0]
    d_model = w_in_odd.shape[1]
    n_hg = d_model // LANES
    W["w_in_odd_heads"] = jnp.transpose(W["w_in_odd"].reshape(n_odd, d_model, 4, n_hg, LANES),
                                        (0, 3, 1, 2, 4)).reshape(n_odd, n_hg, d_model, 4 * LANES)
    attn_prep = []
    for e in range(n_even):
        lam_init = 0.8 - 0.6 * math.exp(-0.3 * (2 * e))
        attn_prep.append(_attn_prep(rel_bias, lam_q1[e], lam_k1[e], lam_q2[e], lam_k2[e], ATTN_TILE, lam_init))

    B = x_prompt.shape[0]
    dt = x_prompt.dtype
    zero_pool = jnp.zeros((n_even, B) + state_pool.shape[2:], dt)
    zero_s = jnp.zeros((n_odd, B) + state_hgrn.shape[2:], dt)
    y_p, k_p, v_p, pool_p, s_p = _trunk(x_prompt, p_prompt, None, None, zero_pool, zero_s, W, attn_prep)
    y_s, k_s, v_s, pool_s, s_s = _trunk(x_sample, p_sample, cache_diff_k, cache_diff_v, state_pool, state_hgrn,
                                        W, attn_prep)
    return (y_p, y_s, k_p, v_p, k_s, v_s, pool_p, pool_s, s_p, s_s)
```

```python
import functools
import math

import jax
import jax.numpy as jnp
from jax import lax
from jax.experimental import pallas as pl
from jax.experimental.pallas import tpu as pltpu

F32 = jnp.float32
BF16 = jnp.bfloat16

CHUNK = 64
POOL_WINDOWS = (2, 4, 8, 16)
POOL_HIST = max(POOL_WINDOWS) - 1
N_BUCKETS = 32
MAX_DISTANCE = 128
EPS = 1e-5
NEG = -1e30
LANES = 128
SUBLANES = 8

VMEM_LIMIT = 56 * 1024 * 1024


def _cparams(*sem):
    return pltpu.CompilerParams(dimension_semantics=sem, vmem_limit_bytes=VMEM_LIMIT)


def _pick(n, pref):
    if n <= pref:
        return n
    t = pref
    while n % t:
        t //= 2
    return t


def _layer_norm(y, g, b):
    mu = jnp.mean(y, axis=-1, keepdims=True)
    d = y - mu
    var = jnp.mean(d * d, axis=-1, keepdims=True)
    return d * lax.rsqrt(var + EPS) * g + b


def _dot(a, b):
    return jnp.dot(a, b, preferred_element_type=F32)


def _dot_nt(a, b):
    return lax.dot_general(a, b, (((1,), (1,)), ((), ())), preferred_element_type=F32)


FFN_COL_CHUNK = 512
MXU_COLS = 2 * LANES


def _ffn_act_body(x_ref, wg_ref, wu_ref, a_ref, xb_ref):
    @pl.when(pl.program_id(1) == 0)
    def _():
        xb_ref[...] = x_ref[...].astype(BF16)

    def gated(hg, hu):
        return (hg * jax.nn.sigmoid(hg) * hu).astype(a_ref.dtype)

    xb = xb_ref[...]
    tf = a_ref.shape[1]
    main = tf // MXU_COLS * MXU_COLS
    for c0 in range(0, main, FFN_COL_CHUNK):
        cs = slice(c0, min(c0 + FFN_COL_CHUNK, main))
        a_ref[:, cs] = gated(_dot(xb, wg_ref[:, cs]), _dot(xb, wu_ref[:, cs]))
    if main < tf:
        assert tf - main == LANES
        h = _dot(xb, jnp.concatenate([wg_ref[:, main:tf], wu_ref[:, main:tf]], axis=1))
        a_ref[:, main:tf] = gated(h[:, :LANES], h[:, LANES:])


def _ffn_down_ln_body(a_ref, x_ref, wd_ref, g_ref, b_ref, *o_refs, alpha):
    tm = x_ref.shape[0]
    hm = tm // 2
    for rows in (slice(0, hm), slice(hm, tm)):
        y = alpha * x_ref[rows, :] + 0.5 * _dot(a_ref[rows, :], wd_ref[...])
        y = _layer_norm(y, g_ref[...], b_ref[...])
        for o_ref in o_refs:
            o_ref[rows, :] = y.astype(o_ref.dtype)


def _ffn_down_ln_ple_body(a_ref, x_ref, wd_ref, g_ref, b_ref, p_ref, pg_ref, pu_ref, o_ref, *, alpha):
    y = alpha * x_ref[...] + 0.5 * _dot(a_ref[...], wd_ref[...])
    y = _layer_norm(y, g_ref[...], b_ref[...])
    gate = jax.nn.sigmoid(_dot(y.astype(BF16), pg_ref[...]))
    o_ref[...] = y + gate * _dot(p_ref[...].astype(BF16), pu_ref[...])


def _ffn_ln(x, wg, wu, wd, g, b, li, si, alpha, also_bf16=False, ple=None, tm_act=1024, tf=1408, tm_down=256):
    T, D = x.shape
    F = wd.shape[-2]
    tm = _pick(T, tm_act)
    tf = tf if F % tf == 0 else _pick(F, 512)
    act = pl.pallas_call(
        _ffn_act_body,
        grid=(T // tm, F // tf),
        in_specs=[
            pl.BlockSpec((tm, D), lambda t, j: (t, 0)),
            pl.BlockSpec((None, None, D, tf), lambda t, j: (li, si, 0, j)),
            pl.BlockSpec((None, None, D, tf), lambda t, j: (li, si, 0, j)),
        ],
        out_specs=pl.BlockSpec((tm, tf), lambda t, j: (t, j)),
        out_shape=jax.ShapeDtypeStruct((T, F), BF16),
        scratch_shapes=[pltpu.VMEM((tm, D), BF16)],
        compiler_params=_cparams("parallel", "arbitrary"),
        name="ffn_act",
    )(x, wg, wu)

    tm = _pick(T, tm_down)
    if ple is not None:
        p, pg, pu = ple
        P = p.shape[-1]
        return pl.pallas_call(
            functools.partial(_ffn_down_ln_ple_body, alpha=alpha),
            grid=(T // tm,),
            in_specs=[
                pl.BlockSpec((tm, F), lambda t: (t, 0)),
                pl.BlockSpec((tm, D), lambda t: (t, 0)),
                pl.BlockSpec((None, None, F, D), lambda t: (li, si, 0, 0), pipeline_mode=pl.Buffered(1)),
                pl.BlockSpec((1, D), lambda t: (0, 0)),
                pl.BlockSpec((1, D), lambda t: (0, 0)),
                pl.BlockSpec((None, tm, P), lambda t: (li, t, 0)),
                pl.BlockSpec((None, D, D), lambda t: (li, 0, 0), pipeline_mode=pl.Buffered(1)),
                pl.BlockSpec((None, P, D), lambda t: (li, 0, 0), pipeline_mode=pl.Buffered(1)),
            ],
            out_specs=pl.BlockSpec((tm, D), lambda t: (t, 0)),
            out_shape=jax.ShapeDtypeStruct((T, D), F32),
            compiler_params=_cparams("parallel"),
            name="ffn_down_ln_ple",
        )(act, x, wd, g, b, p, pg, pu)
    out_dtypes = (F32, BF16) if also_bf16 else (F32,)
    outs = pl.pallas_call(
        functools.partial(_ffn_down_ln_body, alpha=alpha),
        grid=(T // tm,),
        in_specs=[
            pl.BlockSpec((tm, F), lambda t: (t, 0)),
            pl.BlockSpec((tm, D), lambda t: (t, 0)),
            pl.BlockSpec((None, None, F, D), lambda t: (li, si, 0, 0), pipeline_mode=pl.Buffered(1)),
            pl.BlockSpec((1, D), lambda t: (0, 0)),
            pl.BlockSpec((1, D), lambda t: (0, 0)),
        ],
        out_specs=[pl.BlockSpec((tm, D), lambda t: (t, 0)) for _ in out_dtypes],
        out_shape=[jax.ShapeDtypeStruct((T, D), dt) for dt in out_dtypes],
        compiler_params=_cparams("parallel"),
        name="ffn_down_ln",
    )(act, x, wd, g, b)
    return tuple(outs) if also_bf16 else outs[0]


def _matmul_body(x_ref, w_ref, o_ref, xb_ref):
    @pl.when(pl.program_id(1) == 0)
    def _():
        xb_ref[...] = x_ref[...].astype(BF16)

    o_ref[...] = _dot(xb_ref[...], w_ref[...]).astype(o_ref.dtype)


def _matmul(x, w, li, out_dtype=F32, tm=1024, tn=1024):
    T, K = x.shape
    N = w.shape[-1]
    tm = _pick(T, tm)
    tn = _pick(N, tn)
    return pl.pallas_call(
        _matmul_body,
        grid=(T // tm, N // tn),
        in_specs=[
            pl.BlockSpec((tm, K), lambda t, j: (t, 0)),
            pl.BlockSpec((None, K, tn), lambda t, j: (li, 0, j)),
        ],
        out_specs=pl.BlockSpec((tm, tn), lambda t, j: (t, j)),
        out_shape=jax.ShapeDtypeStruct((T, N), out_dtype),
        scratch_shapes=[pltpu.VMEM((tm, K), BF16)],
        compiler_params=_cparams("parallel", "arbitrary"),
        name="in_proj",
    )(x, w)


def _proj_body(x_ref, w_ref, *o_refs, scale):
    r = _dot(x_ref[...], w_ref[...])
    for o_ref in o_refs:
        if len(o_ref.shape) == 4:
            bb, hpt, tl, _ = o_ref.shape
            for hh in range(hpt):
                o_ref[:, hh] = r[:, hh * LANES:(hh + 1) * LANES].reshape(bb, tl, LANES)
        elif o_ref.dtype == BF16:
            o_ref[...] = (r * scale).astype(BF16)
        else:
            o_ref[...] = r


def _proj(x16, w, li, col0, width, B, L, kinds, scale=1.0, tm=2048, tn=512):
    T, K = x16.shape
    tm = _pick(T, tm)
    assert width % tn == 0 and col0 % tn == 0 and tn % LANES == 0
    hpt = tn // LANES
    tl = min(L, tm)
    assert tm % tl == 0 and L % tl == 0
    bb, nl = tm // tl, L // tl
    specs, shapes = [], []
    for kind in kinds:
        if kind == "heads":
            specs.append(pl.BlockSpec((bb, hpt, tl, LANES), lambda t, j: (t // nl, j, t % nl, 0)))
            shapes.append(jax.ShapeDtypeStruct((B, width // LANES, L, LANES), F32))
        else:
            specs.append(pl.BlockSpec((tm, tn), lambda t, j: (t, j)))
            shapes.append(jax.ShapeDtypeStruct((T, width), BF16 if kind == "bf16" else F32))
    return pl.pallas_call(
        functools.partial(_proj_body, scale=scale),
        grid=(T // tm, width // tn),
        in_specs=[
            pl.BlockSpec((tm, K), lambda t, j: (t, 0)),
            pl.BlockSpec((None, K, tn), lambda t, j: (li, 0, col0 // tn + j)),
        ],
        out_specs=specs,
        out_shape=shapes,
        compiler_params=_cparams("parallel", "arbitrary"),
        name="in_proj_cols",
    )(x16, w)


def _out_ln_body(*refs, alpha, widths):
    n = len(widths)
    parts = refs[:n]
    x_ref, w_ref, g_ref, b_ref, o_ref = refs[n:]
    tm = x_ref.shape[0]
    n_split = 2 if tm % 16 == 0 else 1
    hm = tm // n_split
    for h in range(n_split):
        rows = slice(h * hm, (h + 1) * hm)
        acc = alpha * x_ref[rows, :]
        off = 0
        for p_ref, wd in zip(parts, widths):
            acc = acc + _dot(p_ref[rows, :], w_ref[off:off + wd, :])
            off += wd
        o_ref[rows, :] = _layer_norm(acc, g_ref[...], b_ref[...])


def _out_ln(parts, x, w, li, g, b, alpha, tm=512):
    T, D = x.shape
    tm = _pick(T, tm)
    widths = tuple(p.shape[1] for p in parts)
    kin = sum(widths)
    return pl.pallas_call(
        functools.partial(_out_ln_body, alpha=alpha, widths=widths),
        grid=(T // tm,),
        in_specs=[pl.BlockSpec((tm, wd), lambda t: (t, 0)) for wd in widths] + [
            pl.BlockSpec((tm, D), lambda t: (t, 0)),
            pl.BlockSpec((None, kin, D), lambda t: (li, 0, 0)),
            pl.BlockSpec((1, D), lambda t: (0, 0)),
            pl.BlockSpec((1, D), lambda t: (0, 0)),
        ],
        out_specs=pl.BlockSpec((tm, D), lambda t: (t, 0)),
        out_shape=jax.ShapeDtypeStruct((T, D), F32),
        compiler_params=_cparams("parallel"),
        name="out_proj_ln",
    )(*parts, x, w, g, b)


def _ple_body(x_ref, p_ref, wg_ref, wu_ref, o_ref, *, tn):
    xb = x_ref[...].astype(BF16)
    pb = p_ref[...].astype(BF16)
    D = o_ref.shape[1]
    for c in range(D // tn):
        sl = slice(c * tn, (c + 1) * tn)
        gate = jax.nn.sigmoid(_dot(xb, wg_ref[:, sl]))
        up = _dot(pb, wu_ref[:, sl])
        o_ref[:, sl] = x_ref[:, sl] + gate * up


def _ple(x, p, wg, wu, li, tm=512, tn=512):
    T, D = x.shape
    P = p.shape[2]
    tm = _pick(T, tm)
    return pl.pallas_call(
        functools.partial(_ple_body, tn=_pick(D, tn)),
        grid=(T // tm,),
        in_specs=[
            pl.BlockSpec((tm, D), lambda t: (t, 0)),
            pl.BlockSpec((None, tm, P), lambda t: (li, t, 0)),
            pl.BlockSpec((None, D, D), lambda t: (li, 0, 0)),
            pl.BlockSpec((None, P, D), lambda t: (li, 0, 0)),
        ],
        out_specs=pl.BlockSpec((tm, D), lambda t: (t, 0)),
        out_shape=jax.ShapeDtypeStruct((T, D), F32),
        compiler_params=_cparams("parallel"),
        name="ple_gate",
    )(x, p, wg, wu)


def _pool_body(u_ref, hist_ref, w_ref, sc_ref, o_ref, nh_ref, ext_ref, *, tl, start_pos):
    l = pl.program_id(1)
    nl = pl.num_programs(1)
    H = POOL_HIST + 1

    @pl.when(l == 0)
    def _():
        ext_ref[0:1, :] = jnp.zeros((1, ext_ref.shape[1]), F32)
        ext_ref[1:H, :] = hist_ref[...]

    @pl.when(l > 0)
    def _():
        ext_ref[0:H, :] = ext_ref[tl:tl + H, :]

    ext_ref[H:H + tl, :] = u_ref[...]

    pos = start_pos + l * tl + lax.broadcasted_iota(jnp.int32, (tl, 1), 0)
    gd = LANES
    for g, wnd in enumerate(POOL_WINDOWS):
        cs = slice(g * gd, (g + 1) * gd)
        s = ext_ref[H:H + tl, cs]
        cur = s
        for d in range(1, wnd):
            s = s + ext_ref[H - d:H - d + tl, cs]
        cnt = jnp.minimum(pos + 1, wnd).astype(F32)
        pooled = s / cnt - cur
        y = _dot(pooled.astype(BF16), w_ref[g]) * sc_ref[:, cs]
        o_ref[:, cs] = y.astype(o_ref.dtype)

    @pl.when(l == nl - 1)
    def _():
        nh_ref[...] = ext_ref[tl + 1:tl + H, :]


def _pool(u, hist, w, scale, start_pos, tl=512):
    B, L, PW = u.shape
    tl = _pick(L, tl)
    assert tl >= POOL_HIST + 1
    return pl.pallas_call(
        functools.partial(_pool_body, tl=tl, start_pos=start_pos),
        grid=(B, L // tl),
        in_specs=[
            pl.BlockSpec((None, tl, PW), lambda b, l: (b, l, 0)),
            pl.BlockSpec((None, POOL_HIST, PW), lambda b, l: (b, 0, 0)),
            pl.BlockSpec(w.shape, lambda b, l: (0, 0, 0)),
            pl.BlockSpec((1, PW), lambda b, l: (0, 0)),
        ],
        out_specs=[
            pl.BlockSpec((None, tl, PW), lambda b, l: (b, l, 0)),
            pl.BlockSpec((None, POOL_HIST, PW), lambda b, l: (b, 0, 0)),
        ],
        out_shape=[
            jax.ShapeDtypeStruct((B, L, PW), BF16),
            jax.ShapeDtypeStruct((B, POOL_HIST, PW), F32),
        ],
        scratch_shapes=[pltpu.VMEM((tl + POOL_HIST + 1, PW), F32)],
        compiler_params=_cparams("parallel", "arbitrary"),
        name="pool_mixer",
    )(u, hist, w, scale)


def _t5_bucket(rel):
    nb = N_BUCKETS // 2
    max_exact = nb // 2
    n = jnp.abs(rel)
    nf = jnp.maximum(n, 1).astype(jnp.float32)
    large = max_exact + (jnp.log(nf / max_exact) / math.log(MAX_DISTANCE / max_exact)
                         * (nb - max_exact)).astype(jnp.int32)
    large = jnp.minimum(large, nb - 1)
    return jnp.where(rel > 0, nb, 0) + jnp.where(n < max_exact, n, large)


def _attn_prep_body(tbl_ref, bkt_ref, lq1_ref, lk1_ref, lq2_ref, lk2_ref, bias_ref, lam_ref, *, tq, lam_init,
                    far_bucket):
    h = pl.program_id(0)
    bkt = bkt_ref[...]
    far = tbl_ref[far_bucket, h]
    acc = jnp.zeros(bkt.shape, F32)
    for b in range(N_BUCKETS):
        acc = jnp.where(bkt == b, tbl_ref[b, h] - far, acc)
    r = lax.broadcasted_iota(jnp.int32, bkt.shape, 1)
    c = lax.broadcasted_iota(jnp.int32, bkt.shape, 2)
    t = lax.broadcasted_iota(jnp.int32, bkt.shape, 0)
    visible = (t == 0) | ((c // CHUNK) <= (r // CHUNK))
    bias_ref[...] = jnp.where(visible, acc, NEG)
    e1 = jnp.exp(jnp.sum(lq1_ref[...] * lk1_ref[...], axis=-1, keepdims=True))
    e2 = jnp.exp(jnp.sum(lq2_ref[...] * lk2_ref[...], axis=-1, keepdims=True))
    lam_ref[...] = jnp.broadcast_to(e1 - e2 + lam_init, lam_ref.shape)


def _attn_prep(rel_bias, lq1, lk1, lq2, lk2, tq, lam_init):
    nbk, H = rel_bias.shape
    r = jnp.arange(tq, dtype=jnp.int32)[:, None]
    c = jnp.arange(tq, dtype=jnp.int32)[None, :]
    bkt = jnp.stack([_t5_bucket(c - r - tq), _t5_bucket(c - r)])
    far_bucket = N_BUCKETS // 2 - 1
    assert tq >= MAX_DISTANCE
    row = lambda a: a.reshape(1, -1).astype(F32)
    return pl.pallas_call(
        functools.partial(_attn_prep_body, tq=tq, lam_init=lam_init, far_bucket=far_bucket),
        grid=(H,),
        in_specs=[
            pl.BlockSpec(memory_space=pltpu.SMEM),
            pl.BlockSpec((2, tq, tq), lambda h: (0, 0, 0)),
            pl.BlockSpec((1, lq1.shape[-1]), lambda h: (0, 0)),
            pl.BlockSpec((1, lq1.shape[-1]), lambda h: (0, 0)),
            pl.BlockSpec((1, lq1.shape[-1]), lambda h: (0, 0)),
            pl.BlockSpec((1, lq1.shape[-1]), lambda h: (0, 0)),
        ],
        out_specs=[
            pl.BlockSpec((None, 2, tq, tq), lambda h: (h, 0, 0, 0)),
            pl.BlockSpec((8, LANES), lambda h: (0, 0)),
        ],
        out_shape=[
            jax.ShapeDtypeStruct((H, 2, tq, tq), F32),
            jax.ShapeDtypeStruct((8, LANES), F32),
        ],
        compiler_params=_cparams("arbitrary"),
        name="attn_prep",
    )(rel_bias.astype(F32), bkt, row(lq1), row(lk1), row(lq2), row(lk2))


def _split_q(q):
    lane = lax.broadcasted_iota(jnp.int32, q.shape, 1)
    half = q.shape[1] // 2
    zero = jnp.zeros_like(q)
    return jnp.concatenate([jnp.where(lane < half, q, zero), jnp.where(lane >= half, q, zero)], axis=0)


def _attend(qq, spans, s_ref, mx_ref, l_ref, acc_ref):
    rows = qq.shape[0]
    mx_ref[...] = jnp.full(mx_ref.shape, NEG, F32)
    for col, get_k, _, bias in spans:
        s = _dot_nt(qq, get_k())
        w = s.shape[1]
        if bias is not None:
            s = (s.reshape(2, rows // 2, w) + bias[None]).reshape(rows, w)
        s_ref[:, col:col + w] = s
        if w % LANES == 0:
            m = functools.reduce(jnp.maximum, [s[:, c:c + LANES] for c in range(0, w, LANES)])
            mx_ref[...] = jnp.maximum(mx_ref[...], m)
        else:
            mx_ref[:, 0:w] = jnp.maximum(mx_ref[:, 0:w], s)
    m_b = jnp.broadcast_to(jnp.max(mx_ref[...], axis=1, keepdims=True), mx_ref.shape)
    mx_ref[...] = m_b
    l_ref[...] = jnp.zeros_like(l_ref)
    acc_ref[...] = jnp.zeros_like(acc_ref)
    for col, get_k, get_v, _ in spans:
        w = get_v().shape[0]
        s = s_ref[:, col:col + w]
        m_b = mx_ref[...]
        if w % LANES == 0:
            ps = [jnp.exp(s[:, c:c + LANES] - m_b) for c in range(0, w, LANES)]
            l_ref[...] += functools.reduce(jnp.add, ps)
            p = ps[0] if len(ps) == 1 else jnp.concatenate(ps, axis=1)
        else:
            p = jnp.exp(s - m_b[:, 0:w])
            l_ref[:, 0:w] += p
        acc_ref[...] += _dot(p.astype(BF16), get_v())


def _attn_finish(lam_ref, g_ref, l_ref, acc_ref, tq, out_scale):
    lam = lam_ref[0:1, 0:1]
    o = acc_ref[...] / jnp.sum(l_ref[...], axis=1, keepdims=True)
    o = o[:tq] - lam * o[tq:]
    return o * lax.rsqrt(jnp.mean(o * o, axis=-1, keepdims=True) + EPS) * g_ref[...] * out_scale


ATTN_KEY_SPAN = 512
ATTN_SAMPLE_HEADS = 4


def _attn_prompt_body(q_ref, k_ref, v_ref, bias_ref, lam_ref, g_ref, o_ref, s_ref, mx_ref, l_ref, acc_ref, *,
                      tq, out_scale):
    L = q_ref.shape[0]

    def span(st, w, bias):
        return (st, lambda: k_ref[st:st + w, :], lambda: v_ref[st:st + w, :], bias)

    for qi in reversed(range(L // tq)):
        par = qi % 2
        far_end = max(qi - 1, 0) * tq
        spans = [span(st, min(ATTN_KEY_SPAN, far_end - st), None) for st in range(0, far_end, ATTN_KEY_SPAN)]
        if qi >= 1:
            spans.append(span((qi - 1) * tq, tq, bias_ref[0]))
        spans.append(span(qi * tq, tq, bias_ref[1]))
        qq = _split_q(q_ref[qi * tq:(qi + 1) * tq, :])
        _attend(qq, spans, s_ref.at[par], mx_ref.at[par], l_ref.at[par], acc_ref.at[par])
        y = _attn_finish(lam_ref, g_ref, l_ref.at[par], acc_ref.at[par], tq, out_scale)
        o_ref[qi * tq:(qi + 1) * tq, :] = y.astype(o_ref.dtype)


def _attn_prompt(q, k, v, bias, lam, g, out_scale):
    B, L, _ = q.shape
    H = q.shape[2] // LANES
    tq = bias.shape[-1]
    assert L % tq == 0 and tq % CHUNK == 0
    return pl.pallas_call(
        functools.partial(_attn_prompt_body, tq=tq, out_scale=out_scale),
        grid=(B, H),
        in_specs=[
            pl.BlockSpec((None, L, LANES), lambda b, h: (b, 0, h)),
            pl.BlockSpec((None, L, LANES), lambda b, h: (b, 0, h)),
            pl.BlockSpec((None, L, LANES), lambda b, h: (b, 0, h)),
            pl.BlockSpec((None, 2, tq, tq), lambda b, h: (h, 0, 0, 0)),
            pl.BlockSpec((8, LANES), lambda b, h: (0, 0)),
            pl.BlockSpec((1, LANES), lambda b, h: (0, 0)),
        ],
        out_specs=pl.BlockSpec((None, L, LANES), lambda b, h: (b, 0, h)),
        out_shape=jax.ShapeDtypeStruct((B, L, H * LANES), BF16),
        scratch_shapes=[
            pltpu.VMEM((2, 2 * tq, L), F32),
            pltpu.VMEM((2, 2 * tq, LANES), F32),
            pltpu.VMEM((2, 2 * tq, LANES), F32),
            pltpu.VMEM((2, 2 * tq, LANES), F32),
        ],
        compiler_params=_cparams("parallel", "parallel"),
        name="diff_attn_prompt",
    )(q, k, v, bias, lam, g)


def _attn_sample_body(q_ref, kn_ref, vn_ref, kc_ref, vc_ref, bprev_ref, bdiag_ref, lam_ref, g_ref, o_ref,
                      s_ref, mx_ref, l_ref, acc_ref, *, lq, tk, out_scale):
    hp, P = kc_ref.shape[0], kc_ref.shape[1]
    near = P - tk
    for i in range(hp):
        cs = slice(i * LANES, (i + 1) * LANES)

        def span(st, w, bias, i=i):
            return (st, lambda: kc_ref[i, st:st + w, :].astype(BF16),
                    lambda: vc_ref[i, st:st + w, :].astype(BF16), bias)

        spans = [span(st, min(ATTN_KEY_SPAN, near - st), None) for st in range(0, near, ATTN_KEY_SPAN)]
        spans.append(span(near, tk, bprev_ref[i]))
        spans.append((P, lambda cs=cs: kn_ref[:, cs], lambda cs=cs: vn_ref[:, cs], bdiag_ref[i, :, 0:lq]))
        _attend(_split_q(q_ref[:, cs]), spans, s_ref.at[i], mx_ref.at[i], l_ref.at[i], acc_ref.at[i])
        y = _attn_finish(lam_ref, g_ref, l_ref.at[i], acc_ref.at[i], lq, out_scale)
        o_ref[:, cs] = y.astype(o_ref.dtype)


def _attn_sample(q, k, v, k_cache, v_cache, bias, lam, g, out_scale):
    B, lq, _ = q.shape
    H = q.shape[2] // LANES
    P = k_cache.shape[2]
    tk = bias.shape[-1]
    assert lq == CHUNK and P % tk == 0 and P % CHUNK == 0 and lq <= tk
    hp = ATTN_SAMPLE_HEADS if H % ATTN_SAMPLE_HEADS == 0 else 1
    ng = H // hp
    return pl.pallas_call(
        functools.partial(_attn_sample_body, lq=lq, tk=tk, out_scale=out_scale),
        grid=(B, ng),
        in_specs=[
            pl.BlockSpec((None, lq, hp * LANES), lambda b, g: (b, 0, g)),
            pl.BlockSpec((None, lq, hp * LANES), lambda b, g: (b, 0, g)),
            pl.BlockSpec((None, lq, hp * LANES), lambda b, g: (b, 0, g)),
            pl.BlockSpec((None, hp, P, LANES), lambda b, g: (b, g, 0, 0)),
            pl.BlockSpec((None, hp, P, LANES), lambda b, g: (b, g, 0, 0)),
            pl.BlockSpec((hp, None, lq, tk), lambda b, g: (g, 0, 0, 0)),
            pl.BlockSpec((hp, None, lq, tk), lambda b, g: (g, 1, 0, 0)),
            pl.BlockSpec((8, LANES), lambda b, g: (0, 0)),
            pl.BlockSpec((1, LANES), lambda b, g: (0, 0)),
        ],
        out_specs=pl.BlockSpec((None, lq, hp * LANES), lambda b, g: (b, 0, g)),
        out_shape=jax.ShapeDtypeStruct((B, lq, H * LANES), BF16),
        scratch_shapes=[
            pltpu.VMEM((hp, 2 * lq, P + LANES), F32),
            pltpu.VMEM((hp, 2 * lq, LANES), F32),
            pltpu.VMEM((hp, 2 * lq, LANES), F32),
            pltpu.VMEM((hp, 2 * lq, LANES), F32),
        ],
        compiler_params=_cparams("parallel", "arbitrary"),
        name="diff_attn_sample",
    )(q, k, v, k_cache, v_cache, bias, bias, lam, g)


HGRN_PROJ_HEADS = 1
HGRN_HEADS_PER_STEP = 4
MIN_LOG2 = -150.0


def _hgrn_setup(lb_ref, lvl_ref, ck, layer):
    n_lev = ck.bit_length() - 1
    t_i = lax.broadcasted_iota(jnp.int32, (ck, ck), 0)
    s_i = lax.broadcasted_iota(jnp.int32, (ck, ck), 1)
    x = t_i ^ s_i
    hb = jnp.zeros((ck, ck), jnp.int32)
    for b in range(1, n_lev):
        hb = hb + (x >= (1 << b)).astype(jnp.int32)
    lvl_ref[...] = jnp.where(t_i > s_i, hb, -1)

    lg = lb_ref[...]
    e = jnp.exp(lg - jnp.max(lg, axis=0, keepdims=True))
    p = e / jnp.sum(e, axis=0, keepdims=True)
    cum = p[0:1]
    for d in range(1, layer + 1):
        cum = cum + p[d:d + 1]
    lb = cum - p[0:1]
    return lb, 1.0 - lb, lax.broadcasted_iota(jnp.int32, (ck, LANES), 0)


def _hgrn_chunk(hq, z, v, hg, consts, lvl_ref, ng_ref, st_ref):
    lb, one_mlb, row = consts
    ck = hq.shape[0]
    n_lev = ck.bit_length() - 1
    q = hq * jax.nn.sigmoid(hq)
    r = 1.0 / (1.0 + jnp.exp(z))
    k = one_mlb * r
    f = lb + one_mlb * (1.0 - r)
    g = jnp.maximum(jnp.log2(f), MIN_LOG2)
    vb = v.astype(BF16)

    pf = g
    tot = g
    lvl = lvl_ref[...]
    a = jnp.zeros((ck, ck), F32)
    for lev in range(n_lev):
        hsz = 1 << lev
        kl = (k if lev == 0 else k * jnp.exp2(tot - pf)).astype(BF16)
        if hsz % SUBLANES == 0:
            nb = ck // (2 * hsz)
            late = lambda m: m.reshape(nb, 2, hsz, m.shape[-1])[:, 1:2]
            ql = (late(q) * jnp.exp2(late(pf))).astype(BF16).reshape(ck // 2, LANES)
            p_lev = _dot_nt(ql, kl).reshape(nb, 1, hsz, ck)
            a4 = a.reshape(nb, 2, hsz, ck)
            a = jnp.concatenate([a4[:, 0:1], jnp.where(late(lvl) == lev, p_lev, a4[:, 1:2])],
                                axis=1).reshape(ck, ck)
            t4 = tot.reshape(nb, 2, hsz, LANES)
            p4 = pf.reshape(nb, 2, hsz, LANES)
            both = t4[:, 0:1] + t4[:, 1:2]
            tot = jnp.concatenate([both, both], axis=1).reshape(ck, LANES)
            pf = jnp.concatenate([p4[:, 0:1], p4[:, 1:2] + t4[:, 0:1]], axis=1).reshape(ck, LANES)
        else:
            ql = (q * jnp.exp2(pf)).astype(BF16)
            a = jnp.where(lvl == lev, _dot_nt(ql, kl), a)
            second = (row & hsz) != 0
            t3 = tot.reshape(ck // SUBLANES, SUBLANES, LANES)
            other = pltpu.roll(t3, hsz, 1)
            if 2 * hsz < SUBLANES:
                other = jnp.where(second.reshape(t3.shape), other, pltpu.roll(t3, SUBLANES - hsz, 1))
            other = other.reshape(ck, LANES)
            pf = pf + jnp.where(second, other, 0.0)
            tot = tot + other
    st = st_ref[...]
    o = _dot(a.astype(BF16), vb) + jnp.sum(q * k, axis=1, keepdims=True) * v
    o = o + _dot_nt((q * jnp.exp2(pf)).astype(BF16), st.astype(BF16))
    kst = (k * jnp.exp2(tot - pf)).astype(BF16)
    st_ref[...] = st * jnp.exp2(tot[0:1, :]) + _dot(v.T.astype(BF16), kst)
    y = o * lax.rsqrt(jnp.mean(o * o, axis=-1, keepdims=True) + EPS) * ng_ref[...]
    return y * (hg * jax.nn.sigmoid(hg))


def _hgrn_body(hq_ref, hz_ref, hi_ref, hg_ref, lb_ref, ng_ref, s0_ref, o_ref, s_ref, st_ref, lvl_ref, *, ck,
               layer):
    l = pl.program_id(2)
    tl = hq_ref.shape[0]
    hp = st_ref.shape[0]

    @pl.when(l == 0)
    def _():
        for i in range(hp):
            st_ref[i] = s0_ref[i].T

    lb, one_mlb, row = _hgrn_setup(lb_ref, lvl_ref, ck, layer)

    def chunk(c, carry):
        rs = slice(c * ck, (c + 1) * ck) if isinstance(c, int) else pl.ds(pl.multiple_of(c * ck, ck), ck)
        for i in range(hp):
            cs = slice(i * LANES, (i + 1) * LANES)
            y = _hgrn_chunk(hq_ref[rs, cs], hz_ref[rs, cs], hi_ref[rs, cs], hg_ref[rs, cs],
                            (lb[:, cs], one_mlb[:, cs], row), lvl_ref, ng_ref, st_ref.at[i])
            o_ref[rs, cs] = y.astype(o_ref.dtype)
        return carry

    n_chunks = tl // ck
    if n_chunks == 1:
        chunk(0, 0)
    else:
        lax.fori_loop(0, n_chunks, chunk, 0, unroll=4 if n_chunks % 4 == 0 else 1)

    @pl.when(l == pl.num_programs(2) - 1)
    def _():
        for i in range(hp):
            s_ref[i] = st_ref[i].T


def _hgrn(h, lb_logits, norm_g, s0, layer, n_heads, tl=2048, ck=128):
    B, L, _ = h.shape
    H = n_heads
    tl = _pick(L, tl)
    ck = _pick(tl, ck)
    assert ck & (ck - 1) == 0 and ck >= 8
    depth = lb_logits.shape[0]
    hp = HGRN_HEADS_PER_STEP if (L <= LANES and H % HGRN_HEADS_PER_STEP == 0) else 1
    ng = H // hp
    blk = lambda part: pl.BlockSpec((None, tl, hp * LANES), lambda b, g, l: (b, l, part * ng + g))
    return pl.pallas_call(
        functools.partial(_hgrn_body, ck=ck, layer=layer),
        grid=(B, ng, L // tl),
        in_specs=[
            blk(0), blk(1), blk(2), blk(3),
            pl.BlockSpec((depth, hp * LANES), lambda b, g, l: (0, g)),
            pl.BlockSpec((1, LANES), lambda b, g, l: (0, 0)),
            pl.BlockSpec((None, hp, LANES, LANES), lambda b, g, l: (b, g, 0, 0)),
        ],
        out_specs=[
            pl.BlockSpec((None, tl, hp * LANES), lambda b, g, l: (b, l, g)),
            pl.BlockSpec((None, hp, LANES, LANES), lambda b, g, l: (b, g, 0, 0)),
        ],
        out_shape=[
            jax.ShapeDtypeStruct((B, L, H * LANES), BF16),
            jax.ShapeDtypeStruct((B, H, LANES, LANES), F32),
        ],
        scratch_shapes=[pltpu.VMEM((hp, LANES, LANES), F32), pltpu.VMEM((ck, ck), jnp.int32)],
        compiler_params=_cparams("parallel", "parallel", "arbitrary"),
        name="hgrn_scan",
    )(h, h, h, h, lb_logits, norm_g, s0)


def _hgrn_proj_body(x_ref, w_ref, lb_ref, ng_ref, s0_ref, o_ref, s_ref, st_ref, lvl_ref, h_ref, *, ck, layer,
                    rows):
    L = x_ref.shape[0]
    hp = w_ref.shape[0]
    lb, one_mlb, row = _hgrn_setup(lb_ref, lvl_ref, ck, layer)
    n_proj = L // rows

    def project(i, s):
        h_ref[i, s % 2] = _dot(x_ref[s * rows:(s + 1) * rows, :], w_ref[i])

    project(0, 0)
    for i in range(hp):
        cs = slice(i * LANES, (i + 1) * LANES)
        consts = (lb[:, cs], one_mlb[:, cs], row)
        st = st_ref.at[i]
        st[...] = s0_ref[i].T
        for s in range(n_proj):
            if s + 1 < n_proj:
                project(i, s + 1)
            elif i + 1 < hp:
                project(i + 1, 0)
            h = h_ref.at[i, s % 2]
            for c in range(rows // ck):
                rs = slice(c * ck, (c + 1) * ck)
                y = _hgrn_chunk(h[rs, 0:LANES], h[rs, LANES:2 * LANES], h[rs, 2 * LANES:3 * LANES],
                                h[rs, 3 * LANES:4 * LANES], consts, lvl_ref, ng_ref, st)
                o_ref[s * rows + c * ck:s * rows + (c + 1) * ck, cs] = y.astype(o_ref.dtype)
        s_ref[i] = st[...].T


def _hgrn_proj(x, w_heads, lb_logits, norm_g, s0, layer, ck=128, rows=512):
    B, L, D = x.shape
    H = w_heads.shape[0]
    assert L % rows == 0 and rows % ck == 0 and ck & (ck - 1) == 0
    depth = lb_logits.shape[0]
    hp = HGRN_PROJ_HEADS if H % HGRN_PROJ_HEADS == 0 else 1
    return pl.pallas_call(
        functools.partial(_hgrn_proj_body, ck=ck, layer=layer, rows=rows),
        grid=(B, H // hp),
        in_specs=[
            pl.BlockSpec((None, L, D), lambda b, g: (b, 0, 0)),
            pl.BlockSpec((hp, D, 4 * LANES), lambda b, g: (g, 0, 0)),
            pl.BlockSpec((depth, hp * LANES), lambda b, g: (0, g)),
            pl.BlockSpec((1, LANES), lambda b, g: (0, 0)),
            pl.BlockSpec((None, hp, LANES, LANES), lambda b, g: (b, g, 0, 0)),
        ],
        out_specs=[
            pl.BlockSpec((None, L, hp * LANES), lambda b, g: (b, 0, g)),
            pl.BlockSpec((None, hp, LANES, LANES), lambda b, g: (b, g, 0, 0)),
        ],
        out_shape=[
            jax.ShapeDtypeStruct((B, L, H * LANES), BF16),
            jax.ShapeDtypeStruct((B, H, LANES, LANES), F32),
        ],
        scratch_shapes=[
            pltpu.VMEM((hp, LANES, LANES), F32),
            pltpu.VMEM((ck, ck), jnp.int32),
            pltpu.VMEM((hp, 2, rows, 4 * LANES), F32),
        ],
        compiler_params=_cparams("parallel", "arbitrary"),
        name="hgrn_proj_scan",
    )(x, w_heads, lb_logits, norm_g, s0)


def _trunk(x, p, k_cache, v_cache, pool_hist, hg_state, W, attn_prep):
    B, L, D = x.shape
    depth = W["ln_g"].shape[0]
    alpha = (2 * depth) ** 0.25
    T = B * L
    xt = x.reshape(T, D)
    new_k, new_v, new_pool, new_s = [], [], [], []
    pool_width = W["pool_scale"].shape[-1]
    da_width = D - pool_width
    n_da_heads = da_width // LANES
    n_hg_heads = D // LANES
    lnrow = lambda a, i, s: a[i, s].reshape(1, D)
    for i in range(depth):
        fuse_proj = i % 2 == 1 and L % HGRN_PROJ_ROWS == 0
        want16 = fuse_proj or i % 2 == 0
        xt = _ffn_ln(xt, W["wg"], W["wu"], W["wd"], lnrow(W["ln_g"], i, 0), lnrow(W["ln_b"], i, 0), i, 0, alpha,
                     also_bf16=want16)
        if want16:
            xt, xt16 = xt
        if i % 2 == 0:
            e = i // 2
            w_in = W["w_in_even"]
            (u,) = _proj(xt16, w_in, e, 0, pool_width, B, L, ("f32",))
            (q16,) = _proj(xt16, w_in, e, pool_width, da_width, B, L, ("bf16",), scale=(LANES // 2) ** -0.5)
            k, k16 = _proj(xt16, w_in, e, pool_width + da_width, da_width, B, L, ("heads", "bf16"))
            v, v16 = _proj(xt16, w_in, e, pool_width + 2 * da_width, da_width, B, L, ("heads", "bf16"))
            past = 0 if k_cache is None else k_cache.shape[2]
            pool_out, nh = _pool(u.reshape(B, L, pool_width), pool_hist[e], W["pool_w"][e],
                                 W["pool_scale"][e].reshape(1, pool_width), past)
            bias, lam = attn_prep[e]
            lam_init = 0.8 - 0.6 * math.exp(-0.3 * i)
            g = W["diff_norm_g"][e].reshape(1, LANES)
            q3, k3, v3 = (a.reshape(B, L, da_width) for a in (q16, k16, v16))
            if k_cache is None:
                o = _attn_prompt(q3, k3, v3, bias, lam, g, 1.0 - lam_init)
            else:
                kc = jnp.transpose(k_cache[e], (0, 2, 1, 3))
                vc = jnp.transpose(v_cache[e], (0, 2, 1, 3))
                o = _attn_sample(q3, k3, v3, kc, vc, bias, lam, g, 1.0 - lam_init)
            parts = [pool_out.reshape(T, pool_width), o.reshape(T, da_width)]
            w_out = W["w_out_even"]
            new_k.append(jnp.transpose(k, (0, 2, 1, 3)))
            new_v.append(jnp.transpose(v, (0, 2, 1, 3)))
            new_pool.append(nh)
            li = e
        else:
            od = i // 2
            ng = W["hgrn_norm_g"][od].reshape(1, LANES)
            if fuse_proj:
                o, s = _hgrn_proj(xt16.reshape(B, L, D), W["w_in_odd_heads"][od], W["lb_logits"], ng,
                                  hg_state[od], i, rows=HGRN_PROJ_ROWS)
            else:
                h = _matmul(xt, W["w_in_odd"], od)
                o, s = _hgrn(h.reshape(B, L, 4 * D), W["lb_logits"], ng, hg_state[od], i, n_hg_heads)
            parts = [o.reshape(T, D)]
            w_out = W["w_out_odd"]
            new_s.append(s)
            li = od
        xt = _out_ln(parts, xt, w_out, li, lnrow(W["ln_g"], i, 1), lnrow(W["ln_b"], i, 1), alpha)
        xt = _ffn_ln(xt, W["wg"], W["wu"], W["wd"], lnrow(W["ln_g"], i, 2), lnrow(W["ln_b"], i, 2), i, 1, alpha,
                     ple=(p.reshape(depth, T, -1), W["w_ple_gate"], W["w_ple_up"]))
    return xt.reshape(B, L, D), jnp.stack(new_k), jnp.stack(new_v), jnp.stack(new_pool), jnp.stack(new_s)


ATTN_TILE = 256
HGRN_PROJ_ROWS = 512


def kernel(x_prompt, x_sample, cache_diff_k, cache_diff_v, state_pool, state_hgrn, p_prompt, p_sample, ln_g, ln_b, w_ffn_gate, w_ffn_up, w_ffn_down, w_ple_gate, w_ple_up, w_in_even, w_out_even, pool_w, pool_scale, lam_q1, lam_k1, lam_q2, lam_k2, diff_norm_g, rel_bias, w_in_odd, w_out_odd, hgrn_norm_g, hgrn_lb_logits):
    bf = lambda a: a.astype(BF16)
    W = dict(
        ln_g=ln_g.astype(F32), ln_b=ln_b.astype(F32),
        wg=bf(w_ffn_gate), wu=bf(w_ffn_up), wd=bf(w_ffn_down),
        w_ple_gate=bf(w_ple_gate), w_ple_up=bf(w_ple_up),
        w_in_even=bf(w_in_even), w_out_even=bf(w_out_even),
        pool_w=bf(pool_w), pool_scale=pool_scale.astype(F32),
        diff_norm_g=diff_norm_g.astype(F32),
        w_in_odd=bf(w_in_odd), w_out_odd=bf(w_out_odd),
        hgrn_norm_g=hgrn_norm_g.astype(F32), lb_logits=hgrn_lb_logits.astype(F32),
    )
    n_even = w_in_even.shape[0]
    n_odd = w_in_odd.shape[0]
    d_model = w_in_odd.shape[1]
    n_hg = d_model // LANES
    W["w_in_odd_heads"] = jnp.transpose(W["w_in_odd"].reshape(n_odd, d_model, 4, n_hg, LANES),
                                        (0, 3, 1, 2, 4)).reshape(n_odd, n_hg, d_model, 4 * LANES)
    attn_prep = []
    for e in range(n_even):
        lam_init = 0.8 - 0.6 * math.exp(-0.3 * (2 * e))
        attn_prep.append(_attn_prep(rel_bias, lam_q1[e], lam_k1[e], lam_q2[e], lam_k2[e], ATTN_TILE, lam_init))

    B = x_prompt.shape[0]
    dt = x_prompt.dtype
    zero_pool = jnp.zeros((n_even, B) + state_pool.shape[2:], dt)
    zero_s = jnp.zeros((n_odd, B) + state_hgrn.shape[2:], dt)
    y_p, k_p, v_p, pool_p, s_p = _trunk(x_prompt, p_prompt, None, None, zero_pool, zero_s, W, attn_prep)
    y_s, k_s, v_s, pool_s, s_s = _trunk(x_sample, p_sample, cache_diff_k, cache_diff_v, state_pool, state_hgrn,
                                        W, attn_prep)
    return (y_p, y_s, k_p, v_p, k_s, v_s, pool_p, pool_s, s_p, s_s)
```
